```python
import math
import jax, jax.numpy as jnp
from jax import lax
import numpy as np

D_MODEL = 1024
BATCH = 16
SEQ = 2048
DEPTH = 4

D_MIX = D_MODEL
D_NSA = D_MIX // 2
D_CONV = D_MIX - D_NSA
HEAD_DIM = 64
N_HEADS = D_NSA // HEAD_DIM
N_KV = 2
Q_PER_KV = N_HEADS // N_KV
D_KV = N_KV * HEAD_DIM
N_BRANCH = 3
L_CMP = 32
CMP_STRIDE = 16
L_SEL = 64
N_SEL = 8
WINDOW = 512
Q_BLOCK = 128
SEL_Q_BLOCK = 64
CONV_WIDTH = 31
NUM_BUCKETS = 32
MAX_DISTANCE = 128
D_IN = D_NSA + 6 * D_KV + N_HEADS * N_BRANCH + 2 * D_CONV
D_FF = 2816
N_EXPERTS = 8
TOP_K = 2
D_FF_EXPERT = 3584
N_DENSE = DEPTH - DEPTH // 2
N_MOE = DEPTH // 2
DEEPNORM_ALPHA = (2 * DEPTH) ** 0.25
DEEPNORM_BETA = (8 * DEPTH) ** -0.25
LN_EPS = 1e-5
NEG_INF = -1e30

kernel_name = 'hybrid_nsa_conformer_moe_trunk'


def layer_norm(x, g, b):
    xf = x.astype(jnp.float32)
    mu = jnp.mean(xf, axis=-1, keepdims=True)
    var = jnp.mean(jnp.square(xf - mu), axis=-1, keepdims=True)
    y = (xf - mu) * lax.rsqrt(var + LN_EPS)
    return (y * g.astype(jnp.float32) + b.astype(jnp.float32)).astype(x.dtype)


def masked_softmax(logits, valid):
    p = jax.nn.softmax(jnp.where(valid, logits, NEG_INF), axis=-1)
    return jnp.where(valid, p, 0.0)


def rel_bucket(dist):
    n = jnp.maximum(dist, 0)
    max_exact = NUM_BUCKETS // 2
    nf = jnp.maximum(n, 1).astype(jnp.float32)
    large = max_exact + (jnp.log(nf / max_exact) / math.log(MAX_DISTANCE / max_exact)
                         * (NUM_BUCKETS - max_exact)).astype(jnp.int32)
    large = jnp.minimum(large, NUM_BUCKETS - 1)
    return jnp.where(n < max_exact, n, large)


def compress(k, pos, w1, w2):
    B, T = k.shape[:2]
    n_cmp = (T - L_CMP) // CMP_STRIDE + 1
    idx = CMP_STRIDE * np.arange(n_cmp)[:, None] + np.arange(L_CMP)[None, :]
    blk = k[:, idx] + pos[None, None, :, None, :]
    blk = blk.transpose(0, 1, 3, 2, 4).reshape(B, n_cmp, N_KV, L_CMP * HEAD_DIM)
    return jax.nn.gelu(blk @ w1) @ w2


def compressed_branch(q5, kc, vc, rel_bias):
    T = q5.shape[1]
    n_cmp = kc.shape[1]
    starts = CMP_STRIDE * np.arange(n_cmp)
    ends = starts + L_CMP - 1
    dist = jnp.asarray(np.arange(T)[:, None] - ends[None, :], dtype=jnp.int32)
    valid = dist >= 0
    bias = rel_bias[rel_bucket(dist)].reshape(T, n_cmp, N_KV, Q_PER_KV).transpose(2, 3, 0, 1)
    logits = jnp.einsum('btgrd,bngd->bgrtn', q5, kc).astype(jnp.float32) + bias.astype(jnp.float32)
    p = masked_softmax(logits, valid)
    o = jnp.einsum('bgrtn,bngd->btgrd', p.astype(vc.dtype), vc)
    nb = T // L_SEL
    bstart = L_SEL * np.arange(nb)
    overlap = ((starts[:, None] < bstart[None, :] + L_SEL)
               & (starts[:, None] + L_CMP > bstart[None, :])).astype(np.float32)
    imp = jnp.einsum('bgrtn,nj->bgtj', p, jnp.asarray(overlap))
    return o, imp


def selected_branch(q5, ks, vs, imp, rel_bias):
    B, T = q5.shape[:2]
    nb = T // L_SEL
    n_sel = min(N_SEL, nb)
    tb = np.arange(T) // L_SEL
    j = np.arange(nb)
    future = jnp.asarray(j[None, :] > tb[:, None])
    forced = jnp.asarray((j[None, :] == 0) | (j[None, :] == tb[:, None]) | (j[None, :] == tb[:, None] - 1))
    score = jnp.where(future, -jnp.inf, jnp.where(forced, jnp.inf, imp))
    _, idx = lax.top_k(score, n_sel)

    def to_blocks(a):
        return a.reshape(B, nb, L_SEL, N_KV, HEAD_DIM).transpose(0, 3, 1, 2, 4).reshape(B, N_KV, nb, L_SEL * HEAD_DIM)

    kb, vb = to_blocks(ks), to_blocks(vs)
    nc = T // SEL_Q_BLOCK
    qc = q5.reshape(B, nc, SEL_Q_BLOCK, N_KV, Q_PER_KV, HEAD_DIM).transpose(1, 0, 2, 3, 4, 5)
    ic = idx.reshape(B, N_KV, nc, SEL_Q_BLOCK, n_sel).transpose(2, 0, 1, 3, 4)
    t0s = jnp.arange(nc, dtype=jnp.int32) * SEL_Q_BLOCK
    rel_g = rel_bias.reshape(NUM_BUCKETS, N_KV, Q_PER_KV)
    b_ix = jnp.arange(B)[:, None, None]
    g_ix = jnp.arange(N_KV)[None, :, None]
    g_ix5 = jnp.arange(N_KV)[None, :, None, None, None]

    def chunk(args):
        qb, ib, t0 = args
        tq = t0 + jnp.arange(SEL_Q_BLOCK)
        flat = ib.reshape(B, N_KV, SEL_Q_BLOCK * n_sel)
        kg = kb[b_ix, g_ix, flat].reshape(B, N_KV, SEL_Q_BLOCK, n_sel, L_SEL, HEAD_DIM)
        vg = vb[b_ix, g_ix, flat].reshape(B, N_KV, SEL_Q_BLOCK, n_sel, L_SEL, HEAD_DIM)
        pos = ib[..., None] * L_SEL + jnp.arange(L_SEL)
        dist = tq[None, None, :, None, None] - pos
        bias = jnp.moveaxis(rel_g[rel_bucket(dist), g_ix5], -1, 3)
        logits = jnp.einsum('bqgrd,bgqskd->bgqrsk', qb, kg).astype(jnp.float32) + bias.astype(jnp.float32)
        logits = logits.reshape(B, N_KV, SEL_Q_BLOCK, Q_PER_KV, n_sel * L_SEL)
        valid = (dist >= 0).reshape(B, N_KV, SEL_Q_BLOCK, 1, n_sel * L_SEL)
        p = masked_softmax(logits, valid).reshape(B, N_KV, SEL_Q_BLOCK, Q_PER_KV, n_sel, L_SEL)
        return jnp.einsum('bgqrsk,bgqskd->bqgrd', p.astype(vg.dtype), vg)

    o = lax.map(chunk, (qc, ic, t0s))
    return o.transpose(1, 0, 2, 3, 4, 5).reshape(B, T, N_KV, Q_PER_KV, HEAD_DIM)


def window_branch(q5, kw, vw, rel_bias):
    B, T = q5.shape[:2]
    nqb = T // Q_BLOCK
    kp = jnp.pad(kw, ((0, 0), (WINDOW, 0), (0, 0), (0, 0)))
    vp = jnp.pad(vw, ((0, 0), (WINDOW, 0), (0, 0), (0, 0)))
    qw = q5.reshape(B, nqb, Q_BLOCK, N_KV, Q_PER_KV, HEAD_DIM).transpose(1, 0, 2, 3, 4, 5)
    t0s = jnp.arange(nqb, dtype=jnp.int32) * Q_BLOCK

    def block(args):
        qb, t0 = args
        kband = lax.dynamic_slice_in_dim(kp, t0, Q_BLOCK + WINDOW, axis=1)
        vband = lax.dynamic_slice_in_dim(vp, t0, Q_BLOCK + WINDOW, axis=1)
        tq = t0 + jnp.arange(Q_BLOCK)
        pos = t0 - WINDOW + jnp.arange(Q_BLOCK + WINDOW)
        dist = tq[:, None] - pos[None, :]
        valid = (dist >= 0) & (dist < WINDOW) & (pos[None, :] >= 0)
        bias = rel_bias[rel_bucket(dist)].reshape(Q_BLOCK, Q_BLOCK + WINDOW, N_KV, Q_PER_KV).transpose(2, 3, 0, 1)
        logits = jnp.einsum('bqgrd,bkgd->bgrqk', qb, kband).astype(jnp.float32) + bias.astype(jnp.float32)
        p = masked_softmax(logits, valid)
        return jnp.einsum('bgrqk,bkgd->bqgrd', p.astype(vband.dtype), vband)

    o = lax.map(block, (qw, t0s))
    return o.transpose(1, 0, 2, 3, 4, 5).reshape(B, T, N_KV, Q_PER_KV, HEAD_DIM)


def conformer_conv(u, w, b, g, beta):
    a, gt = jnp.split(u, 2, axis=-1)
    y = a * jax.nn.sigmoid(gt)
    y = lax.conv_general_dilated(y, w[:, None, :], window_strides=(1,), padding=[(CONV_WIDTH - 1, 0)],
                                 dimension_numbers=('NWC', 'WIO', 'NWC'), feature_group_count=D_CONV) + b
    return jax.nn.silu(layer_norm(y, g, beta))


def hybrid_mixer(h, w_in, cmp_pos, cmp_w1, cmp_w2, conv_w, conv_b, conv_ln_g, conv_ln_b, w_out, rel_bias):
    B, T, _ = h.shape
    u = h @ w_in
    splits = [int(s) for s in np.cumsum([D_NSA] + [D_KV] * 6 + [N_HEADS * N_BRANCH])]
    q, kc, vc, ks, vs, kw, vw, gl, conv_in = jnp.split(u, splits, axis=-1)

    def kv(a):
        return a.reshape(B, T, N_KV, HEAD_DIM)

    q5 = q.reshape(B, T, N_KV, Q_PER_KV, HEAD_DIM) * (HEAD_DIM ** -0.5)
    kcmp = compress(kv(kc), cmp_pos[0], cmp_w1[0], cmp_w2[0])
    vcmp = compress(kv(vc), cmp_pos[1], cmp_w1[1], cmp_w2[1])
    o_cmp, imp = compressed_branch(q5, kcmp, vcmp, rel_bias)
    o_slc = selected_branch(q5, kv(ks), kv(vs), imp, rel_bias)
    o_win = window_branch(q5, kv(kw), kv(vw), rel_bias)
    g = jax.nn.sigmoid(gl.reshape(B, T, N_KV, Q_PER_KV, N_BRANCH))
    o_nsa = g[..., 0:1] * o_cmp + g[..., 1:2] * o_slc + g[..., 2:3] * o_win
    o_conv = conformer_conv(conv_in, conv_w, conv_b, conv_ln_g, conv_ln_b)
    return jnp.concatenate([o_nsa.reshape(B, T, D_NSA), o_conv], axis=-1) @ w_out


def swiglu(h, w1, w2):
    gate, up = jnp.split(h @ w1, 2, axis=-1)
    return (jax.nn.silu(gate) * up) @ w2


def moe_ffn(h, router_w, w1, w2):
    B, T, D = h.shape
    hf = h.reshape(B * T, D)
    logits = (hf @ router_w).astype(jnp.float32)
    top_vals, top_idx = lax.top_k(logits, TOP_K)
    wts = jax.nn.softmax(top_vals, axis=-1)
    out = jnp.zeros_like(hf)
    for e in range(N_EXPERTS):
        we = jnp.sum(jnp.where(top_idx == e, wts, 0.0), axis=-1).astype(hf.dtype)
        out = out + we[:, None] * swiglu(hf, w1[e], w2[e])
    return out.reshape(B, T, D)


def setup_inputs(seed: int = 0) -> dict:
    key = jax.random.key(seed)
    ks = jax.random.split(key, 24)
    nrm = lambda k, s, sc: jax.random.normal(k, s, jnp.float32) * sc
    return {
        'x': nrm(ks[0], (BATCH, SEQ, D_MODEL), 1.0),
        'c': nrm(ks[1], (BATCH, D_MODEL), 1.0),
        'w_in': nrm(ks[2], (DEPTH, D_MODEL, D_IN), D_MODEL ** -0.5),
        'cmp_pos': nrm(ks[3], (DEPTH, 2, L_CMP, HEAD_DIM), 0.1),
        'cmp_w1': nrm(ks[4], (DEPTH, 2, L_CMP * HEAD_DIM, HEAD_DIM), (L_CMP * HEAD_DIM) ** -0.5),
        'cmp_w2': nrm(ks[5], (DEPTH, 2, HEAD_DIM, HEAD_DIM), HEAD_DIM ** -0.5),
        'conv_w': nrm(ks[6], (DEPTH, CONV_WIDTH, D_CONV), CONV_WIDTH ** -0.5),
        'conv_b': nrm(ks[7], (DEPTH, D_CONV), 0.02),
        'conv_ln_g': 1.0 + nrm(ks[8], (DEPTH, D_CONV), 0.02),
        'conv_ln_b': nrm(ks[9], (DEPTH, D_CONV), 0.02),
        'w_out': nrm(ks[10], (DEPTH, D_MIX, D_MODEL), D_MIX ** -0.5 * DEEPNORM_BETA),
        'rel_bias': nrm(ks[11], (NUM_BUCKETS, N_HEADS), 0.5),
        'ada_w': nrm(ks[12], (DEPTH, D_MODEL, 6 * D_MODEL), 0.2 * D_MODEL ** -0.5),
        'ada_b': nrm(ks[13], (DEPTH, 6 * D_MODEL), 0.02),
        'ln_g': 1.0 + nrm(ks[14], (DEPTH, 2, D_MODEL), 0.02),
        'ln_b': nrm(ks[15], (DEPTH, 2, D_MODEL), 0.02),
        'ffn_w1': nrm(ks[16], (N_DENSE, D_MODEL, 2 * D_FF), D_MODEL ** -0.5),
        'ffn_w2': nrm(ks[17], (N_DENSE, D_FF, D_MODEL), D_FF ** -0.5 * DEEPNORM_BETA),
        'router_w': nrm(ks[18], (N_MOE, D_MODEL, N_EXPERTS), D_MODEL ** -0.5),
        'moe_w1': nrm(ks[19], (N_MOE, N_EXPERTS, D_MODEL, 2 * D_FF_EXPERT), D_MODEL ** -0.5),
        'moe_w2': nrm(ks[20], (N_MOE, N_EXPERTS, D_FF_EXPERT, D_MODEL), D_FF_EXPERT ** -0.5 * DEEPNORM_BETA),
    }


def reference(x, c, w_in, cmp_pos, cmp_w1, cmp_w2, conv_w, conv_b, conv_ln_g, conv_ln_b, w_out, rel_bias,
              ada_w, ada_b, ln_g, ln_b, ffn_w1, ffn_w2, router_w, moe_w1, moe_w2):
    sc = jax.nn.silu(c)
    for l in range(DEPTH):
        mod = sc @ ada_w[l] + ada_b[l]
        shift1, scale1, gate1, shift2, scale2, gate2 = [m[:, None, :] for m in jnp.split(mod, 6, axis=-1)]
        h = x * (1.0 + scale1) + shift1
        y = hybrid_mixer(h, w_in[l], cmp_pos[l], cmp_w1[l], cmp_w2[l], conv_w[l], conv_b[l],
                         conv_ln_g[l], conv_ln_b[l], w_out[l], rel_bias)
        x = layer_norm(DEEPNORM_ALPHA * x + (1.0 + gate1) * y, ln_g[l, 0], ln_b[l, 0])
        h = x * (1.0 + scale2) + shift2
        if l % 2 == 0:
            f = swiglu(h, ffn_w1[l // 2], ffn_w2[l // 2])
        else:
            f = moe_ffn(h, router_w[l // 2], moe_w1[l // 2], moe_w2[l // 2])
        x = layer_norm(DEEPNORM_ALPHA * x + (1.0 + gate2) * f, ln_g[l, 1], ln_b[l, 1])
    return x
```

```python
import functools
import math

import numpy as np
import jax
import jax.numpy as jnp
from jax import lax
from jax.experimental import pallas as pl
from jax.experimental.pallas import tpu as pltpu

F32 = jnp.float32
BF16 = jnp.bfloat16
HIGHEST = lax.Precision.HIGHEST

HEAD_DIM = 64
N_HEADS = 8
N_KV = 2
Q_PER_KV = N_HEADS // N_KV
N_BRANCH = 3
L_CMP = 32
CMP_STRIDE = 16
L_SEL = 64
N_SEL = 8
WINDOW = 512
CONV_WIDTH = 31
NUM_BUCKETS = 32
MAX_DISTANCE = 128
N_EXPERTS = 8
LN_EPS = 1e-5
NEG_INF = -1e30

LANE = 128
ATT_TQ = 128
ATT_ROWS = Q_PER_KV * ATT_TQ
CONV_TQ = 256
CONV_HALO = 32
CONV_RC = 64
VMEM_LIMIT_BYTES = 52 * 1024 * 1024


def _cparams(*sem):
    return pltpu.CompilerParams(dimension_semantics=sem, vmem_limit_bytes=VMEM_LIMIT_BYTES)


def _dot(a, b):
    return jnp.dot(a, b, preferred_element_type=F32)


def _dot_nt(a, b, precision=None):
    return lax.dot_general(a, b, (((1,), (1,)), ((), ())), precision=precision,
                           preferred_element_type=F32)


def _res_ln(x, gate, y, g, b, alpha):
    z = alpha * x + (1.0 + gate) * y
    mu = jnp.mean(z, axis=-1, keepdims=True)
    zc = z - mu
    var = jnp.mean(zc * zc, axis=-1, keepdims=True)
    return zc * lax.rsqrt(var + LN_EPS) * g + b


def _ada_kernel(c_ref, w_ref, b_ref, o_ref):
    sc = jax.nn.silu(c_ref[...])
    o_ref[0, 0] = jnp.dot(sc, w_ref[0], precision=HIGHEST, preferred_element_type=F32) + b_ref[0]


def _ada(c, ada_w, ada_b):
    depth, d, _ = ada_w.shape
    bsz = c.shape[0]
    return pl.pallas_call(
        _ada_kernel,
        out_shape=jax.ShapeDtypeStruct((depth, 6, bsz, d), F32),
        grid=(depth, 6),
        in_specs=[pl.BlockSpec((bsz, d), lambda l, k: (0, 0)),
                  pl.BlockSpec((1, d, d), lambda l, k: (l, 0, k)),
                  pl.BlockSpec((1, 1, d), lambda l, k: (l, 0, k))],
        out_specs=pl.BlockSpec((1, 1, bsz, d), lambda l, k: (l, k, 0, 0)),
        compiler_params=_cparams("parallel", "parallel"),
        name="ada_mod",
    )(c, ada_w, ada_b.reshape(depth, 1, 6 * d))


def _inproj_kernel(x_ref, sc_ref, sh_ref, w_ref, q_ref, kvc_ref, kvsw_ref, gl_ref, cv_ref):
    h = (x_ref[...] * (1.0 + sc_ref[0]) + sh_ref[0]).astype(BF16)
    q_ref[...] = (_dot(h, w_ref[:, 0:512]) * (HEAD_DIM ** -0.5)).astype(BF16)
    kvc_ref[...] = _dot(h, w_ref[:, 512:768])
    kvsw_ref[...] = _dot(h, w_ref[:, 768:1280]).astype(BF16)
    gl_ref[...] = _dot(h, w_ref[:, 1280:1536])
    cv_ref[...] = _dot(h, w_ref[:, 1536:2560])


def _inproj(x2, scale, shift, w, seq, tm):
    n, d = x2.shape
    per_b = seq // tm
    row = lambda i: (i, 0)
    mod = lambda i: (i // per_b, 0, 0)
    return pl.pallas_call(
        _inproj_kernel,
        out_shape=(jax.ShapeDtypeStruct((n, 512), BF16), jax.ShapeDtypeStruct((n, 256), F32),
                   jax.ShapeDtypeStruct((n, 512), BF16), jax.ShapeDtypeStruct((n, 256), F32),
                   jax.ShapeDtypeStruct((n, 1024), F32)),
        grid=(n // tm,),
        in_specs=[pl.BlockSpec((tm, d), row), pl.BlockSpec((1, 1, d), mod), pl.BlockSpec((1, 1, d), mod),
                  pl.BlockSpec(w.shape, lambda i: (0, 0))],
        out_specs=(pl.BlockSpec((tm, 512), row), pl.BlockSpec((tm, 256), row), pl.BlockSpec((tm, 512), row),
                   pl.BlockSpec((tm, 256), row), pl.BlockSpec((tm, 1024), row)),
        compiler_params=_cparams("parallel"),
        name="in_proj",
    )(x2, scale, shift, w)


def _compress_kernel(a_ref, plo_ref, phi_ref, wlo_ref, whi_ref, w2_ref, o_ref):
    a = a_ref[0]
    lo = _dot((a + plo_ref[...]).astype(BF16), wlo_ref[...])
    hi = _dot((a + phi_ref[...]).astype(BF16), whi_ref[...])
    nrow = lo.shape[0]
    pre = lo + pltpu.roll(hi, nrow - 1, 0)
    act = jax.nn.gelu(pre, approximate=True)
    o_ref[0] = _dot(act.astype(BF16), w2_ref[...]).astype(BF16)


def _compress(kvc3, pos_lo, pos_hi, wlo, whi, w2bd):
    bsz, nrow, width = kvc3.shape
    const = lambda b: (0, 0)
    return pl.pallas_call(
        _compress_kernel,
        out_shape=jax.ShapeDtypeStruct((bsz, nrow, 256), BF16),
        grid=(bsz,),
        in_specs=[pl.BlockSpec((1, nrow, width), lambda b: (b, 0, 0)),
                  pl.BlockSpec((1, width), const), pl.BlockSpec((1, width), const),
                  pl.BlockSpec((width, 256), const), pl.BlockSpec((width, 256), const),
                  pl.BlockSpec((256, 256), const)],
        out_specs=pl.BlockSpec((1, nrow, 256), lambda b: (b, 0, 0)),
        compiler_params=_cparams("parallel"),
        name="compress_mlp",
    )(kvc3, pos_lo, pos_hi, wlo, whi, w2bd)


def _stack_heads(q):
    zeros = jnp.zeros((q.shape[0], HEAD_DIM), q.dtype)
    parts = []
    for r in range(Q_PER_KV):
        parts.append(jnp.concatenate([q[:, r * HEAD_DIM:(r + 1) * HEAD_DIM], zeros], axis=1))
    return jnp.concatenate(parts, axis=0)


def _gated_heads(o, gl, branch):
    tq = gl.shape[0]
    outs = []
    for r in range(Q_PER_KV):
        col = branch * Q_PER_KV + r
        gate = jax.nn.sigmoid(gl[:, col:col + 1])
        outs.append(o[r * tq:(r + 1) * tq, HEAD_DIM:] * gate)
    return jnp.concatenate(outs, axis=1)


def _cmp_attn_kernel(q_ref, kv_ref, tab_ref, gl_ref, o_ref, sel_ref, *, n_blocks, n_sel):
    qt = pl.program_id(2)
    qs = _stack_heads(q_ref[0])
    kv = kv_ref[0]
    bias = tab_ref[0, 0]
    s = _dot_nt(qs, kv) + bias
    valid = bias > 0.5 * NEG_INF
    m = jnp.max(s, axis=-1, keepdims=True)
    p = jnp.where(valid, jnp.exp(s - m), 0.0)
    l = jnp.sum(p, axis=-1, keepdims=True)
    p = p / jnp.where(l > 0.0, l, 1.0)
    o = _dot(p.astype(BF16), kv)
    o_ref[0] = _gated_heads(o, gl_ref[0], 0).astype(BF16)

    tq = ATT_TQ
    psum = p[0:tq] + p[tq:2 * tq] + p[2 * tq:3 * tq] + p[3 * tq:4 * tq]
    jj = lax.broadcasted_iota(jnp.int32, (n_blocks, LANE), 0)
    nn = lax.broadcasted_iota(jnp.int32, (n_blocks, LANE), 1)
    overlap_t = jnp.where((CMP_STRIDE * nn < L_SEL * jj + L_SEL) & (CMP_STRIDE * nn + L_CMP > L_SEL * jj),
                          1.0, 0.0).astype(F32)
    imp_t = _dot_nt(overlap_t, psum, precision=HIGHEST)

    jb = lax.broadcasted_iota(jnp.int32, (n_blocks, tq), 0)
    tb = (qt * tq + lax.broadcasted_iota(jnp.int32, (n_blocks, tq), 1)) // L_SEL
    score = jnp.where(jb > tb, -jnp.inf,
                      jnp.where((jb == 0) | (jb == tb) | (jb == tb - 1), jnp.inf, imp_t))
    rank = jnp.zeros((n_blocks, tq), F32)
    for jp in range(n_blocks):
        row = score[jp:jp + 1, :]
        earlier = jnp.where(jb > jp, 1.0, 0.0)
        rank = rank + jnp.where(row > score, 1.0, 0.0) + jnp.where(row == score, earlier, 0.0)
    sel_t = jnp.where(rank < float(n_sel), 1.0, 0.0).astype(BF16)
    ii = lax.broadcasted_iota(jnp.int32, (tq, tq), 0)
    kk = lax.broadcasted_iota(jnp.int32, (tq, tq), 1)
    eye = jnp.where(ii == kk, 1.0, 0.0).astype(BF16)
    sel_ref[0, 0] = _dot_nt(eye, sel_t).astype(BF16)


def _cmp_attn(q3, kvcmp, tab, gl3, n_blocks, n_sel):
    bsz, seq, _ = q3.shape
    nqt = seq // ATT_TQ
    kern = functools.partial(_cmp_attn_kernel, n_blocks=n_blocks, n_sel=n_sel)
    return pl.pallas_call(
        kern,
        out_shape=(jax.ShapeDtypeStruct((bsz, seq, 512), BF16),
                   jax.ShapeDtypeStruct((bsz, N_KV, seq, n_blocks), BF16)),
        grid=(bsz, N_KV, nqt),
        in_specs=[pl.BlockSpec((1, ATT_TQ, 256), lambda b, g, t: (b, t, g)),
                  pl.BlockSpec((1, kvcmp.shape[1], LANE), lambda b, g, t: (b, 0, g)),
                  pl.BlockSpec((1, 1, ATT_ROWS, LANE), lambda b, g, t: (g, t, 0, 0)),
                  pl.BlockSpec((1, ATT_TQ, LANE), lambda b, g, t: (b, t, g))],
        out_specs=(pl.BlockSpec((1, ATT_TQ, 256), lambda b, g, t: (b, t, g)),
                   pl.BlockSpec((1, 1, ATT_TQ, n_blocks), lambda b, g, t: (b, g, t, 0))),
        compiler_params=_cparams("parallel", "parallel", "parallel"),
        name="cmp_attn",
    )(q3, kvcmp, tab, gl3)


def _band_attn_kernel(*refs, n_delta, span, branch, use_sel):
    if use_sel:
        q_ref, kv_ref, tab_ref, gl_ref, sel_ref, o_ref = refs
    else:
        q_ref, kv_ref, tab_ref, gl_ref, o_ref = refs
    qt = pl.program_id(2)
    qs = _stack_heads(q_ref[0])
    tq = ATT_TQ
    if use_sel:
        sel = sel_ref[0, 0]
        nb = sel.shape[1]
        jj = lax.broadcasted_iota(jnp.int32, (nb, tq), 0)
        kb = lax.broadcasted_iota(jnp.int32, (nb, tq), 1) // L_SEL

    def body(c, carry):
        m, l, acc = carry
        kv = kv_ref[0, pl.ds(pl.multiple_of(c * tq, tq), tq), :]
        dd = jnp.minimum(qt - c, n_delta - 1)
        s = _dot_nt(qs, kv) + tab_ref[0, dd]
        if use_sel:
            expand = jnp.where(jj == c * (tq // L_SEL) + kb, 1.0, 0.0).astype(BF16)
            allowed = _dot(sel, expand)
            madd = (allowed - 1.0) * (-NEG_INF)
            s = s + jnp.concatenate([madd] * Q_PER_KV, axis=0)
        m_new = jnp.maximum(m, jnp.max(s, axis=-1, keepdims=True))
        alpha = jnp.exp(m - m_new)
        p = jnp.exp(s - m_new)
        l = alpha * l + jnp.sum(p, axis=-1, keepdims=True)
        acc = alpha * acc + _dot(p.astype(BF16), kv)
        return m_new, l, acc

    c_lo = jnp.maximum(qt - span, 0) if span is not None else 0
    init = (jnp.full((ATT_ROWS, 1), NEG_INF, F32), jnp.zeros((ATT_ROWS, 1), F32),
            jnp.zeros((ATT_ROWS, LANE), F32))
    m, l, acc = lax.fori_loop(c_lo, qt + 1, body, init)
    o = acc / l
    o_ref[0] = _gated_heads(o, gl_ref[0], branch).astype(BF16)


def _band_attn(q3, kvsw3, kv_block0, tab, gl3, sel, span, branch):
    bsz, seq, _ = q3.shape
    nqt = seq // ATT_TQ
    use_sel = sel is not None
    kern = functools.partial(_band_attn_kernel, n_delta=tab.shape[1], span=span, branch=branch,
                             use_sel=use_sel)
    in_specs = [pl.BlockSpec((1, ATT_TQ, 256), lambda b, g, t: (b, t, g)),
                pl.BlockSpec((1, seq, LANE), lambda b, g, t: (b, 0, kv_block0 + g)),
                pl.BlockSpec((1,) + tab.shape[1:], lambda b, g, t: (g, 0, 0, 0)),
                pl.BlockSpec((1, ATT_TQ, LANE), lambda b, g, t: (b, t, g))]
    args = [q3, kvsw3, tab, gl3]
    if use_sel:
        in_specs.append(pl.BlockSpec((1, 1, ATT_TQ, sel.shape[-1]), lambda b, g, t: (b, g, t, 0)))
        args.append(sel)
    return pl.pallas_call(
        kern,
        out_shape=jax.ShapeDtypeStruct((bsz, seq, 512), BF16),
        grid=(bsz, N_KV, nqt),
        in_specs=in_specs,
        out_specs=pl.BlockSpec((1, ATT_TQ, 256), lambda b, g, t: (b, t, g)),
        compiler_params=_cparams("parallel", "parallel", "parallel"),
        name="sel_attn" if use_sel else "win_attn",
    )(*args)


def _conv_kernel(cur_ref, prev_ref, w_ref, cb_ref, g_ref, b_ref, o_ref, ybuf, cbuf):
    i = pl.program_id(1)
    dc = o_ref.shape[-1]
    cur = cur_ref[0]
    prev = prev_ref[0]
    ybuf[CONV_HALO:, :] = cur[:, :dc] * jax.nn.sigmoid(cur[:, dc:])
    halo = prev[:, :dc] * jax.nn.sigmoid(prev[:, dc:])
    ybuf[0:CONV_HALO, :] = jnp.where(i > 0, halo, 0.0)
    off = CONV_HALO - (CONV_WIDTH - 1)
    for lc in range(dc // LANE):
        cols = slice(lc * LANE, (lc + 1) * LANE)
        for rc in range(CONV_TQ // CONV_RC):
            acc = jnp.zeros((CONV_RC, LANE), F32)
            for k in range(CONV_WIDTH):
                r0 = rc * CONV_RC + off + k
                acc = acc + w_ref[k:k + 1, cols] * ybuf[r0:r0 + CONV_RC, cols]
            cbuf[rc * CONV_RC:(rc + 1) * CONV_RC, cols] = acc
    z = cbuf[...] + cb_ref[...]
    mu = jnp.mean(z, axis=-1, keepdims=True)
    zc = z - mu
    var = jnp.mean(zc * zc, axis=-1, keepdims=True)
    y = zc * lax.rsqrt(var + LN_EPS) * g_ref[...] + b_ref[...]
    o_ref[0] = jax.nn.silu(y).astype(BF16)


def _conv_branch(cv3, w, cb, g, b):
    bsz, seq, two_dc = cv3.shape
    dc = two_dc // 2
    per = CONV_TQ // CONV_HALO
    const = lambda bb, i: (0, 0)
    return pl.pallas_call(
        _conv_kernel,
        out_shape=jax.ShapeDtypeStruct((bsz, seq, dc), BF16),
        grid=(bsz, seq // CONV_TQ),
        in_specs=[pl.BlockSpec((1, CONV_TQ, two_dc), lambda bb, i: (bb, i, 0)),
                  pl.BlockSpec((1, CONV_HALO, two_dc), lambda bb, i: (bb, jnp.maximum(i * per - 1, 0), 0)),
                  pl.BlockSpec(w.shape, const), pl.BlockSpec((1, dc), const),
                  pl.BlockSpec((1, dc), const), pl.BlockSpec((1, dc), const)],
        out_specs=pl.BlockSpec((1, CONV_TQ, dc), lambda bb, i: (bb, i, 0)),
        scratch_shapes=[pltpu.VMEM((CONV_HALO + CONV_TQ, dc), F32), pltpu.VMEM((CONV_TQ, dc), F32)],
        compiler_params=_cparams("parallel", "parallel"),
        name="conv_branch",
    )(cv3, cv3, w, cb, g, b)


def _outproj_kernel(x_ref, gate_ref, g_ref, b_ref, oc_ref, os_ref, ow_ref, ocv_ref, w_ref, o_ref, *, alpha):
    nsa = (oc_ref[...].astype(F32) + os_ref[...].astype(F32) + ow_ref[...].astype(F32)).astype(BF16)
    half = nsa.shape[1]
    y = _dot(nsa, w_ref[0:half, :]) + _dot(ocv_ref[...], w_ref[half:, :])
    o_ref[...] = _res_ln(x_ref[...], gate_ref[0], y, g_ref[...], b_ref[...], alpha)


def _outproj(x2, gate, lng, lnb, o_cmp, o_sel, o_win, o_conv, w, seq, tm, alpha):
    n, d = x2.shape
    per_b = seq // tm
    row = lambda i: (i, 0)
    const = lambda i: (0, 0)
    half = o_cmp.shape[1]
    return pl.pallas_call(
        functools.partial(_outproj_kernel, alpha=alpha),
        out_shape=jax.ShapeDtypeStruct((n, d), F32),
        grid=(n // tm,),
        in_specs=[pl.BlockSpec((tm, d), row), pl.BlockSpec((1, 1, d), lambda i: (i // per_b, 0, 0)),
                  pl.BlockSpec((1, d), const), pl.BlockSpec((1, d), const),
                  pl.BlockSpec((tm, half), row), pl.BlockSpec((tm, half), row), pl.BlockSpec((tm, half), row),
                  pl.BlockSpec((tm, o_conv.shape[1]), row), pl.BlockSpec(w.shape, const)],
        out_specs=pl.BlockSpec((tm, d), row),
        compiler_params=_cparams("parallel"),
        name="out_proj_ln",
    )(x2, gate, lng, lnb, o_cmp, o_sel, o_win, o_conv, w)


def _ffn1_kernel(x_ref, sc_ref, sh_ref, wg_ref, wu_ref, o_ref, *, tn):
    h = (x_ref[...] * (1.0 + sc_ref[0]) + sh_ref[0]).astype(BF16)
    for j in range(o_ref.shape[1] // tn):
        cols = slice(j * tn, (j + 1) * tn)
        gate = _dot(h, wg_ref[:, cols])
        up = _dot(h, wu_ref[:, cols])
        o_ref[:, cols] = (jax.nn.silu(gate) * up).astype(BF16)


def _ffn1(x2, scale, shift, wg, wu, seq, tm):
    n, d = x2.shape
    dff = wg.shape[1]
    per_b = seq // tm
    row = lambda i: (i, 0)
    mod = lambda i: (i // per_b, 0, 0)
    const = lambda i: (0, 0)
    return pl.pallas_call(
        functools.partial(_ffn1_kernel, tn=256),
        out_shape=jax.ShapeDtypeStruct((n, dff), BF16),
        grid=(n // tm,),
        in_specs=[pl.BlockSpec((tm, d), row), pl.BlockSpec((1, 1, d), mod), pl.BlockSpec((1, 1, d), mod),
                  pl.BlockSpec((d, dff), const), pl.BlockSpec((d, dff), const)],
        out_specs=pl.BlockSpec((tm, dff), row),
        compiler_params=_cparams("parallel"),
        name="ffn_up",
    )(x2, scale, shift, wg, wu)


def _ffn2_kernel(x_ref, gate_ref, g_ref, b_ref, a_ref, w_ref, o_ref, *, alpha):
    y = _dot(a_ref[...], w_ref[...])
    o_ref[...] = _res_ln(x_ref[...], gate_ref[0], y, g_ref[...], b_ref[...], alpha)


def _ffn2(x2, gate, lng, lnb, act, w, seq, tm, alpha):
    n, d = x2.shape
    per_b = seq // tm
    row = lambda i: (i, 0)
    const = lambda i: (0, 0)
    return pl.pallas_call(
        functools.partial(_ffn2_kernel, alpha=alpha),
        out_shape=jax.ShapeDtypeStruct((n, d), F32),
        grid=(n // tm,),
        in_specs=[pl.BlockSpec((tm, d), row), pl.BlockSpec((1, 1, d), lambda i: (i // per_b, 0, 0)),
                  pl.BlockSpec((1, d), const), pl.BlockSpec((1, d), const),
                  pl.BlockSpec((tm, act.shape[1]), row), pl.BlockSpec(w.shape, const)],
        out_specs=pl.BlockSpec((tm, d), row),
        compiler_params=_cparams("parallel"),
        name="ffn_down_ln",
    )(x2, gate, lng, lnb, act, w)


def _moe_kernel(x_ref, sc_ref, sh_ref, gate_ref, g_ref, b_ref, rw_ref, wg_ref, wu_ref, w2_ref, o_ref,
                h_sc, acc_sc, wts_sc, *, alpha):
    e = pl.program_id(1)
    f = pl.program_id(2)

    @pl.when((e == 0) & (f == 0))
    def _route():
        hf = x_ref[...] * (1.0 + sc_ref[0]) + sh_ref[0]
        h_sc[...] = hf.astype(BF16)
        lane = lax.broadcasted_iota(jnp.int32, (hf.shape[0], LANE), 1)
        logits = jnp.dot(hf, rw_ref[...], precision=HIGHEST, preferred_element_type=F32)
        logits = jnp.where(lane < N_EXPERTS, logits, -jnp.inf)
        m1 = jnp.max(logits, axis=-1, keepdims=True)
        i1 = jnp.min(jnp.where(logits == m1, lane, LANE), axis=-1, keepdims=True)
        rest = jnp.where(lane == i1, -jnp.inf, logits)
        m2 = jnp.max(rest, axis=-1, keepdims=True)
        i2 = jnp.min(jnp.where(rest == m2, lane, LANE), axis=-1, keepdims=True)
        e2 = jnp.exp(m2 - m1)
        w_first = 1.0 / (1.0 + e2)
        w_second = e2 / (1.0 + e2)
        for ee in range(N_EXPERTS):
            wts_sc[ee] = jnp.where(i1 == ee, w_first, 0.0) + jnp.where(i2 == ee, w_second, 0.0)
        acc_sc[...] = jnp.zeros_like(acc_sc)

    h = h_sc[...]
    gate = _dot(h, wg_ref[0, 0].astype(BF16))
    up = _dot(h, wu_ref[0, 0].astype(BF16))
    act = (jax.nn.silu(gate) * up).astype(BF16)
    acc_sc[...] += wts_sc[e] * _dot(act, w2_ref[0, 0].astype(BF16))

    @pl.when((e == pl.num_programs(1) - 1) & (f == pl.num_programs(2) - 1))
    def _finish():
        o_ref[...] = _res_ln(x_ref[...], gate_ref[0], acc_sc[...], g_ref[...], b_ref[...], alpha)


def _moe(x2, scale, shift, gate, lng, lnb, router_p, moe_w1, moe_w2, layer, seq, tm, alpha, fc=512):
    n, d = x2.shape
    dffe = moe_w2.shape[2]
    nf = dffe // fc
    per_b = seq // tm
    row = lambda i, e, f: (i, 0)
    mod = lambda i, e, f: (i // per_b, 0, 0)
    const = lambda i, e, f: (0, 0)
    return pl.pallas_call(
        functools.partial(_moe_kernel, alpha=alpha),
        out_shape=jax.ShapeDtypeStruct((n, d), F32),
        grid=(n // tm, N_EXPERTS, nf),
        in_specs=[pl.BlockSpec((tm, d), row), pl.BlockSpec((1, 1, d), mod), pl.BlockSpec((1, 1, d), mod),
                  pl.BlockSpec((1, 1, d), mod), pl.BlockSpec((1, d), const), pl.BlockSpec((1, d), const),
                  pl.BlockSpec(router_p.shape, const),
                  pl.BlockSpec((1, 1, d, fc), lambda i, e, f: (layer, e, 0, f)),
                  pl.BlockSpec((1, 1, d, fc), lambda i, e, f: (layer, e, 0, nf + f)),
                  pl.BlockSpec((1, 1, fc, d), lambda i, e, f: (layer, e, f, 0))],
        out_specs=pl.BlockSpec((tm, d), row),
        scratch_shapes=[pltpu.VMEM((tm, d), BF16), pltpu.VMEM((tm, d), F32),
                        pltpu.VMEM((N_EXPERTS, tm, 1), F32)],
        compiler_params=_cparams("parallel", "arbitrary", "arbitrary"),
        name="moe_ffn_ln",
    )(x2, scale, shift, gate, lng, lnb, router_p, moe_w1, moe_w1, moe_w2)


def _rel_bucket_np(dist):
    n = np.maximum(dist, 0)
    max_exact = NUM_BUCKETS // 2
    nf = np.maximum(n, 1).astype(np.float32)
    large = max_exact + (np.log(nf / np.float32(max_exact)) / np.float32(math.log(MAX_DISTANCE / max_exact))
                         * np.float32(NUM_BUCKETS - max_exact)).astype(np.int32)
    large = np.minimum(large, NUM_BUCKETS - 1)
    return np.where(n < max_exact, n, large).astype(np.int32)


def _bias_table(rel_bias, dist, valid):
    n_tiles, tq, tk = dist.shape
    vals = rel_bias[jnp.asarray(_rel_bucket_np(dist))]
    vals = jnp.where(jnp.asarray(valid)[..., None], vals, NEG_INF)
    vals = vals.reshape(n_tiles, tq, tk, N_KV, Q_PER_KV).transpose(3, 0, 4, 1, 2)
    return vals.reshape(N_KV, n_tiles, Q_PER_KV * tq, tk).astype(F32)


def _band_dist(n_delta):
    i = np.arange(ATT_TQ)[None, :, None]
    k = np.arange(ATT_TQ)[None, None, :]
    return ATT_TQ * np.arange(n_delta)[:, None, None] + i - k


def _inproj_perm():
    d_nsa = N_HEADS * HEAD_DIM
    d_kv = N_KV * HEAD_DIM
    base_kc = d_nsa
    base_ks = d_nsa + 2 * d_kv
    base_gl = d_nsa + 6 * d_kv
    base_cv = base_gl + N_HEADS * N_BRANCH
    perm = list(range(d_nsa))
    perm += list(range(base_kc, base_kc + 2 * d_kv))
    for pair in range(2):
        kbase = base_ks + pair * 2 * d_kv
        for g in range(N_KV):
            perm += list(range(kbase + g * HEAD_DIM, kbase + (g + 1) * HEAD_DIM))
            perm += list(range(kbase + d_kv + g * HEAD_DIM, kbase + d_kv + (g + 1) * HEAD_DIM))
    for g in range(N_KV):
        cols = [-1] * LANE
        for br in range(N_BRANCH):
            for r in range(Q_PER_KV):
                cols[br * Q_PER_KV + r] = base_gl + (g * Q_PER_KV + r) * N_BRANCH + br
        perm += cols
    perm += list(range(base_cv, base_cv + 2 * (N_HEADS * HEAD_DIM)))
    return np.asarray(perm, dtype=np.int32)


def kernel(x, c, w_in, cmp_pos, cmp_w1, cmp_w2, conv_w, conv_b, conv_ln_g, conv_ln_b, w_out, rel_bias,
           ada_w, ada_b, ln_g, ln_b, ffn_w1, ffn_w2, router_w, moe_w1, moe_w2):
    bsz, seq, d = x.shape
    depth = w_in.shape[0]
    n = bsz * seq
    alpha = (2 * depth) ** 0.25
    n_blocks = seq // L_SEL
    n_sel = min(N_SEL, n_blocks)
    n_cmp = (seq - L_CMP) // CMP_STRIDE + 1
    nrow = seq // CMP_STRIDE
    assert d == 1024 and seq % 512 == 0 and nrow == LANE and n_cmp == nrow - 1
    tm = 512
    tm_moe = min(1024, seq)

    n_qt = seq // ATT_TQ
    d_win = _band_dist(WINDOW // ATT_TQ + 1)
    win_tab = _bias_table(rel_bias, d_win, (d_win >= 0) & (d_win < WINDOW))
    d_sel = _band_dist(3)
    assert _rel_bucket_np(d_sel[2]).min() == NUM_BUCKETS - 1
    sel_tab = _bias_table(rel_bias, d_sel, d_sel >= 0)
    ends = CMP_STRIDE * np.arange(nrow) + L_CMP - 1
    d_cmp = np.arange(seq).reshape(n_qt, ATT_TQ, 1) - ends[None, None, :]
    cmp_tab = _bias_table(rel_bias, d_cmp, (d_cmp >= 0) & (np.arange(nrow) < n_cmp)[None, None, :])

    perm = _inproj_perm()
    w_in_p = jnp.where(jnp.asarray(perm >= 0)[None, None, :], w_in[:, :, np.maximum(perm, 0)], 0.0).astype(BF16)
    eye2 = jnp.eye(2, dtype=F32)
    w1r = cmp_w1.reshape(depth, 2, L_CMP, HEAD_DIM, HEAD_DIM)
    w1x = jnp.einsum('zslde,st,gh->zlsgdhte', w1r, eye2, eye2).reshape(depth, L_CMP, 4 * HEAD_DIM, 4 * HEAD_DIM)
    half = L_CMP // 2
    w_lo = w1x[:, :half].reshape(depth, half * 4 * HEAD_DIM, 4 * HEAD_DIM).astype(BF16)
    w_hi = w1x[:, half:].reshape(depth, half * 4 * HEAD_DIM, 4 * HEAD_DIM).astype(BF16)
    w2bd = jnp.einsum('zsde,st,gh->zgsdhte', cmp_w2, eye2, eye2).reshape(depth, 4 * HEAD_DIM, 4 * HEAD_DIM).astype(BF16)
    posx = jnp.broadcast_to(cmp_pos[:, :, :, None, :], (depth, 2, L_CMP, N_KV, HEAD_DIM)).transpose(0, 2, 1, 3, 4)
    pos_lo = posx[:, :half].reshape(depth, 1, half * 4 * HEAD_DIM)
    pos_hi = posx[:, half:].reshape(depth, 1, half * 4 * HEAD_DIM)
    conv_wp = jnp.pad(conv_w, ((0, 0), (0, 1), (0, 0)))
    w_out_b = w_out.astype(BF16)
    dff = ffn_w2.shape[1]
    ffn_wg = ffn_w1[:, :, :dff].astype(BF16)
    ffn_wu = ffn_w1[:, :, dff:].astype(BF16)
    ffn_w2b = ffn_w2.astype(BF16)
    router_p = jnp.pad(router_w, ((0, 0), (0, 0), (0, LANE - N_EXPERTS)))

    mod = _ada(c, ada_w, ada_b).reshape(depth, 6, bsz, 1, d)
    x2 = x.reshape(n, d)
    for l in range(depth):
        shift1, scale1, gate1, shift2, scale2, gate2 = [mod[l, k] for k in range(6)]
        q, kvc, kvsw, gl, cv = _inproj(x2, scale1, shift1, w_in_p[l], seq, tm)
        q3 = q.reshape(bsz, seq, 512)
        gl3 = gl.reshape(bsz, seq, 256)
        kvsw3 = kvsw.reshape(bsz, seq, 512)
        kvcmp = _compress(kvc.reshape(bsz, nrow, CMP_STRIDE * 256), pos_lo[l], pos_hi[l], w_lo[l], w_hi[l], w2bd[l])
        o_cmp, sel = _cmp_attn(q3, kvcmp, cmp_tab, gl3, n_blocks, n_sel)
        o_sel = _band_attn(q3, kvsw3, 0, sel_tab, gl3, sel, None, 1)
        o_win = _band_attn(q3, kvsw3, N_KV, win_tab, gl3, None, WINDOW // ATT_TQ, 2)
        o_conv = _conv_branch(cv.reshape(bsz, seq, 1024), conv_wp[l], conv_b[l][None], conv_ln_g[l][None],
                              conv_ln_b[l][None])
        x2 = _outproj(x2, gate1, ln_g[l, 0][None], ln_b[l, 0][None], o_cmp.reshape(n, 512), o_sel.reshape(n, 512),
                      o_win.reshape(n, 512), o_conv.reshape(n, 512), w_out_b[l], seq, tm, alpha)
        if l % 2 == 0:
            act = _ffn1(x2, scale2, shift2, ffn_wg[l // 2], ffn_wu[l // 2], seq, tm)
            x2 = _ffn2(x2, gate2, ln_g[l, 1][None], ln_b[l, 1][None], act, ffn_w2b[l // 2], seq, tm, alpha)
        else:
            x2 = _moe(x2, scale2, shift2, gate2, ln_g[l, 1][None], ln_b[l, 1][None], router_p[l // 2],
                      moe_w1, moe_w2, l // 2, seq, tm_moe, alpha)
    return x2.reshape(bsz, seq, d)
```

```python
import functools
import math

import numpy as np
import jax
import jax.numpy as jnp
from jax import lax
from jax.experimental import pallas as pl
from jax.experimental.pallas import tpu as pltpu

F32 = jnp.float32
BF16 = jnp.bfloat16
HIGHEST = lax.Precision.HIGHEST

HEAD_DIM = 64
N_HEADS = 8
N_KV = 2
Q_PER_KV = N_HEADS // N_KV
N_BRANCH = 3
L_CMP = 32
CMP_STRIDE = 16
L_SEL = 64
N_SEL = 8
WINDOW = 512
CONV_WIDTH = 31
NUM_BUCKETS = 32
MAX_DISTANCE = 128
N_EXPERTS = 8
LN_EPS = 1e-5
NEG_INF = -1e30

LANE = 128
ATT_TQ = 128
ATT_ROWS = Q_PER_KV * ATT_TQ
CONV_TQ = 256
CONV_HALO = 32
CONV_RC = 64
VMEM_LIMIT_BYTES = 52 * 1024 * 1024


def _cparams(*sem):
    return pltpu.CompilerParams(dimension_semantics=sem, vmem_limit_bytes=VMEM_LIMIT_BYTES)


def _dot(a, b):
    return jnp.dot(a, b, preferred_element_type=F32)


def _dot_nt(a, b, precision=None):
    return lax.dot_general(a, b, (((1,), (1,)), ((), ())), precision=precision,
                           preferred_element_type=F32)


def _res_ln(x, gate, y, g, b, alpha):
    z = alpha * x + (1.0 + gate) * y
    mu = jnp.mean(z, axis=-1, keepdims=True)
    zc = z - mu
    var = jnp.mean(zc * zc, axis=-1, keepdims=True)
    return zc * lax.rsqrt(var + LN_EPS) * g + b


def _ada_kernel(c_ref, w_ref, b_ref, o_ref):
    sc = jax.nn.silu(c_ref[...])
    o_ref[0, 0] = jnp.dot(sc, w_ref[0], precision=HIGHEST, preferred_element_type=F32) + b_ref[0]


def _ada(c, ada_w, ada_b):
    depth, d, _ = ada_w.shape
    bsz = c.shape[0]
    return pl.pallas_call(
        _ada_kernel,
        out_shape=jax.ShapeDtypeStruct((depth, 6, bsz, d), F32),
        grid=(depth, 6),
        in_specs=[pl.BlockSpec((bsz, d), lambda l, k: (0, 0)),
                  pl.BlockSpec((1, d, d), lambda l, k: (l, 0, k)),
                  pl.BlockSpec((1, 1, d), lambda l, k: (l, 0, k))],
        out_specs=pl.BlockSpec((1, 1, bsz, d), lambda l, k: (l, k, 0, 0)),
        compiler_params=_cparams("parallel", "parallel"),
        name="ada_mod",
    )(c, ada_w, ada_b.reshape(depth, 1, 6 * d))


Q_COLS = N_HEADS * LANE
KVC_COLS = 4 * HEAD_DIM
KVSW_COLS = 8 * HEAD_DIM
GL_ROWS = N_KV * LANE


def _inproj_kernel(x_ref, sc_ref, sh_ref, w_ref, wgl_ref, q_ref, kvc_ref, kvsw_ref, glt_ref, cv_ref):
    h = (x_ref[...] * (1.0 + sc_ref[0]) + sh_ref[0]).astype(BF16)
    c0, c1, c2 = Q_COLS, Q_COLS + KVC_COLS, Q_COLS + KVC_COLS + KVSW_COLS
    q_ref[...] = (_dot(h, w_ref[:, 0:c0]) * (HEAD_DIM ** -0.5)).astype(BF16)
    kvc_ref[...] = _dot(h, w_ref[:, c0:c1])
    kvsw_ref[...] = _dot(h, w_ref[:, c1:c2]).astype(BF16)
    cv_ref[...] = _dot(h, w_ref[:, c2:])
    glt_ref[...] = _dot_nt(wgl_ref[...], h)


def _inproj(x2, scale, shift, w, wgl_t, seq, tm):
    n, d = x2.shape
    per_b = seq // tm
    d_cv = w.shape[1] - (Q_COLS + KVC_COLS + KVSW_COLS)
    row = lambda i: (i, 0)
    mod = lambda i: (i // per_b, 0, 0)
    const = lambda i: (0, 0)
    return pl.pallas_call(
        _inproj_kernel,
        out_shape=(jax.ShapeDtypeStruct((n, Q_COLS), BF16), jax.ShapeDtypeStruct((n, KVC_COLS), F32),
                   jax.ShapeDtypeStruct((n, KVSW_COLS), BF16), jax.ShapeDtypeStruct((GL_ROWS, n), F32),
                   jax.ShapeDtypeStruct((n, d_cv), F32)),
        grid=(n // tm,),
        in_specs=[pl.BlockSpec((tm, d), row), pl.BlockSpec((1, 1, d), mod), pl.BlockSpec((1, 1, d), mod),
                  pl.BlockSpec(w.shape, const), pl.BlockSpec(wgl_t.shape, const)],
        out_specs=(pl.BlockSpec((tm, Q_COLS), row), pl.BlockSpec((tm, KVC_COLS), row),
                   pl.BlockSpec((tm, KVSW_COLS), row), pl.BlockSpec((GL_ROWS, tm), lambda i: (0, i)),
                   pl.BlockSpec((tm, d_cv), row)),
        compiler_params=_cparams("parallel"),
        name="in_proj",
    )(x2, scale, shift, w, wgl_t)


def _compress_kernel(a_ref, plo_ref, phi_ref, wlo_ref, whi_ref, w2_ref, o_ref, ot_ref):
    a = a_ref[0]
    lo = _dot((a + plo_ref[...]).astype(BF16), wlo_ref[...])
    hi = _dot((a + phi_ref[...]).astype(BF16), whi_ref[...])
    nrow = lo.shape[0]
    pre = lo + pltpu.roll(hi, nrow - 1, 0)
    act = jax.nn.gelu(pre, approximate=True)
    out = _dot(act.astype(BF16), w2_ref[...])
    o_ref[0] = out.astype(BF16)
    ot_ref[0] = out.T.astype(BF16)


def _compress(kvc3, pos_lo, pos_hi, wlo, whi, w2bd):
    bsz, nrow, width = kvc3.shape
    const = lambda b: (0, 0)
    return pl.pallas_call(
        _compress_kernel,
        out_shape=(jax.ShapeDtypeStruct((bsz, nrow, 256), BF16), jax.ShapeDtypeStruct((bsz, 256, nrow), BF16)),
        grid=(bsz,),
        in_specs=[pl.BlockSpec((1, nrow, width), lambda b: (b, 0, 0)),
                  pl.BlockSpec((1, width), const), pl.BlockSpec((1, width), const),
                  pl.BlockSpec((width, 256), const), pl.BlockSpec((width, 256), const),
                  pl.BlockSpec((256, 256), const)],
        out_specs=(pl.BlockSpec((1, nrow, 256), lambda b: (b, 0, 0)),
                   pl.BlockSpec((1, 256, nrow), lambda b: (b, 0, 0))),
        compiler_params=_cparams("parallel"),
        name="compress_mlp",
    )(kvc3, pos_lo, pos_hi, wlo, whi, w2bd)


def _stack_heads(q):
    return jnp.concatenate([q[:, r * LANE:(r + 1) * LANE] for r in range(Q_PER_KV)], axis=0)


def _gated_heads(o_t, gl_t, branch):
    tq = gl_t.shape[1]
    gates = jnp.concatenate([jax.nn.sigmoid(gl_t[branch * Q_PER_KV + r:branch * Q_PER_KV + r + 1, :])
                             for r in range(Q_PER_KV)], axis=1)
    og = (o_t * gates).astype(BF16)
    ii = lax.broadcasted_iota(jnp.int32, (tq, tq), 0)
    kk = lax.broadcasted_iota(jnp.int32, (tq, tq), 1)
    eye = jnp.where(ii == kk, 1.0, 0.0).astype(BF16)
    halves = []
    for pair in range(Q_PER_KV // 2):
        two_heads = jnp.concatenate([og[:, (2 * pair) * tq:(2 * pair + 1) * tq],
                                     og[:, (2 * pair + 1) * tq:(2 * pair + 2) * tq]], axis=0)
        halves.append(_dot_nt(eye, two_heads))
    return jnp.concatenate(halves, axis=1).astype(BF16)


def _cmp_attn_kernel(q_ref, kv_ref, kvt_ref, tab_ref, gl_ref, o_ref, sel_ref, *, n_blocks, n_sel):
    qt = pl.program_id(2)
    tq = ATT_TQ
    qs = _stack_heads(q_ref[0])
    bias = tab_ref[0, 0]
    s = _dot_nt(kv_ref[0], qs) + bias
    valid = bias > 0.5 * NEG_INF
    m = jnp.max(s, axis=0, keepdims=True)
    p = jnp.where(valid, jnp.exp(s - m), 0.0)
    l = jnp.sum(p, axis=0, keepdims=True)
    p = p / jnp.where(l > 0.0, l, 1.0)
    o_t = _dot(kvt_ref[0, HEAD_DIM:, :], p.astype(BF16))
    o_ref[0] = _gated_heads(o_t, gl_ref[...], 0)

    psum = p[:, 0:tq] + p[:, tq:2 * tq] + p[:, 2 * tq:3 * tq] + p[:, 3 * tq:4 * tq]
    jj = lax.broadcasted_iota(jnp.int32, (n_blocks, LANE), 0)
    nn = lax.broadcasted_iota(jnp.int32, (n_blocks, LANE), 1)
    overlap_t = jnp.where((CMP_STRIDE * nn < L_SEL * jj + L_SEL) & (CMP_STRIDE * nn + L_CMP > L_SEL * jj),
                          1.0, 0.0).astype(F32)
    imp_t = jnp.dot(overlap_t, psum, precision=HIGHEST, preferred_element_type=F32)

    jb = lax.broadcasted_iota(jnp.int32, (n_blocks, tq), 0)
    tb = (qt * tq + lax.broadcasted_iota(jnp.int32, (n_blocks, tq), 1)) // L_SEL
    score = jnp.where(jb > tb, -jnp.inf,
                      jnp.where((jb == 0) | (jb == tb) | (jb == tb - 1), jnp.inf, imp_t))
    rank = jnp.zeros((n_blocks, tq), F32)
    for jp in range(n_blocks):
        row = score[jp:jp + 1, :]
        earlier = jnp.where(jb > jp, 1.0, 0.0)
        rank = rank + jnp.where(row > score, 1.0, 0.0) + jnp.where(row == score, earlier, 0.0)
    sel_ref[0, 0] = jnp.where(rank < float(n_sel), 1.0, 0.0).astype(BF16)


def _cmp_attn(q3, kvcmp, kvcmp_t, tab, gl3, n_blocks, n_sel):
    bsz, seq, _ = q3.shape
    nqt = seq // ATT_TQ
    n_cmp = kvcmp.shape[1]
    kern = functools.partial(_cmp_attn_kernel, n_blocks=n_blocks, n_sel=n_sel)
    return pl.pallas_call(
        kern,
        out_shape=(jax.ShapeDtypeStruct((bsz, seq, 512), BF16),
                   jax.ShapeDtypeStruct((bsz, N_KV, n_blocks, seq), BF16)),
        grid=(bsz, N_KV, nqt),
        in_specs=[pl.BlockSpec((1, ATT_TQ, Q_PER_KV * LANE), lambda b, g, t: (b, t, g)),
                  pl.BlockSpec((1, n_cmp, LANE), lambda b, g, t: (b, 0, g)),
                  pl.BlockSpec((1, LANE, n_cmp), lambda b, g, t: (b, g, 0)),
                  pl.BlockSpec((1, 1, n_cmp, ATT_ROWS), lambda b, g, t: (g, t, 0, 0)),
                  pl.BlockSpec((LANE, ATT_TQ), lambda b, g, t: (g, b * nqt + t))],
        out_specs=(pl.BlockSpec((1, ATT_TQ, 256), lambda b, g, t: (b, t, g)),
                   pl.BlockSpec((1, 1, n_blocks, ATT_TQ), lambda b, g, t: (b, g, 0, t))),
        compiler_params=_cparams("parallel", "parallel", "parallel"),
        name="cmp_attn",
    )(q3, kvcmp, kvcmp_t, tab, gl3)


def _band_attn_kernel(*refs, n_delta, span, group, branch, use_sel):
    if use_sel:
        q_ref, kv_ref, kvt_ref, tab_ref, gl_ref, sel_ref, o_ref, s_sc = refs
    else:
        q_ref, kv_ref, kvt_ref, tab_ref, gl_ref, o_ref, s_sc = refs
    qt = pl.program_id(2)
    qs = _stack_heads(q_ref[0])
    tq = ATT_TQ
    n_chunks = kv_ref.shape[1] // tq
    c_base = qt - span if span is not None else 0
    if use_sel:
        sel_t = sel_ref[0, 0]
        nb = sel_t.shape[0]
        kb = lax.broadcasted_iota(jnp.int32, (tq, nb), 0) // L_SEL
        jj = lax.broadcasted_iota(jnp.int32, (tq, nb), 1)

    def chunk_ids(gi, u):
        c = c_base + gi * group + u
        inside = (c >= 0) & (c <= qt)
        dd = jnp.where(inside, jnp.minimum(qt - c, n_delta - 1), n_delta)
        return jnp.clip(c, 0, n_chunks - 1), dd

    def rows(gi, u):
        return pl.ds(pl.multiple_of((gi * group + u) * tq, tq), tq)

    def scores(gi, m):
        for u in range(group):
            cc, dd = chunk_ids(gi, u)
            kv = kv_ref[0, pl.ds(pl.multiple_of(cc * tq, tq), tq), :]
            s = _dot_nt(kv, qs) + tab_ref[0, dd]
            if use_sel:
                expand_t = jnp.where(jj == cc * (tq // L_SEL) + kb, 1.0, 0.0).astype(BF16)
                allowed = _dot(expand_t, sel_t)
                madd = (allowed - 1.0) * (-NEG_INF)
                s = s + jnp.concatenate([madd] * Q_PER_KV, axis=1)
            s_sc[rows(gi, u), :] = s
            m = jnp.maximum(m, jnp.max(s, axis=0, keepdims=True))
        return m

    m0 = jnp.full((1, ATT_ROWS), NEG_INF, F32)
    if span is not None:
        n_groups = 1
        m = scores(0, m0)
    else:
        n_groups = qt // group + 1
        m = lax.fori_loop(0, n_groups, scores, m0)

    def values(gi, carry):
        l, acc = carry
        for u in range(group):
            cc, _ = chunk_ids(gi, u)
            p = jnp.exp(s_sc[rows(gi, u), :] - m)
            l = l + jnp.sum(p, axis=0, keepdims=True)
            acc = acc + _dot(kvt_ref[0, 0, cc, HEAD_DIM:, :], p.astype(BF16))
        return l, acc

    init = (jnp.zeros((1, ATT_ROWS), F32), jnp.zeros((HEAD_DIM, ATT_ROWS), F32))
    if span is not None:
        l, acc = values(0, init)
    else:
        l, acc = lax.fori_loop(0, n_groups, values, init)
    o_ref[0] = _gated_heads(acc / l, gl_ref[...], branch)


def _band_attn(q3, kvsw3, kvsw_t, kv_block0, tab, gl3, sel_t, span, branch):
    bsz, seq, _ = q3.shape
    nqt = seq // ATT_TQ
    use_sel = sel_t is not None
    group = span + 1 if span is not None else 4
    slots = group if span is not None else -(-nqt // group) * group
    kern = functools.partial(_band_attn_kernel, n_delta=tab.shape[1] - 1, span=span, group=group, branch=branch,
                             use_sel=use_sel)
    in_specs = [pl.BlockSpec((1, ATT_TQ, Q_PER_KV * LANE), lambda b, g, t: (b, t, g)),
                pl.BlockSpec((1, seq, LANE), lambda b, g, t: (b, 0, kv_block0 + g)),
                pl.BlockSpec((1, 1) + kvsw_t.shape[2:], lambda b, g, t: (b, kv_block0 + g, 0, 0, 0)),
                pl.BlockSpec((1,) + tab.shape[1:], lambda b, g, t: (g, 0, 0, 0)),
                pl.BlockSpec((LANE, ATT_TQ), lambda b, g, t: (g, b * nqt + t))]
    args = [q3, kvsw3, kvsw_t, tab, gl3]
    if use_sel:
        in_specs.append(pl.BlockSpec((1, 1, sel_t.shape[2], ATT_TQ), lambda b, g, t: (b, g, 0, t)))
        args.append(sel_t)
    return pl.pallas_call(
        kern,
        out_shape=jax.ShapeDtypeStruct((bsz, seq, 512), BF16),
        grid=(bsz, N_KV, nqt),
        in_specs=in_specs,
        out_specs=pl.BlockSpec((1, ATT_TQ, 256), lambda b, g, t: (b, t, g)),
        scratch_shapes=[pltpu.VMEM((slots * ATT_TQ, ATT_ROWS), F32)],
        compiler_params=_cparams("parallel", "parallel", "parallel"),
        name="sel_attn" if use_sel else "win_attn",
    )(*args)


def _conv_kernel(cur_ref, prev_ref, w_ref, cb_ref, g_ref, b_ref, o_ref, ybuf, cbuf):
    i = pl.program_id(1)
    dc = o_ref.shape[-1]
    cur = cur_ref[0]
    prev = prev_ref[0]
    ybuf[CONV_HALO:, :] = cur[:, :dc] * jax.nn.sigmoid(cur[:, dc:])
    halo = prev[:, :dc] * jax.nn.sigmoid(prev[:, dc:])
    ybuf[0:CONV_HALO, :] = jnp.where(i > 0, halo, 0.0)
    off = CONV_HALO - (CONV_WIDTH - 1)
    for lc in range(dc // LANE):
        cols = slice(lc * LANE, (lc + 1) * LANE)
        for rc in range(CONV_TQ // CONV_RC):
            acc = jnp.zeros((CONV_RC, LANE), F32)
            for k in range(CONV_WIDTH):
                r0 = rc * CONV_RC + off + k
                acc = acc + w_ref[k:k + 1, cols] * ybuf[r0:r0 + CONV_RC, cols]
            cbuf[rc * CONV_RC:(rc + 1) * CONV_RC, cols] = acc
    z = cbuf[...] + cb_ref[...]
    mu = jnp.mean(z, axis=-1, keepdims=True)
    zc = z - mu
    var = jnp.mean(zc * zc, axis=-1, keepdims=True)
    y = zc * lax.rsqrt(var + LN_EPS) * g_ref[...] + b_ref[...]
    o_ref[0] = jax.nn.silu(y).astype(BF16)


def _conv_branch(cv3, w, cb, g, b):
    bsz, seq, two_dc = cv3.shape
    dc = two_dc // 2
    per = CONV_TQ // CONV_HALO
    const = lambda bb, i: (0, 0)
    return pl.pallas_call(
        _conv_kernel,
        out_shape=jax.ShapeDtypeStruct((bsz, seq, dc), BF16),
        grid=(bsz, seq // CONV_TQ),
        in_specs=[pl.BlockSpec((1, CONV_TQ, two_dc), lambda bb, i: (bb, i, 0)),
                  pl.BlockSpec((1, CONV_HALO, two_dc), lambda bb, i: (bb, jnp.maximum(i * per - 1, 0), 0)),
                  pl.BlockSpec(w.shape, const), pl.BlockSpec((1, dc), const),
                  pl.BlockSpec((1, dc), const), pl.BlockSpec((1, dc), const)],
        out_specs=pl.BlockSpec((1, CONV_TQ, dc), lambda bb, i: (bb, i, 0)),
        scratch_shapes=[pltpu.VMEM((CONV_HALO + CONV_TQ, dc), F32), pltpu.VMEM((CONV_TQ, dc), F32)],
        compiler_params=_cparams("parallel", "parallel"),
        name="conv_branch",
    )(cv3, cv3, w, cb, g, b)


def _outproj_kernel(x_ref, gate_ref, g_ref, b_ref, oc_ref, os_ref, ow_ref, ocv_ref, w_ref, o_ref, *, alpha):
    nsa = (oc_ref[...].astype(F32) + os_ref[...].astype(F32) + ow_ref[...].astype(F32)).astype(BF16)
    half = nsa.shape[1]
    y = _dot(nsa, w_ref[0:half, :]) + _dot(ocv_ref[...], w_ref[half:, :])
    o_ref[...] = _res_ln(x_ref[...], gate_ref[0], y, g_ref[...], b_ref[...], alpha)


def _outproj(x2, gate, lng, lnb, o_cmp, o_sel, o_win, o_conv, w, seq, tm, alpha):
    n, d = x2.shape
    per_b = seq // tm
    row = lambda i: (i, 0)
    const = lambda i: (0, 0)
    half = o_cmp.shape[1]
    return pl.pallas_call(
        functools.partial(_outproj_kernel, alpha=alpha),
        out_shape=jax.ShapeDtypeStruct((n, d), F32),
        grid=(n // tm,),
        in_specs=[pl.BlockSpec((tm, d), row), pl.BlockSpec((1, 1, d), lambda i: (i // per_b, 0, 0)),
                  pl.BlockSpec((1, d), const), pl.BlockSpec((1, d), const),
                  pl.BlockSpec((tm, half), row), pl.BlockSpec((tm, half), row), pl.BlockSpec((tm, half), row),
                  pl.BlockSpec((tm, o_conv.shape[1]), row), pl.BlockSpec(w.shape, const)],
        out_specs=pl.BlockSpec((tm, d), row),
        compiler_params=_cparams("parallel"),
        name="out_proj_ln",
    )(x2, gate, lng, lnb, o_cmp, o_sel, o_win, o_conv, w)


def _ffn1_kernel(x_ref, sc_ref, sh_ref, wg_ref, wu_ref, o_ref, *, tn):
    h = (x_ref[...] * (1.0 + sc_ref[0]) + sh_ref[0]).astype(BF16)
    for j in range(o_ref.shape[1] // tn):
        cols = slice(j * tn, (j + 1) * tn)
        gate = _dot(h, wg_ref[:, cols])
        up = _dot(h, wu_ref[:, cols])
        o_ref[:, cols] = (jax.nn.silu(gate) * up).astype(BF16)


def _ffn1(x2, scale, shift, wg, wu, seq, tm):
    n, d = x2.shape
    dff = wg.shape[1]
    per_b = seq // tm
    row = lambda i: (i, 0)
    mod = lambda i: (i // per_b, 0, 0)
    const = lambda i: (0, 0)
    return pl.pallas_call(
        functools.partial(_ffn1_kernel, tn=256),
        out_shape=jax.ShapeDtypeStruct((n, dff), BF16),
        grid=(n // tm,),
        in_specs=[pl.BlockSpec((tm, d), row), pl.BlockSpec((1, 1, d), mod), pl.BlockSpec((1, 1, d), mod),
                  pl.BlockSpec((d, dff), const), pl.BlockSpec((d, dff), const)],
        out_specs=pl.BlockSpec((tm, dff), row),
        compiler_params=_cparams("parallel"),
        name="ffn_up",
    )(x2, scale, shift, wg, wu)


def _ffn2_kernel(x_ref, gate_ref, g_ref, b_ref, a_ref, w_ref, o_ref, *, alpha):
    y = _dot(a_ref[...], w_ref[...])
    o_ref[...] = _res_ln(x_ref[...], gate_ref[0], y, g_ref[...], b_ref[...], alpha)


def _ffn2(x2, gate, lng, lnb, act, w, seq, tm, alpha):
    n, d = x2.shape
    per_b = seq // tm
    row = lambda i: (i, 0)
    const = lambda i: (0, 0)
    return pl.pallas_call(
        functools.partial(_ffn2_kernel, alpha=alpha),
        out_shape=jax.ShapeDtypeStruct((n, d), F32),
        grid=(n // tm,),
        in_specs=[pl.BlockSpec((tm, d), row), pl.BlockSpec((1, 1, d), lambda i: (i // per_b, 0, 0)),
                  pl.BlockSpec((1, d), const), pl.BlockSpec((1, d), const),
                  pl.BlockSpec((tm, act.shape[1]), row), pl.BlockSpec(w.shape, const)],
        out_specs=pl.BlockSpec((tm, d), row),
        compiler_params=_cparams("parallel"),
        name="ffn_down_ln",
    )(x2, gate, lng, lnb, act, w)


def _moe_kernel(x_ref, sc_ref, sh_ref, gate_ref, g_ref, b_ref, rw_ref, wg_ref, wu_ref, w2_ref, o_ref,
                h_sc, acc_sc, wts_sc, *, alpha):
    e = pl.program_id(1)
    f = pl.program_id(2)

    @pl.when((e == 0) & (f == 0))
    def _route():
        hf = x_ref[...] * (1.0 + sc_ref[0]) + sh_ref[0]
        h_sc[...] = hf.astype(BF16)
        lane = lax.broadcasted_iota(jnp.int32, (hf.shape[0], LANE), 1)
        logits = jnp.dot(hf, rw_ref[...], precision=HIGHEST, preferred_element_type=F32)
        logits = jnp.where(lane < N_EXPERTS, logits, -jnp.inf)
        m1 = jnp.max(logits, axis=-1, keepdims=True)
        i1 = jnp.min(jnp.where(logits == m1, lane, LANE), axis=-1, keepdims=True)
        rest = jnp.where(lane == i1, -jnp.inf, logits)
        m2 = jnp.max(rest, axis=-1, keepdims=True)
        i2 = jnp.min(jnp.where(rest == m2, lane, LANE), axis=-1, keepdims=True)
        e2 = jnp.exp(m2 - m1)
        w_first = 1.0 / (1.0 + e2)
        w_second = e2 / (1.0 + e2)
        for ee in range(N_EXPERTS):
            wts_sc[ee] = jnp.where(i1 == ee, w_first, 0.0) + jnp.where(i2 == ee, w_second, 0.0)
        acc_sc[...] = jnp.zeros_like(acc_sc)

    h = h_sc[...]
    gate = _dot(h, wg_ref[0, 0].astype(BF16))
    up = _dot(h, wu_ref[0, 0].astype(BF16))
    act = (jax.nn.silu(gate) * up).astype(BF16)
    acc_sc[...] += wts_sc[e] * _dot(act, w2_ref[0, 0].astype(BF16))

    @pl.when((e == pl.num_programs(1) - 1) & (f == pl.num_programs(2) - 1))
    def _finish():
        o_ref[...] = _res_ln(x_ref[...], gate_ref[0], acc_sc[...], g_ref[...], b_ref[...], alpha)


def _moe(x2, scale, shift, gate, lng, lnb, router_p, moe_w1, moe_w2, layer, seq, tm, alpha, fc=512):
    n, d = x2.shape
    dffe = moe_w2.shape[2]
    nf = dffe // fc
    per_b = seq // tm
    row = lambda i, e, f: (i, 0)
    mod = lambda i, e, f: (i // per_b, 0, 0)
    const = lambda i, e, f: (0, 0)
    return pl.pallas_call(
        functools.partial(_moe_kernel, alpha=alpha),
        out_shape=jax.ShapeDtypeStruct((n, d), F32),
        grid=(n // tm, N_EXPERTS, nf),
        in_specs=[pl.BlockSpec((tm, d), row), pl.BlockSpec((1, 1, d), mod), pl.BlockSpec((1, 1, d), mod),
                  pl.BlockSpec((1, 1, d), mod), pl.BlockSpec((1, d), const), pl.BlockSpec((1, d), const),
                  pl.BlockSpec(router_p.shape, const),
                  pl.BlockSpec((1, 1, d, fc), lambda i, e, f: (layer, e, 0, f)),
                  pl.BlockSpec((1, 1, d, fc), lambda i, e, f: (layer, e, 0, nf + f)),
                  pl.BlockSpec((1, 1, fc, d), lambda i, e, f: (layer, e, f, 0))],
        out_specs=pl.BlockSpec((tm, d), row),
        scratch_shapes=[pltpu.VMEM((tm, d), BF16), pltpu.VMEM((tm, d), F32),
                        pltpu.VMEM((N_EXPERTS, tm, 1), F32)],
        compiler_params=_cparams("parallel", "arbitrary", "arbitrary"),
        name="moe_ffn_ln",
    )(x2, scale, shift, gate, lng, lnb, router_p, moe_w1, moe_w1, moe_w2)


def _rel_bucket_np(dist):
    n = np.maximum(dist, 0)
    max_exact = NUM_BUCKETS // 2
    nf = np.maximum(n, 1).astype(np.float32)
    large = max_exact + (np.log(nf / np.float32(max_exact)) / np.float32(math.log(MAX_DISTANCE / max_exact))
                         * np.float32(NUM_BUCKETS - max_exact)).astype(np.int32)
    large = np.minimum(large, NUM_BUCKETS - 1)
    return np.where(n < max_exact, n, large).astype(np.int32)


def _bias_lookup(rel_bias, dist, valid):
    vals = rel_bias[jnp.asarray(_rel_bucket_np(dist))]
    return jnp.where(jnp.asarray(valid)[..., None], vals, NEG_INF)


def _key_major(vals, add_masked_tile=False):
    n_tiles, tq, tk, _ = vals.shape
    if add_masked_tile:
        vals = jnp.concatenate([vals, jnp.full((1,) + vals.shape[1:], NEG_INF, vals.dtype)], axis=0)
        n_tiles += 1
    vals = vals.reshape(n_tiles, tq, tk, N_KV, Q_PER_KV).transpose(3, 0, 2, 4, 1)
    return vals.reshape(N_KV, n_tiles, tk, Q_PER_KV * tq).astype(F32)


def _band_dist(n_delta):
    i = np.arange(ATT_TQ)[None, :, None]
    k = np.arange(ATT_TQ)[None, None, :]
    return ATT_TQ * np.arange(n_delta)[:, None, None] + i - k


def _cmp_bias_table(rel_bias, n_qt, n_cmp_pad):
    per_tile = ATT_TQ // CMP_STRIDE
    shift = per_tile * (n_qt - 1)
    rel_n = np.arange(n_cmp_pad + shift) - shift
    dist = np.arange(ATT_TQ)[:, None] - (CMP_STRIDE * rel_n[None, :] + L_CMP - 1)
    base = _bias_lookup(rel_bias, dist, dist >= 0)
    tiles = [base[:, shift - per_tile * qt: shift - per_tile * qt + n_cmp_pad] for qt in range(n_qt)]
    return _key_major(jnp.stack(tiles))


def _inproj_weight(w_in):
    depth, d, _ = w_in.shape
    d_nsa = N_HEADS * HEAD_DIM
    d_kv = N_KV * HEAD_DIM
    base_ks = d_nsa + 2 * d_kv
    base_gl = d_nsa + 6 * d_kv
    base_cv = base_gl + N_HEADS * N_BRANCH
    wq = jnp.pad(w_in[:, :, :d_nsa].reshape(depth, d, N_HEADS, HEAD_DIM),
                 ((0, 0), (0, 0), (0, 0), (0, LANE - HEAD_DIM))).reshape(depth, d, Q_COLS)
    parts = [wq, w_in[:, :, d_nsa:base_ks]]
    for pair in range(2):
        kbase = base_ks + pair * 2 * d_kv
        for g in range(N_KV):
            parts.append(w_in[:, :, kbase + g * HEAD_DIM: kbase + (g + 1) * HEAD_DIM])
            parts.append(w_in[:, :, kbase + d_kv + g * HEAD_DIM: kbase + d_kv + (g + 1) * HEAD_DIM])
    parts.append(w_in[:, :, base_cv:])
    gl = w_in[:, :, base_gl:base_cv].reshape(depth, d, N_KV, Q_PER_KV, N_BRANCH).transpose(0, 2, 4, 3, 1)
    gl = jnp.pad(gl.reshape(depth, N_KV, N_BRANCH * Q_PER_KV, d),
                 ((0, 0), (0, 0), (0, LANE - N_BRANCH * Q_PER_KV), (0, 0)))
    return jnp.concatenate(parts, axis=-1).astype(BF16), gl.reshape(depth, GL_ROWS, d).astype(BF16)


def kernel(x, c, w_in, cmp_pos, cmp_w1, cmp_w2, conv_w, conv_b, conv_ln_g, conv_ln_b, w_out, rel_bias,
           ada_w, ada_b, ln_g, ln_b, ffn_w1, ffn_w2, router_w, moe_w1, moe_w2):
    bsz, seq, d = x.shape
    depth = w_in.shape[0]
    n = bsz * seq
    alpha = (2 * depth) ** 0.25
    n_blocks = seq // L_SEL
    n_sel = min(N_SEL, n_blocks)
    n_cmp = (seq - L_CMP) // CMP_STRIDE + 1
    nrow = seq // CMP_STRIDE
    assert d == 1024 and seq % 512 == 0 and nrow == LANE and n_cmp == nrow - 1
    tm = 512
    tm_moe = min(1024, seq)

    n_qt = seq // ATT_TQ
    d_win = _band_dist(WINDOW // ATT_TQ + 1)
    win_tab = _key_major(_bias_lookup(rel_bias, d_win, (d_win >= 0) & (d_win < WINDOW)), add_masked_tile=True)
    d_sel = _band_dist(3)
    assert _rel_bucket_np(d_sel[2]).min() == NUM_BUCKETS - 1
    sel_tab = _key_major(_bias_lookup(rel_bias, d_sel, d_sel >= 0), add_masked_tile=True)
    cmp_tab = _cmp_bias_table(rel_bias, n_qt, nrow)

    w_in_p, w_gl_t = _inproj_weight(w_in)
    eye2 = jnp.eye(2, dtype=F32)
    w1r = cmp_w1.reshape(depth, 2, L_CMP, HEAD_DIM, HEAD_DIM)
    w1x = jnp.einsum('zslde,st,gh->zlsgdhte', w1r, eye2, eye2).reshape(depth, L_CMP, 4 * HEAD_DIM, 4 * HEAD_DIM)
    half = L_CMP // 2
    w_lo = w1x[:, :half].reshape(depth, half * 4 * HEAD_DIM, 4 * HEAD_DIM).astype(BF16)
    w_hi = w1x[:, half:].reshape(depth, half * 4 * HEAD_DIM, 4 * HEAD_DIM).astype(BF16)
    w2bd = jnp.einsum('zsde,st,gh->zgsdhte', cmp_w2, eye2, eye2).reshape(depth, 4 * HEAD_DIM, 4 * HEAD_DIM).astype(BF16)
    posx = jnp.broadcast_to(cmp_pos[:, :, :, None, :], (depth, 2, L_CMP, N_KV, HEAD_DIM)).transpose(0, 2, 1, 3, 4)
    pos_lo = posx[:, :half].reshape(depth, 1, half * 4 * HEAD_DIM)
    pos_hi = posx[:, half:].reshape(depth, 1, half * 4 * HEAD_DIM)
    conv_wp = jnp.pad(conv_w, ((0, 0), (0, 1), (0, 0)))
    w_out_b = w_out.astype(BF16)
    dff = ffn_w2.shape[1]
    ffn_wg = ffn_w1[:, :, :dff].astype(BF16)
    ffn_wu = ffn_w1[:, :, dff:].astype(BF16)
    ffn_w2b = ffn_w2.astype(BF16)
    router_p = jnp.pad(router_w, ((0, 0), (0, 0), (0, LANE - N_EXPERTS)))

    mod = _ada(c, ada_w, ada_b).reshape(depth, 6, bsz, 1, d)
    x2 = x.reshape(n, d)
    for l in range(depth):
        shift1, scale1, gate1, shift2, scale2, gate2 = [mod[l, k] for k in range(6)]
        q, kvc, kvsw, gl3, cv = _inproj(x2, scale1, shift1, w_in_p[l], w_gl_t[l], seq, tm)
        q3 = q.reshape(bsz, seq, Q_COLS)
        kvsw3 = kvsw.reshape(bsz, seq, KVSW_COLS)
        kvsw_t = kvsw3.reshape(bsz, n_qt, ATT_TQ, 2 * N_KV, LANE).transpose(0, 3, 1, 4, 2)
        kvcmp, kvcmp_t = _compress(kvc.reshape(bsz, nrow, CMP_STRIDE * 256), pos_lo[l], pos_hi[l], w_lo[l], w_hi[l],
                                   w2bd[l])
        o_cmp, sel_t = _cmp_attn(q3, kvcmp, kvcmp_t, cmp_tab, gl3, n_blocks, n_sel)
        o_sel = _band_attn(q3, kvsw3, kvsw_t, 0, sel_tab, gl3, sel_t, None, 1)
        o_win = _band_attn(q3, kvsw3, kvsw_t, N_KV, win_tab, gl3, None, WINDOW // ATT_TQ, 2)
        o_conv = _conv_branch(cv.reshape(bsz, seq, 1024), conv_wp[l], conv_b[l][None], conv_ln_g[l][None],
                              conv_ln_b[l][None])
        x2 = _outproj(x2, gate1, ln_g[l, 0][None], ln_b[l, 0][None], o_cmp.reshape(n, 512), o_sel.reshape(n, 512),
                      o_win.reshape(n, 512), o_conv.reshape(n, 512), w_out_b[l], seq, tm, alpha)
        if l % 2 == 0:
            act = _ffn1(x2, scale2, shift2, ffn_wg[l // 2], ffn_wu[l // 2], seq, tm)
            x2 = _ffn2(x2, gate2, ln_g[l, 1][None], ln_b[l, 1][None], act, ffn_w2b[l // 2], seq, tm, alpha)
        else:
            x2 = _moe(x2, scale2, shift2, gate2, ln_g[l, 1][None], ln_b[l, 1][None], router_p[l // 2],
                      moe_w1, moe_w2, l // 2, seq, tm_moe, alpha)
    return x2.reshape(bsz, seq, d)
```

```python
import functools
import math

import numpy as np
import jax
import jax.numpy as jnp
from jax import lax
from jax.experimental import pallas as pl
from jax.experimental.pallas import tpu as pltpu

F32 = jnp.float32
BF16 = jnp.bfloat16
HIGHEST = lax.Precision.HIGHEST

HEAD_DIM = 64
N_HEADS = 8
N_KV = 2
Q_PER_KV = N_HEADS // N_KV
N_BRANCH = 3
L_CMP = 32
CMP_STRIDE = 16
L_SEL = 64
N_SEL = 8
WINDOW = 512
CONV_WIDTH = 31
NUM_BUCKETS = 32
MAX_DISTANCE = 128
N_EXPERTS = 8
LN_EPS = 1e-5
NEG_INF = -1e30

LANE = 128
ATT_TQ = 128
ATT_ROWS = Q_PER_KV * ATT_TQ
CONV_TQ = 256
CONV_HALO = 32
CONV_RC = 64
VMEM_LIMIT_BYTES = 52 * 1024 * 1024


def _cparams(*sem):
    return pltpu.CompilerParams(dimension_semantics=sem, vmem_limit_bytes=VMEM_LIMIT_BYTES)


def _dot(a, b):
    return jnp.dot(a, b, preferred_element_type=F32)


def _dot_nt(a, b, precision=None):
    return lax.dot_general(a, b, (((1,), (1,)), ((), ())), precision=precision,
                           preferred_element_type=F32)


def _res_ln(x, gate, y, g, b, alpha):
    z = alpha * x + (1.0 + gate) * y
    mu = jnp.mean(z, axis=-1, keepdims=True)
    zc = z - mu
    var = jnp.mean(zc * zc, axis=-1, keepdims=True)
    return zc * lax.rsqrt(var + LN_EPS) * g + b


def _ada_kernel(c_ref, w_ref, b_ref, o_ref):
    sc = jax.nn.silu(c_ref[...])
    o_ref[0, 0] = jnp.dot(sc, w_ref[0], precision=HIGHEST, preferred_element_type=F32) + b_ref[0]


def _ada(c, ada_w, ada_b):
    depth, d, _ = ada_w.shape
    bsz = c.shape[0]
    return pl.pallas_call(
        _ada_kernel,
        out_shape=jax.ShapeDtypeStruct((depth, 6, bsz, d), F32),
        grid=(depth, 6),
        in_specs=[pl.BlockSpec((bsz, d), lambda l, k: (0, 0)),
                  pl.BlockSpec((1, d, d), lambda l, k: (l, 0, k)),
                  pl.BlockSpec((1, 1, d), lambda l, k: (l, 0, k))],
        out_specs=pl.BlockSpec((1, 1, bsz, d), lambda l, k: (l, k, 0, 0)),
        compiler_params=_cparams("parallel", "parallel"),
        name="ada_mod",
    )(c, ada_w, ada_b.reshape(depth, 1, 6 * d))


Q_COLS = N_HEADS * LANE
KVC_COLS = 4 * HEAD_DIM
KVSW_COLS = 8 * HEAD_DIM
GL_ROWS = N_KV * LANE


def _inproj_kernel(x_ref, sc_ref, sh_ref, w_ref, wgl_ref, q_ref, kvc_ref, kvsw_ref, glt_ref, cv_ref):
    h = (x_ref[...] * (1.0 + sc_ref[0]) + sh_ref[0]).astype(BF16)
    c0, c1, c2 = Q_COLS, Q_COLS + KVC_COLS, Q_COLS + KVC_COLS + KVSW_COLS
    q_ref[...] = (_dot(h, w_ref[:, 0:c0]) * (HEAD_DIM ** -0.5)).astype(BF16)
    kvc_ref[...] = _dot(h, w_ref[:, c0:c1])
    kvsw_ref[...] = _dot(h, w_ref[:, c1:c2]).astype(BF16)
    cv_ref[...] = _dot(h, w_ref[:, c2:])
    glt_ref[...] = _dot_nt(wgl_ref[...], h)


def _inproj(x2, scale, shift, w, wgl_t, seq, tm):
    n, d = x2.shape
    per_b = seq // tm
    d_cv = w.shape[1] - (Q_COLS + KVC_COLS + KVSW_COLS)
    row = lambda i: (i, 0)
    mod = lambda i: (i // per_b, 0, 0)
    const = lambda i: (0, 0)
    return pl.pallas_call(
        _inproj_kernel,
        out_shape=(jax.ShapeDtypeStruct((n, Q_COLS), BF16), jax.ShapeDtypeStruct((n, KVC_COLS), F32),
                   jax.ShapeDtypeStruct((n, KVSW_COLS), BF16), jax.ShapeDtypeStruct((GL_ROWS, n), F32),
                   jax.ShapeDtypeStruct((n, d_cv), F32)),
        grid=(n // tm,),
        in_specs=[pl.BlockSpec((tm, d), row), pl.BlockSpec((1, 1, d), mod), pl.BlockSpec((1, 1, d), mod),
                  pl.BlockSpec(w.shape, const), pl.BlockSpec(wgl_t.shape, const)],
        out_specs=(pl.BlockSpec((tm, Q_COLS), row), pl.BlockSpec((tm, KVC_COLS), row),
                   pl.BlockSpec((tm, KVSW_COLS), row), pl.BlockSpec((GL_ROWS, tm), lambda i: (0, i)),
                   pl.BlockSpec((tm, d_cv), row)),
        compiler_params=_cparams("parallel"),
        name="in_proj",
    )(x2, scale, shift, w, wgl_t)


def _compress_kernel(a_ref, plo_ref, phi_ref, wlo_ref, whi_ref, w2_ref, o_ref, ot_ref):
    a = a_ref[0]
    lo = _dot((a + plo_ref[...]).astype(BF16), wlo_ref[...])
    hi = _dot((a + phi_ref[...]).astype(BF16), whi_ref[...])
    nrow = lo.shape[0]
    pre = lo + pltpu.roll(hi, nrow - 1, 0)
    act = jax.nn.gelu(pre, approximate=True)
    out = _dot(act.astype(BF16), w2_ref[...])
    o_ref[0] = out.astype(BF16)
    ot_ref[0] = out.T.astype(BF16)


def _compress(kvc3, pos_lo, pos_hi, wlo, whi, w2bd):
    bsz, nrow, width = kvc3.shape
    const = lambda b: (0, 0)
    return pl.pallas_call(
        _compress_kernel,
        out_shape=(jax.ShapeDtypeStruct((bsz, nrow, 256), BF16), jax.ShapeDtypeStruct((bsz, 256, nrow), BF16)),
        grid=(bsz,),
        in_specs=[pl.BlockSpec((1, nrow, width), lambda b: (b, 0, 0)),
                  pl.BlockSpec((1, width), const), pl.BlockSpec((1, width), const),
                  pl.BlockSpec((width, 256), const), pl.BlockSpec((width, 256), const),
                  pl.BlockSpec((256, 256), const)],
        out_specs=(pl.BlockSpec((1, nrow, 256), lambda b: (b, 0, 0)),
                   pl.BlockSpec((1, 256, nrow), lambda b: (b, 0, 0))),
        compiler_params=_cparams("parallel"),
        name="compress_mlp",
    )(kvc3, pos_lo, pos_hi, wlo, whi, w2bd)


def _stack_heads(q):
    return jnp.concatenate([q[:, r * LANE:(r + 1) * LANE] for r in range(Q_PER_KV)], axis=0)


def _gated_heads(o_t, gl_t, branch):
    tq = gl_t.shape[1]
    gates = jnp.concatenate([jax.nn.sigmoid(gl_t[branch * Q_PER_KV + r:branch * Q_PER_KV + r + 1, :])
                             for r in range(Q_PER_KV)], axis=1)
    og = (o_t * gates).astype(BF16)
    ii = lax.broadcasted_iota(jnp.int32, (tq, tq), 0)
    kk = lax.broadcasted_iota(jnp.int32, (tq, tq), 1)
    eye = jnp.where(ii == kk, 1.0, 0.0).astype(BF16)
    halves = []
    for pair in range(Q_PER_KV // 2):
        two_heads = jnp.concatenate([og[:, (2 * pair) * tq:(2 * pair + 1) * tq],
                                     og[:, (2 * pair + 1) * tq:(2 * pair + 2) * tq]], axis=0)
        halves.append(_dot_nt(eye, two_heads))
    return jnp.concatenate(halves, axis=1).astype(BF16)


def _cmp_attn_kernel(q_ref, kv_ref, kvt_ref, tab_ref, gl_ref, o_ref, sel_ref, *, n_blocks, n_sel):
    qt = pl.program_id(2)
    tq = ATT_TQ
    qs = _stack_heads(q_ref[0])
    bias = tab_ref[0, 0]
    s = _dot_nt(kv_ref[0], qs) + bias
    valid = bias > 0.5 * NEG_INF
    m = jnp.max(s, axis=0, keepdims=True)
    p = jnp.where(valid, jnp.exp(s - m), 0.0)
    l = jnp.sum(p, axis=0, keepdims=True)
    p = p / jnp.where(l > 0.0, l, 1.0)
    o_t = _dot(kvt_ref[0, HEAD_DIM:, :], p.astype(BF16))
    o_ref[0] = _gated_heads(o_t, gl_ref[...], 0)

    psum = p[:, 0:tq] + p[:, tq:2 * tq] + p[:, 2 * tq:3 * tq] + p[:, 3 * tq:4 * tq]
    jj = lax.broadcasted_iota(jnp.int32, (n_blocks, LANE), 0)
    nn = lax.broadcasted_iota(jnp.int32, (n_blocks, LANE), 1)
    overlap_t = jnp.where((CMP_STRIDE * nn < L_SEL * jj + L_SEL) & (CMP_STRIDE * nn + L_CMP > L_SEL * jj),
                          1.0, 0.0).astype(F32)
    imp_t = jnp.dot(overlap_t, psum, precision=HIGHEST, preferred_element_type=F32)

    jb = lax.broadcasted_iota(jnp.int32, (n_blocks, tq), 0)
    tb = (qt * tq + lax.broadcasted_iota(jnp.int32, (n_blocks, tq), 1)) // L_SEL
    score = jnp.where(jb > tb, -jnp.inf,
                      jnp.where((jb == 0) | (jb == tb) | (jb == tb - 1), jnp.inf, imp_t))
    rank = jnp.zeros((n_blocks, tq), F32)
    for jp in range(n_blocks):
        row = score[jp:jp + 1, :]
        earlier = jnp.where(jb > jp, 1.0, 0.0)
        rank = rank + jnp.where(row > score, 1.0, 0.0) + jnp.where(row == score, earlier, 0.0)
    sel_ref[0, 0] = jnp.where(rank < float(n_sel), 1.0, 0.0).astype(BF16)


def _cmp_attn(q3, kvcmp, kvcmp_t, tab, gl3, n_blocks, n_sel):
    bsz, seq, _ = q3.shape
    nqt = seq // ATT_TQ
    n_cmp = kvcmp.shape[1]
    kern = functools.partial(_cmp_attn_kernel, n_blocks=n_blocks, n_sel=n_sel)
    return pl.pallas_call(
        kern,
        out_shape=(jax.ShapeDtypeStruct((bsz, seq, 512), BF16),
                   jax.ShapeDtypeStruct((bsz, N_KV, n_blocks, seq), BF16)),
        grid=(bsz, N_KV, nqt),
        in_specs=[pl.BlockSpec((1, ATT_TQ, Q_PER_KV * LANE), lambda b, g, t: (b, t, g)),
                  pl.BlockSpec((1, n_cmp, LANE), lambda b, g, t: (b, 0, g)),
                  pl.BlockSpec((1, LANE, n_cmp), lambda b, g, t: (b, g, 0)),
                  pl.BlockSpec((1, 1, n_cmp, ATT_ROWS), lambda b, g, t: (g, t, 0, 0)),
                  pl.BlockSpec((LANE, ATT_TQ), lambda b, g, t: (g, b * nqt + t))],
        out_specs=(pl.BlockSpec((1, ATT_TQ, 256), lambda b, g, t: (b, t, g)),
                   pl.BlockSpec((1, 1, n_blocks, ATT_TQ), lambda b, g, t: (b, g, 0, t))),
        compiler_params=_cparams("parallel", "parallel", "parallel"),
        name="cmp_attn",
    )(q3, kvcmp, kvcmp_t, tab, gl3)


def _band_attn_kernel(*refs, n_delta, span, group, branch, use_sel):
    if use_sel:
        q_ref, kv_ref, kvt_ref, tab_ref, gl_ref, sel_ref, o_ref, s_sc = refs
    else:
        q_ref, kv_ref, kvt_ref, tab_ref, gl_ref, o_ref, s_sc = refs
    qt = pl.program_id(2)
    qs = _stack_heads(q_ref[0])
    tq = ATT_TQ
    n_chunks = kv_ref.shape[1] // tq
    c_base = qt - span if span is not None else 0
    if use_sel:
        sel_t = sel_ref[0, 0]
        nb = sel_t.shape[0]
        kb = lax.broadcasted_iota(jnp.int32, (tq, nb), 0) // L_SEL
        jj = lax.broadcasted_iota(jnp.int32, (tq, nb), 1)

    def chunk_ids(gi, u):
        c = c_base + gi * group + u
        inside = (c >= 0) & (c <= qt)
        dd = jnp.where(inside, jnp.minimum(qt - c, n_delta - 1), n_delta)
        return jnp.clip(c, 0, n_chunks - 1), dd

    def rows(gi, u):
        return pl.ds(pl.multiple_of((gi * group + u) * tq, tq), tq)

    def scores(gi, m):
        for u in range(group):
            cc, dd = chunk_ids(gi, u)
            kv = kv_ref[0, pl.ds(pl.multiple_of(cc * tq, tq), tq), :]
            s = _dot_nt(kv, qs) + tab_ref[0, dd]
            if use_sel:
                expand_t = jnp.where(jj == cc * (tq // L_SEL) + kb, 1.0, 0.0).astype(BF16)
                allowed = _dot(expand_t, sel_t)
                madd = (allowed - 1.0) * (-NEG_INF)
                s = s + jnp.concatenate([madd] * Q_PER_KV, axis=1)
            s_sc[rows(gi, u), :] = s
            m = jnp.maximum(m, jnp.max(s, axis=0, keepdims=True))
        return m

    m0 = jnp.full((1, ATT_ROWS), NEG_INF, F32)
    if span is not None:
        n_groups = 1
        m = scores(0, m0)
    else:
        n_groups = qt // group + 1
        m = lax.fori_loop(0, n_groups, scores, m0)

    def values(gi, carry):
        l, acc = carry
        for u in range(group):
            cc, _ = chunk_ids(gi, u)
            p = jnp.exp(s_sc[rows(gi, u), :] - m)
            l = l + jnp.sum(p, axis=0, keepdims=True)
            acc = acc + _dot(kvt_ref[0, 0, cc, HEAD_DIM:, :], p.astype(BF16))
        return l, acc

    init = (jnp.zeros((1, ATT_ROWS), F32), jnp.zeros((HEAD_DIM, ATT_ROWS), F32))
    if span is not None:
        l, acc = values(0, init)
    else:
        l, acc = lax.fori_loop(0, n_groups, values, init)
    o_ref[0] = _gated_heads(acc / l, gl_ref[...], branch)


def _band_attn(q3, kvsw3, kvsw_t, kv_block0, tab, gl3, sel_t, span, branch):
    bsz, seq, _ = q3.shape
    nqt = seq // ATT_TQ
    use_sel = sel_t is not None
    group = span + 1 if span is not None else 4
    slots = group if span is not None else -(-nqt // group) * group
    kern = functools.partial(_band_attn_kernel, n_delta=tab.shape[1] - 1, span=span, group=group, branch=branch,
                             use_sel=use_sel)
    in_specs = [pl.BlockSpec((1, ATT_TQ, Q_PER_KV * LANE), lambda b, g, t: (b, t, g)),
                pl.BlockSpec((1, seq, LANE), lambda b, g, t: (b, 0, kv_block0 + g)),
                pl.BlockSpec((1, 1) + kvsw_t.shape[2:], lambda b, g, t: (b, kv_block0 + g, 0, 0, 0)),
                pl.BlockSpec((1,) + tab.shape[1:], lambda b, g, t: (g, 0, 0, 0)),
                pl.BlockSpec((LANE, ATT_TQ), lambda b, g, t: (g, b * nqt + t))]
    args = [q3, kvsw3, kvsw_t, tab, gl3]
    if use_sel:
        in_specs.append(pl.BlockSpec((1, 1, sel_t.shape[2], ATT_TQ), lambda b, g, t: (b, g, 0, t)))
        args.append(sel_t)
    return pl.pallas_call(
        kern,
        out_shape=jax.ShapeDtypeStruct((bsz, seq, 512), BF16),
        grid=(bsz, N_KV, nqt),
        in_specs=in_specs,
        out_specs=pl.BlockSpec((1, ATT_TQ, 256), lambda b, g, t: (b, t, g)),
        scratch_shapes=[pltpu.VMEM((slots * ATT_TQ, ATT_ROWS), F32)],
        compiler_params=_cparams("parallel", "parallel", "parallel"),
        name="sel_attn" if use_sel else "win_attn",
    )(*args)


def _conv_kernel(cur_ref, prev_ref, w_ref, cb_ref, g_ref, b_ref, o_ref, ybuf, cbuf):
    i = pl.program_id(1)
    dc = o_ref.shape[-1]
    cur = cur_ref[0]
    prev = prev_ref[0]
    ybuf[CONV_HALO:, :] = cur[:, :dc] * jax.nn.sigmoid(cur[:, dc:])
    halo = prev[:, :dc] * jax.nn.sigmoid(prev[:, dc:])
    ybuf[0:CONV_HALO, :] = jnp.where(i > 0, halo, 0.0)
    off = CONV_HALO - (CONV_WIDTH - 1)
    for lc in range(dc // LANE):
        cols = slice(lc * LANE, (lc + 1) * LANE)
        for rc in range(CONV_TQ // CONV_RC):
            acc = jnp.zeros((CONV_RC, LANE), F32)
            for k in range(CONV_WIDTH):
                r0 = rc * CONV_RC + off + k
                acc = acc + w_ref[k:k + 1, cols] * ybuf[r0:r0 + CONV_RC, cols]
            cbuf[rc * CONV_RC:(rc + 1) * CONV_RC, cols] = acc
    z = cbuf[...] + cb_ref[...]
    mu = jnp.mean(z, axis=-1, keepdims=True)
    zc = z - mu
    var = jnp.mean(zc * zc, axis=-1, keepdims=True)
    y = zc * lax.rsqrt(var + LN_EPS) * g_ref[...] + b_ref[...]
    o_ref[0] = jax.nn.silu(y).astype(BF16)


def _conv_branch(cv3, w, cb, g, b):
    bsz, seq, two_dc = cv3.shape
    dc = two_dc // 2
    per = CONV_TQ // CONV_HALO
    const = lambda bb, i: (0, 0)
    return pl.pallas_call(
        _conv_kernel,
        out_shape=jax.ShapeDtypeStruct((bsz, seq, dc), BF16),
        grid=(bsz, seq // CONV_TQ),
        in_specs=[pl.BlockSpec((1, CONV_TQ, two_dc), lambda bb, i: (bb, i, 0)),
                  pl.BlockSpec((1, CONV_HALO, two_dc), lambda bb, i: (bb, jnp.maximum(i * per - 1, 0), 0)),
                  pl.BlockSpec(w.shape, const), pl.BlockSpec((1, dc), const),
                  pl.BlockSpec((1, dc), const), pl.BlockSpec((1, dc), const)],
        out_specs=pl.BlockSpec((1, CONV_TQ, dc), lambda bb, i: (bb, i, 0)),
        scratch_shapes=[pltpu.VMEM((CONV_HALO + CONV_TQ, dc), F32), pltpu.VMEM((CONV_TQ, dc), F32)],
        compiler_params=_cparams("parallel", "parallel"),
        name="conv_branch",
    )(cv3, cv3, w, cb, g, b)


def _outproj_kernel(x_ref, gate_ref, g_ref, b_ref, oc_ref, os_ref, ow_ref, ocv_ref, w_ref, o_ref, *, alpha):
    nsa = (oc_ref[...].astype(F32) + os_ref[...].astype(F32) + ow_ref[...].astype(F32)).astype(BF16)
    half = nsa.shape[1]
    y = _dot(nsa, w_ref[0:half, :]) + _dot(ocv_ref[...], w_ref[half:, :])
    o_ref[...] = _res_ln(x_ref[...], gate_ref[0], y, g_ref[...], b_ref[...], alpha)


def _outproj(x2, gate, lng, lnb, o_cmp, o_sel, o_win, o_conv, w, seq, tm, alpha):
    n, d = x2.shape
    per_b = seq // tm
    row = lambda i: (i, 0)
    const = lambda i: (0, 0)
    half = o_cmp.shape[1]
    return pl.pallas_call(
        functools.partial(_outproj_kernel, alpha=alpha),
        out_shape=jax.ShapeDtypeStruct((n, d), F32),
        grid=(n // tm,),
        in_specs=[pl.BlockSpec((tm, d), row), pl.BlockSpec((1, 1, d), lambda i: (i // per_b, 0, 0)),
                  pl.BlockSpec((1, d), const), pl.BlockSpec((1, d), const),
                  pl.BlockSpec((tm, half), row), pl.BlockSpec((tm, half), row), pl.BlockSpec((tm, half), row),
                  pl.BlockSpec((tm, o_conv.shape[1]), row), pl.BlockSpec(w.shape, const)],
        out_specs=pl.BlockSpec((tm, d), row),
        compiler_params=_cparams("parallel"),
        name="out_proj_ln",
    )(x2, gate, lng, lnb, o_cmp, o_sel, o_win, o_conv, w)


def _ffn1_kernel(x_ref, sc_ref, sh_ref, wg_ref, wu_ref, o_ref, *, tn):
    h = (x_ref[...] * (1.0 + sc_ref[0]) + sh_ref[0]).astype(BF16)
    for j in range(o_ref.shape[1] // tn):
        cols = slice(j * tn, (j + 1) * tn)
        gate = _dot(h, wg_ref[:, cols])
        up = _dot(h, wu_ref[:, cols])
        o_ref[:, cols] = (jax.nn.silu(gate) * up).astype(BF16)


def _ffn1(x2, scale, shift, wg, wu, seq, tm):
    n, d = x2.shape
    dff = wg.shape[1]
    per_b = seq // tm
    row = lambda i: (i, 0)
    mod = lambda i: (i // per_b, 0, 0)
    const = lambda i: (0, 0)
    return pl.pallas_call(
        functools.partial(_ffn1_kernel, tn=256),
        out_shape=jax.ShapeDtypeStruct((n, dff), BF16),
        grid=(n // tm,),
        in_specs=[pl.BlockSpec((tm, d), row), pl.BlockSpec((1, 1, d), mod), pl.BlockSpec((1, 1, d), mod),
                  pl.BlockSpec((d, dff), const), pl.BlockSpec((d, dff), const)],
        out_specs=pl.BlockSpec((tm, dff), row),
        compiler_params=_cparams("parallel"),
        name="ffn_up",
    )(x2, scale, shift, wg, wu)


def _ffn2_kernel(x_ref, gate_ref, g_ref, b_ref, a_ref, w_ref, o_ref, *, alpha):
    y = _dot(a_ref[...], w_ref[...])
    o_ref[...] = _res_ln(x_ref[...], gate_ref[0], y, g_ref[...], b_ref[...], alpha)


def _ffn2(x2, gate, lng, lnb, act, w, seq, tm, alpha):
    n, d = x2.shape
    per_b = seq // tm
    row = lambda i: (i, 0)
    const = lambda i: (0, 0)
    return pl.pallas_call(
        functools.partial(_ffn2_kernel, alpha=alpha),
        out_shape=jax.ShapeDtypeStruct((n, d), F32),
        grid=(n // tm,),
        in_specs=[pl.BlockSpec((tm, d), row), pl.BlockSpec((1, 1, d), lambda i: (i // per_b, 0, 0)),
                  pl.BlockSpec((1, d), const), pl.BlockSpec((1, d), const),
                  pl.BlockSpec((tm, act.shape[1]), row), pl.BlockSpec(w.shape, const)],
        out_specs=pl.BlockSpec((tm, d), row),
        compiler_params=_cparams("parallel"),
        name="ffn_down_ln",
    )(x2, gate, lng, lnb, act, w)


MOE_TM = 512
MOE_FC = 512
GATHER_ROWS = 1024


def _router_kernel(x_ref, sc_ref, sh_ref, rw_ref, h_ref, ti_ref, tw_ref):
    hf = x_ref[...] * (1.0 + sc_ref[0]) + sh_ref[0]
    h_ref[...] = hf
    lane = lax.broadcasted_iota(jnp.int32, (hf.shape[0], LANE), 1)
    logits = jnp.dot(hf, rw_ref[...], precision=HIGHEST, preferred_element_type=F32)
    logits = jnp.where(lane < N_EXPERTS, logits, -jnp.inf)
    m1 = jnp.max(logits, axis=-1, keepdims=True)
    i1 = jnp.min(jnp.where(logits == m1, lane, LANE), axis=-1, keepdims=True)
    rest = jnp.where(lane == i1, -jnp.inf, logits)
    m2 = jnp.max(rest, axis=-1, keepdims=True)
    i2 = jnp.min(jnp.where(rest == m2, lane, LANE), axis=-1, keepdims=True)
    e2 = jnp.exp(m2 - m1)
    ti_ref[...] = jnp.where(lane == 0, i1, jnp.where(lane == 1, i2, 0))
    tw_ref[...] = jnp.where(lane == 0, 1.0 / (1.0 + e2), jnp.where(lane == 1, e2 / (1.0 + e2), 0.0))


def _router(x2, scale, shift, router_p, seq, tm):
    n, d = x2.shape
    per_b = seq // tm
    row = lambda i: (i, 0)
    mod = lambda i: (i // per_b, 0, 0)
    return pl.pallas_call(
        _router_kernel,
        out_shape=(jax.ShapeDtypeStruct((n, d), F32), jax.ShapeDtypeStruct((n, LANE), jnp.int32),
                   jax.ShapeDtypeStruct((n, LANE), F32)),
        grid=(n // tm,),
        in_specs=[pl.BlockSpec((tm, d), row), pl.BlockSpec((1, 1, d), mod), pl.BlockSpec((1, 1, d), mod),
                  pl.BlockSpec(router_p.shape, lambda i: (0, 0))],
        out_specs=(pl.BlockSpec((tm, d), row), pl.BlockSpec((tm, LANE), row), pl.BlockSpec((tm, LANE), row)),
        compiler_params=_cparams("parallel"),
        name="moe_router",
    )(x2, scale, shift, router_p)


def _gather_rows_kernel(idx_ref, src_ref, o_ref, sem):
    rows = o_ref.shape[0]
    base = pl.program_id(0) * rows

    def issue(r, carry):
        pltpu.make_async_copy(src_ref.at[pl.ds(idx_ref[base + r], 1)], o_ref.at[pl.ds(r, 1)], sem).start()
        return carry

    lax.fori_loop(0, rows, issue, 0, unroll=8)
    pltpu.make_async_copy(src_ref.at[pl.ds(0, rows)], o_ref, sem).wait()


def _gather_rows(idx, src):
    n_out = idx.shape[0]
    d = src.shape[1]
    return pl.pallas_call(
        _gather_rows_kernel,
        out_shape=jax.ShapeDtypeStruct((n_out, d), src.dtype),
        grid_spec=pltpu.PrefetchScalarGridSpec(
            num_scalar_prefetch=1,
            grid=(n_out // GATHER_ROWS,),
            in_specs=[pl.BlockSpec(memory_space=pl.ANY)],
            out_specs=pl.BlockSpec((GATHER_ROWS, d), lambda i, idx_ref: (i, 0)),
            scratch_shapes=[pltpu.SemaphoreType.DMA]),
        compiler_params=_cparams("arbitrary"),
        name="row_gather",
    )(idx, src)


def _expert_up_kernel(te_ref, nu_ref, x_ref, wg_ref, wu_ref, o_ref, xb_sc):
    i = pl.program_id(0)
    f = pl.program_id(1)

    @pl.when(i < nu_ref[0])
    def _compute():
        @pl.when(f == 0)
        def _cast():
            xb_sc[...] = x_ref[...].astype(BF16)
        h = xb_sc[...]
        gate = _dot(h, wg_ref[0, 0])
        up = _dot(h, wu_ref[0, 0])
        o_ref[...] = (jax.nn.silu(gate) * up).astype(BF16)

    @pl.when(i >= nu_ref[0])
    def _unused_tile():
        o_ref[...] = jnp.zeros_like(o_ref)


def _expert_up(tile_expert, n_used, xg, w1b, layer):
    p_rows, d = xg.shape
    nf = w1b.shape[3] // (2 * MOE_FC)
    return pl.pallas_call(
        _expert_up_kernel,
        out_shape=jax.ShapeDtypeStruct((p_rows, nf * MOE_FC), BF16),
        grid_spec=pltpu.PrefetchScalarGridSpec(
            num_scalar_prefetch=2,
            grid=(p_rows // MOE_TM, nf),
            in_specs=[pl.BlockSpec((MOE_TM, d), lambda i, f, te, nu: (i, 0)),
                      pl.BlockSpec((1, 1, d, MOE_FC), lambda i, f, te, nu: (layer, te[i], 0, f)),
                      pl.BlockSpec((1, 1, d, MOE_FC), lambda i, f, te, nu: (layer, te[i], 0, nf + f))],
            out_specs=pl.BlockSpec((MOE_TM, MOE_FC), lambda i, f, te, nu: (i, f)),
            scratch_shapes=[pltpu.VMEM((MOE_TM, d), BF16)]),
        compiler_params=_cparams("parallel", "arbitrary"),
        name="moe_up",
    )(tile_expert, n_used, xg, w1b, w1b)


def _expert_down_kernel(te_ref, nu_ref, a_ref, w_ref, o_ref):
    i = pl.program_id(0)

    @pl.when(i < nu_ref[0])
    def _compute():
        o_ref[...] = _dot(a_ref[...], w_ref[0, 0])

    @pl.when(i >= nu_ref[0])
    def _unused_tile():
        o_ref[...] = jnp.zeros_like(o_ref)


def _expert_down(tile_expert, n_used, act, w2b, layer):
    p_rows, dffe = act.shape
    d = w2b.shape[3]
    return pl.pallas_call(
        _expert_down_kernel,
        out_shape=jax.ShapeDtypeStruct((p_rows, d), F32),
        grid_spec=pltpu.PrefetchScalarGridSpec(
            num_scalar_prefetch=2,
            grid=(p_rows // MOE_TM,),
            in_specs=[pl.BlockSpec((MOE_TM, dffe), lambda i, te, nu: (i, 0)),
                      pl.BlockSpec((1, 1, dffe, d), lambda i, te, nu: (layer, te[i], 0, 0))],
            out_specs=pl.BlockSpec((MOE_TM, d), lambda i, te, nu: (i, 0))),
        compiler_params=_cparams("parallel"),
        name="moe_down",
    )(tile_expert, n_used, act, w2b)


def _moe_combine_kernel(x_ref, gate_ref, g_ref, b_ref, tw_ref, y1_ref, y2_ref, o_ref, *, alpha):
    tw = tw_ref[...]
    y = tw[:, 0:1] * y1_ref[...] + tw[:, 1:2] * y2_ref[...]
    o_ref[...] = _res_ln(x_ref[...], gate_ref[0], y, g_ref[...], b_ref[...], alpha)


def _moe_combine(x2, gate, lng, lnb, tw, yg, seq, tm, alpha):
    n, d = x2.shape
    per_b = seq // tm
    nt = n // tm
    row = lambda i: (i, 0)
    const = lambda i: (0, 0)
    return pl.pallas_call(
        functools.partial(_moe_combine_kernel, alpha=alpha),
        out_shape=jax.ShapeDtypeStruct((n, d), F32),
        grid=(nt,),
        in_specs=[pl.BlockSpec((tm, d), row), pl.BlockSpec((1, 1, d), lambda i: (i // per_b, 0, 0)),
                  pl.BlockSpec((1, d), const), pl.BlockSpec((1, d), const), pl.BlockSpec((tm, LANE), row),
                  pl.BlockSpec((tm, d), row), pl.BlockSpec((tm, d), lambda i: (nt + i, 0))],
        out_specs=pl.BlockSpec((tm, d), row),
        compiler_params=_cparams("parallel"),
        name="moe_combine_ln",
    )(x2, gate, lng, lnb, tw, yg, yg)


def _route_tables(ti, n_rows):
    n = ti.shape[0]
    e_flat = jnp.concatenate([ti[:, 0], ti[:, 1]])
    onehot = (e_flat[:, None] == jnp.arange(N_EXPERTS, dtype=jnp.int32)[None, :]).astype(jnp.int32)
    cum = jnp.cumsum(onehot, axis=0)
    counts = cum[-1]
    padded = (counts + MOE_TM - 1) // MOE_TM * MOE_TM
    ends = jnp.cumsum(padded)
    pos = jnp.sum(onehot * (cum - 1 + (ends - padded)[None, :]), axis=1)
    tok = jnp.tile(jnp.arange(n, dtype=jnp.int32), 2)
    src = jnp.zeros((n_rows,), jnp.int32).at[pos].set(tok)
    tile_start = jnp.arange(n_rows // MOE_TM, dtype=jnp.int32) * MOE_TM
    tile_expert = jnp.minimum(jnp.sum((tile_start[:, None] >= ends[None, :]).astype(jnp.int32), axis=1),
                              N_EXPERTS - 1)
    return pos.astype(jnp.int32), src, tile_expert, (ends[-1:] // MOE_TM).astype(jnp.int32)


def _moe(x2, scale, shift, gate, lng, lnb, router_p, w1b, w2b, layer, seq, tm, alpha):
    n, d = x2.shape
    n_rows = 2 * n + N_EXPERTS * MOE_TM
    assert n_rows % GATHER_ROWS == 0 and (2 * n) % GATHER_ROWS == 0
    h, ti, tw = _router(x2, scale, shift, router_p, seq, tm)
    pos, src, tile_expert, n_used = _route_tables(ti, n_rows)
    xg = _gather_rows(src, h)
    act = _expert_up(tile_expert, n_used, xg, w1b, layer)
    y = _expert_down(tile_expert, n_used, act, w2b, layer)
    yg = _gather_rows(pos, y)
    return _moe_combine(x2, gate, lng, lnb, tw, yg, seq, tm, alpha)


def _rel_bucket_np(dist):
    n = np.maximum(dist, 0)
    max_exact = NUM_BUCKETS // 2
    nf = np.maximum(n, 1).astype(np.float32)
    large = max_exact + (np.log(nf / np.float32(max_exact)) / np.float32(math.log(MAX_DISTANCE / max_exact))
                         * np.float32(NUM_BUCKETS - max_exact)).astype(np.int32)
    large = np.minimum(large, NUM_BUCKETS - 1)
    return np.where(n < max_exact, n, large).astype(np.int32)


def _bias_lookup(rel_bias, dist, valid):
    vals = rel_bias[jnp.asarray(_rel_bucket_np(dist))]
    return jnp.where(jnp.asarray(valid)[..., None], vals, NEG_INF)


def _key_major(vals, add_masked_tile=False):
    n_tiles, tq, tk, _ = vals.shape
    if add_masked_tile:
        vals = jnp.concatenate([vals, jnp.full((1,) + vals.shape[1:], NEG_INF, vals.dtype)], axis=0)
        n_tiles += 1
    vals = vals.reshape(n_tiles, tq, tk, N_KV, Q_PER_KV).transpose(3, 0, 2, 4, 1)
    return vals.reshape(N_KV, n_tiles, tk, Q_PER_KV * tq).astype(F32)


def _band_dist(n_delta):
    i = np.arange(ATT_TQ)[None, :, None]
    k = np.arange(ATT_TQ)[None, None, :]
    return ATT_TQ * np.arange(n_delta)[:, None, None] + i - k


def _cmp_bias_table(rel_bias, n_qt, n_cmp_pad):
    per_tile = ATT_TQ // CMP_STRIDE
    shift = per_tile * (n_qt - 1)
    rel_n = np.arange(n_cmp_pad + shift) - shift
    dist = np.arange(ATT_TQ)[:, None] - (CMP_STRIDE * rel_n[None, :] + L_CMP - 1)
    base = _bias_lookup(rel_bias, dist, dist >= 0)
    tiles = [base[:, shift - per_tile * qt: shift - per_tile * qt + n_cmp_pad] for qt in range(n_qt)]
    return _key_major(jnp.stack(tiles))


def _inproj_weight(w_in):
    depth, d, _ = w_in.shape
    d_nsa = N_HEADS * HEAD_DIM
    d_kv = N_KV * HEAD_DIM
    base_ks = d_nsa + 2 * d_kv
    base_gl = d_nsa + 6 * d_kv
    base_cv = base_gl + N_HEADS * N_BRANCH
    wq = jnp.pad(w_in[:, :, :d_nsa].reshape(depth, d, N_HEADS, HEAD_DIM),
                 ((0, 0), (0, 0), (0, 0), (0, LANE - HEAD_DIM))).reshape(depth, d, Q_COLS)
    parts = [wq, w_in[:, :, d_nsa:base_ks]]
    for pair in range(2):
        kbase = base_ks + pair * 2 * d_kv
        for g in range(N_KV):
            parts.append(w_in[:, :, kbase + g * HEAD_DIM: kbase + (g + 1) * HEAD_DIM])
            parts.append(w_in[:, :, kbase + d_kv + g * HEAD_DIM: kbase + d_kv + (g + 1) * HEAD_DIM])
    parts.append(w_in[:, :, base_cv:])
    gl = w_in[:, :, base_gl:base_cv].reshape(depth, d, N_KV, Q_PER_KV, N_BRANCH).transpose(0, 2, 4, 3, 1)
    gl = jnp.pad(gl.reshape(depth, N_KV, N_BRANCH * Q_PER_KV, d),
                 ((0, 0), (0, 0), (0, LANE - N_BRANCH * Q_PER_KV), (0, 0)))
    return jnp.concatenate(parts, axis=-1).astype(BF16), gl.reshape(depth, GL_ROWS, d).astype(BF16)


def kernel(x, c, w_in, cmp_pos, cmp_w1, cmp_w2, conv_w, conv_b, conv_ln_g, conv_ln_b, w_out, rel_bias,
           ada_w, ada_b, ln_g, ln_b, ffn_w1, ffn_w2, router_w, moe_w1, moe_w2):
    bsz, seq, d = x.shape
    depth = w_in.shape[0]
    n = bsz * seq
    alpha = (2 * depth) ** 0.25
    n_blocks = seq // L_SEL
    n_sel = min(N_SEL, n_blocks)
    n_cmp = (seq - L_CMP) // CMP_STRIDE + 1
    nrow = seq // CMP_STRIDE
    assert d == 1024 and seq % 512 == 0 and nrow == LANE and n_cmp == nrow - 1
    tm = 512

    n_qt = seq // ATT_TQ
    d_win = _band_dist(WINDOW // ATT_TQ + 1)
    win_tab = _key_major(_bias_lookup(rel_bias, d_win, (d_win >= 0) & (d_win < WINDOW)), add_masked_tile=True)
    d_sel = _band_dist(3)
    assert _rel_bucket_np(d_sel[2]).min() == NUM_BUCKETS - 1
    sel_tab = _key_major(_bias_lookup(rel_bias, d_sel, d_sel >= 0), add_masked_tile=True)
    cmp_tab = _cmp_bias_table(rel_bias, n_qt, nrow)

    w_in_p, w_gl_t = _inproj_weight(w_in)
    eye2 = jnp.eye(2, dtype=F32)
    w1r = cmp_w1.reshape(depth, 2, L_CMP, HEAD_DIM, HEAD_DIM)
    w1x = jnp.einsum('zslde,st,gh->zlsgdhte', w1r, eye2, eye2).reshape(depth, L_CMP, 4 * HEAD_DIM, 4 * HEAD_DIM)
    half = L_CMP // 2
    w_lo = w1x[:, :half].reshape(depth, half * 4 * HEAD_DIM, 4 * HEAD_DIM).astype(BF16)
    w_hi = w1x[:, half:].reshape(depth, half * 4 * HEAD_DIM, 4 * HEAD_DIM).astype(BF16)
    w2bd = jnp.einsum('zsde,st,gh->zgsdhte', cmp_w2, eye2, eye2).reshape(depth, 4 * HEAD_DIM, 4 * HEAD_DIM).astype(BF16)
    posx = jnp.broadcast_to(cmp_pos[:, :, :, None, :], (depth, 2, L_CMP, N_KV, HEAD_DIM)).transpose(0, 2, 1, 3, 4)
    pos_lo = posx[:, :half].reshape(depth, 1, half * 4 * HEAD_DIM)
    pos_hi = posx[:, half:].reshape(depth, 1, half * 4 * HEAD_DIM)
    conv_wp = jnp.pad(conv_w, ((0, 0), (0, 1), (0, 0)))
    w_out_b = w_out.astype(BF16)
    dff = ffn_w2.shape[1]
    ffn_wg = ffn_w1[:, :, :dff].astype(BF16)
    ffn_wu = ffn_w1[:, :, dff:].astype(BF16)
    ffn_w2b = ffn_w2.astype(BF16)
    router_p = jnp.pad(router_w, ((0, 0), (0, 0), (0, LANE - N_EXPERTS)))
    moe_w1b = moe_w1.astype(BF16)
    moe_w2b = moe_w2.astype(BF16)

    mod = _ada(c, ada_w, ada_b).reshape(depth, 6, bsz, 1, d)
    x2 = x.reshape(n, d)
    for l in range(depth):
        shift1, scale1, gate1, shift2, scale2, gate2 = [mod[l, k] for k in range(6)]
        q, kvc, kvsw, gl3, cv = _inproj(x2, scale1, shift1, w_in_p[l], w_gl_t[l], seq, tm)
        q3 = q.reshape(bsz, seq, Q_COLS)
        kvsw3 = kvsw.reshape(bsz, seq, KVSW_COLS)
        kvsw_t = kvsw3.reshape(bsz, n_qt, ATT_TQ, 2 * N_KV, LANE).transpose(0, 3, 1, 4, 2)
        kvcmp, kvcmp_t = _compress(kvc.reshape(bsz, nrow, CMP_STRIDE * 256), pos_lo[l], pos_hi[l], w_lo[l], w_hi[l],
                                   w2bd[l])
        o_cmp, sel_t = _cmp_attn(q3, kvcmp, kvcmp_t, cmp_tab, gl3, n_blocks, n_sel)
        o_sel = _band_attn(q3, kvsw3, kvsw_t, 0, sel_tab, gl3, sel_t, None, 1)
        o_win = _band_attn(q3, kvsw3, kvsw_t, N_KV, win_tab, gl3, None, WINDOW // ATT_TQ, 2)
        o_conv = _conv_branch(cv.reshape(bsz, seq, 1024), conv_wp[l], conv_b[l][None], conv_ln_g[l][None],
                              conv_ln_b[l][None])
        x2 = _outproj(x2, gate1, ln_g[l, 0][None], ln_b[l, 0][None], o_cmp.reshape(n, 512), o_sel.reshape(n, 512),
                      o_win.reshape(n, 512), o_conv.reshape(n, 512), w_out_b[l], seq, tm, alpha)
        if l % 2 == 0:
            act = _ffn1(x2, scale2, shift2, ffn_wg[l // 2], ffn_wu[l // 2], seq, tm)
            x2 = _ffn2(x2, gate2, ln_g[l, 1][None], ln_b[l, 1][None], act, ffn_w2b[l // 2], seq, tm, alpha)
        else:
            x2 = _moe(x2, scale2, shift2, gate2, ln_g[l, 1][None], ln_b[l, 1][None], router_p[l // 2],
                      moe_w1b, moe_w2b, l // 2, seq, tm, alpha)
    return x2.reshape(bsz, seq, d)
```

```python
import functools
import math

import numpy as np
import jax
import jax.numpy as jnp
from jax import lax
from jax.experimental import pallas as pl
from jax.experimental.pallas import tpu as pltpu

F32 = jnp.float32
BF16 = jnp.bfloat16
HIGHEST = lax.Precision.HIGHEST

HEAD_DIM = 64
N_HEADS = 8
N_KV = 2
Q_PER_KV = N_HEADS // N_KV
N_BRANCH = 3
L_CMP = 32
CMP_STRIDE = 16
L_SEL = 64
N_SEL = 8
WINDOW = 512
CONV_WIDTH = 31
NUM_BUCKETS = 32
MAX_DISTANCE = 128
N_EXPERTS = 8
LN_EPS = 1e-5
NEG_INF = -1e30

LANE = 128
ATT_TQ = 128
ATT_ROWS = Q_PER_KV * ATT_TQ
CONV_TQ = 256
CONV_HALO = 32
CONV_RC = 64
VMEM_LIMIT_BYTES = 52 * 1024 * 1024


def _cparams(*sem):
    return pltpu.CompilerParams(dimension_semantics=sem, vmem_limit_bytes=VMEM_LIMIT_BYTES)


def _dot(a, b):
    return jnp.dot(a, b, preferred_element_type=F32)


def _dot_nt(a, b, precision=None):
    return lax.dot_general(a, b, (((1,), (1,)), ((), ())), precision=precision,
                           preferred_element_type=F32)


def _res_ln(x, gate, y, g, b, alpha):
    z = alpha * x + (1.0 + gate) * y
    mu = jnp.mean(z, axis=-1, keepdims=True)
    zc = z - mu
    var = jnp.mean(zc * zc, axis=-1, keepdims=True)
    return zc * lax.rsqrt(var + LN_EPS) * g + b


def _ada_kernel(c_ref, w_ref, b_ref, o_ref):
    sc = jax.nn.silu(c_ref[...])
    o_ref[0, 0] = jnp.dot(sc, w_ref[0], precision=HIGHEST, preferred_element_type=F32) + b_ref[0]


def _ada(c, ada_w, ada_b):
    depth, d, _ = ada_w.shape
    bsz = c.shape[0]
    return pl.pallas_call(
        _ada_kernel,
        out_shape=jax.ShapeDtypeStruct((depth, 6, bsz, d), F32),
        grid=(depth, 6),
        in_specs=[pl.BlockSpec((bsz, d), lambda l, k: (0, 0)),
                  pl.BlockSpec((1, d, d), lambda l, k: (l, 0, k)),
                  pl.BlockSpec((1, 1, d), lambda l, k: (l, 0, k))],
        out_specs=pl.BlockSpec((1, 1, bsz, d), lambda l, k: (l, k, 0, 0)),
        compiler_params=_cparams("parallel", "parallel"),
        name="ada_mod",
    )(c, ada_w, ada_b.reshape(depth, 1, 6 * d))


Q_COLS = N_HEADS * LANE
KVC_COLS = 4 * HEAD_DIM
KVSW_COLS = 8 * HEAD_DIM
GL_ROWS = N_KV * LANE


def _inproj_kernel(x_ref, sc_ref, sh_ref, w_ref, wgl_ref, q_ref, kvc_ref, kvsw_ref, glt_ref, cv_ref):
    h = (x_ref[...] * (1.0 + sc_ref[0]) + sh_ref[0]).astype(BF16)
    c0, c1, c2 = Q_COLS, Q_COLS + KVC_COLS, Q_COLS + KVC_COLS + KVSW_COLS
    q_ref[...] = (_dot(h, w_ref[:, 0:c0]) * (HEAD_DIM ** -0.5)).astype(BF16)
    kvc_ref[...] = _dot(h, w_ref[:, c0:c1])
    kvsw_ref[...] = _dot(h, w_ref[:, c1:c2]).astype(BF16)
    cv_ref[...] = _dot(h, w_ref[:, c2:])
    glt_ref[...] = _dot_nt(wgl_ref[...], h)


def _inproj(x2, scale, shift, w, wgl_t, seq, tm):
    n, d = x2.shape
    per_b = seq // tm
    d_cv = w.shape[1] - (Q_COLS + KVC_COLS + KVSW_COLS)
    row = lambda i: (i, 0)
    mod = lambda i: (i // per_b, 0, 0)
    const = lambda i: (0, 0)
    return pl.pallas_call(
        _inproj_kernel,
        out_shape=(jax.ShapeDtypeStruct((n, Q_COLS), BF16), jax.ShapeDtypeStruct((n, KVC_COLS), F32),
                   jax.ShapeDtypeStruct((n, KVSW_COLS), BF16), jax.ShapeDtypeStruct((GL_ROWS, n), F32),
                   jax.ShapeDtypeStruct((n, d_cv), F32)),
        grid=(n // tm,),
        in_specs=[pl.BlockSpec((tm, d), row), pl.BlockSpec((1, 1, d), mod), pl.BlockSpec((1, 1, d), mod),
                  pl.BlockSpec(w.shape, const), pl.BlockSpec(wgl_t.shape, const)],
        out_specs=(pl.BlockSpec((tm, Q_COLS), row), pl.BlockSpec((tm, KVC_COLS), row),
                   pl.BlockSpec((tm, KVSW_COLS), row), pl.BlockSpec((GL_ROWS, tm), lambda i: (0, i)),
                   pl.BlockSpec((tm, d_cv), row)),
        compiler_params=_cparams("parallel"),
        name="in_proj",
    )(x2, scale, shift, w, wgl_t)


def _compress_kernel(a_ref, plo_ref, phi_ref, wlo_ref, whi_ref, w2_ref, o_ref, ot_ref):
    a = a_ref[0]
    lo = _dot((a + plo_ref[...]).astype(BF16), wlo_ref[...])
    hi = _dot((a + phi_ref[...]).astype(BF16), whi_ref[...])
    nrow = lo.shape[0]
    pre = lo + pltpu.roll(hi, nrow - 1, 0)
    act = jax.nn.gelu(pre, approximate=True)
    out = _dot(act.astype(BF16), w2_ref[...])
    o_ref[0] = out.astype(BF16)
    ot_ref[0] = out.T.astype(BF16)


def _compress(kvc3, pos_lo, pos_hi, wlo, whi, w2bd):
    bsz, nrow, width = kvc3.shape
    const = lambda b: (0, 0)
    return pl.pallas_call(
        _compress_kernel,
        out_shape=(jax.ShapeDtypeStruct((bsz, nrow, 256), BF16), jax.ShapeDtypeStruct((bsz, 256, nrow), BF16)),
        grid=(bsz,),
        in_specs=[pl.BlockSpec((1, nrow, width), lambda b: (b, 0, 0)),
                  pl.BlockSpec((1, width), const), pl.BlockSpec((1, width), const),
                  pl.BlockSpec((width, 256), const), pl.BlockSpec((width, 256), const),
                  pl.BlockSpec((256, 256), const)],
        out_specs=(pl.BlockSpec((1, nrow, 256), lambda b: (b, 0, 0)),
                   pl.BlockSpec((1, 256, nrow), lambda b: (b, 0, 0))),
        compiler_params=_cparams("parallel"),
        name="compress_mlp",
    )(kvc3, pos_lo, pos_hi, wlo, whi, w2bd)


def _stack_heads(q):
    return jnp.concatenate([q[:, r * LANE:(r + 1) * LANE] for r in range(Q_PER_KV)], axis=0)


def _gated_heads(o_t, gl_t, branch):
    tq = gl_t.shape[1]
    gates = jnp.concatenate([jax.nn.sigmoid(gl_t[branch * Q_PER_KV + r:branch * Q_PER_KV + r + 1, :])
                             for r in range(Q_PER_KV)], axis=1)
    og = (o_t * gates).astype(BF16)
    ii = lax.broadcasted_iota(jnp.int32, (tq, tq), 0)
    kk = lax.broadcasted_iota(jnp.int32, (tq, tq), 1)
    eye = jnp.where(ii == kk, 1.0, 0.0).astype(BF16)
    halves = []
    for pair in range(Q_PER_KV // 2):
        two_heads = jnp.concatenate([og[:, (2 * pair) * tq:(2 * pair + 1) * tq],
                                     og[:, (2 * pair + 1) * tq:(2 * pair + 2) * tq]], axis=0)
        halves.append(_dot_nt(eye, two_heads))
    return jnp.concatenate(halves, axis=1).astype(BF16)


def _cmp_attn_kernel(q_ref, kv_ref, kvt_ref, tab_ref, gl_ref, o_ref, sel_ref, *, n_blocks, n_sel):
    qt = pl.program_id(1)
    tq = ATT_TQ
    jj = lax.broadcasted_iota(jnp.int32, (n_blocks, LANE), 0)
    nn = lax.broadcasted_iota(jnp.int32, (n_blocks, LANE), 1)
    overlap_t = jnp.where((CMP_STRIDE * nn < L_SEL * jj + L_SEL) & (CMP_STRIDE * nn + L_CMP > L_SEL * jj),
                          1.0, 0.0).astype(F32)
    jb = lax.broadcasted_iota(jnp.int32, (n_blocks, tq), 0)
    tb = (qt * tq + lax.broadcasted_iota(jnp.int32, (n_blocks, tq), 1)) // L_SEL
    for g in range(N_KV):
        qs = _stack_heads(q_ref[0, :, g * Q_PER_KV * LANE:(g + 1) * Q_PER_KV * LANE])
        bias = tab_ref[g, 0]
        s = _dot_nt(kv_ref[0, :, g * LANE:(g + 1) * LANE], qs) + bias
        valid = bias > 0.5 * NEG_INF
        m = jnp.max(s, axis=0, keepdims=True)
        p = jnp.where(valid, jnp.exp(s - m), 0.0)
        l = jnp.sum(p, axis=0, keepdims=True)
        p = p / jnp.where(l > 0.0, l, 1.0)
        o_t = _dot(kvt_ref[0, g * LANE + HEAD_DIM:(g + 1) * LANE, :], p.astype(BF16))
        o_ref[0, :, g * 256:(g + 1) * 256] = _gated_heads(o_t, gl_ref[g * LANE:(g + 1) * LANE, :], 0)

        psum = p[:, 0:tq] + p[:, tq:2 * tq] + p[:, 2 * tq:3 * tq] + p[:, 3 * tq:4 * tq]
        imp_t = jnp.dot(overlap_t, psum, precision=HIGHEST, preferred_element_type=F32)
        score = jnp.where(jb > tb, -jnp.inf,
                          jnp.where((jb == 0) | (jb == tb) | (jb == tb - 1), jnp.inf, imp_t))
        rank = jnp.zeros((n_blocks, tq), F32)
        for jp in range(n_blocks):
            row = score[jp:jp + 1, :]
            earlier = jnp.where(jb > jp, 1.0, 0.0)
            rank = rank + jnp.where(row > score, 1.0, 0.0) + jnp.where(row == score, earlier, 0.0)
        sel_ref[0, g] = jnp.where(rank < float(n_sel), 1.0, 0.0).astype(BF16)


def _cmp_attn(q3, kvcmp, kvcmp_t, tab, gl3, n_blocks, n_sel):
    bsz, seq, _ = q3.shape
    nqt = seq // ATT_TQ
    n_cmp = kvcmp.shape[1]
    kern = functools.partial(_cmp_attn_kernel, n_blocks=n_blocks, n_sel=n_sel)
    return pl.pallas_call(
        kern,
        out_shape=(jax.ShapeDtypeStruct((bsz, seq, 512), BF16),
                   jax.ShapeDtypeStruct((bsz, N_KV, n_blocks, seq), BF16)),
        grid=(bsz, nqt),
        in_specs=[pl.BlockSpec((1, ATT_TQ, Q_COLS), lambda b, t: (b, t, 0)),
                  pl.BlockSpec((1, n_cmp, N_KV * LANE), lambda b, t: (b, 0, 0)),
                  pl.BlockSpec((1, N_KV * LANE, n_cmp), lambda b, t: (b, 0, 0)),
                  pl.BlockSpec((N_KV, 1, n_cmp, ATT_ROWS), lambda b, t: (0, t, 0, 0)),
                  pl.BlockSpec((GL_ROWS, ATT_TQ), lambda b, t: (0, b * nqt + t))],
        out_specs=(pl.BlockSpec((1, ATT_TQ, 512), lambda b, t: (b, t, 0)),
                   pl.BlockSpec((1, N_KV, n_blocks, ATT_TQ), lambda b, t: (b, 0, 0, t))),
        compiler_params=_cparams("parallel", "parallel"),
        name="cmp_attn",
    )(q3, kvcmp, kvcmp_t, tab, gl3)


def _band_attn_kernel(*refs, n_delta, span, group, branch, use_sel):
    if use_sel:
        q_ref, kv_ref, kvt_ref, tab_ref, gl_ref, sel_ref, o_ref, s_sc = refs
    else:
        q_ref, kv_ref, kvt_ref, tab_ref, gl_ref, o_ref, s_sc = refs
    qt = pl.program_id(1)
    tq = ATT_TQ
    hw = Q_PER_KV * LANE
    qs = [_stack_heads(q_ref[0, :, g * hw:(g + 1) * hw]) for g in range(N_KV)]
    n_chunks = kv_ref.shape[1] // tq
    c_base = qt - span if span is not None else 0
    if use_sel:
        sel_t = [sel_ref[0, g] for g in range(N_KV)]
        nb = sel_t[0].shape[0]
        kb = lax.broadcasted_iota(jnp.int32, (tq, nb), 0) // L_SEL
        jj = lax.broadcasted_iota(jnp.int32, (tq, nb), 1)

    def chunk_ids(gi, u):
        c = c_base + gi * group + u
        inside = (c >= 0) & (c <= qt)
        dd = jnp.where(inside, jnp.minimum(qt - c, n_delta - 1), n_delta)
        return jnp.clip(c, 0, n_chunks - 1), dd

    def rows(gi, u):
        return pl.ds(pl.multiple_of((gi * group + u) * tq, tq), tq)

    def scores(gi, ms):
        ms = list(ms)
        for u in range(group):
            cc, dd = chunk_ids(gi, u)
            kv = kv_ref[0, pl.ds(pl.multiple_of(cc * tq, tq), tq), :]
            if use_sel:
                expand_t = jnp.where(jj == cc * (tq // L_SEL) + kb, 1.0, 0.0).astype(BF16)
            for g in range(N_KV):
                s = _dot_nt(kv[:, g * LANE:(g + 1) * LANE], qs[g]) + tab_ref[g, dd]
                if use_sel:
                    allowed = _dot(expand_t, sel_t[g])
                    madd = (allowed - 1.0) * (-NEG_INF)
                    s = s + jnp.concatenate([madd] * Q_PER_KV, axis=1)
                s_sc[g, rows(gi, u), :] = s
                ms[g] = jnp.maximum(ms[g], jnp.max(s, axis=0, keepdims=True))
        return tuple(ms)

    m0 = tuple(jnp.full((1, ATT_ROWS), NEG_INF, F32) for _ in range(N_KV))
    if span is not None:
        n_groups = 1
        m = scores(0, m0)
    else:
        n_groups = qt // group + 1
        m = lax.fori_loop(0, n_groups, scores, m0)

    def values(gi, carry):
        carry = [list(c) for c in carry]
        for u in range(group):
            cc, _ = chunk_ids(gi, u)
            for g in range(N_KV):
                l, acc = carry[g]
                p = jnp.exp(s_sc[g, rows(gi, u), :] - m[g])
                l = l + jnp.sum(p, axis=0, keepdims=True)
                acc = acc + _dot(kvt_ref[0, g, cc, HEAD_DIM:, :], p.astype(BF16))
                carry[g] = [l, acc]
        return tuple(tuple(c) for c in carry)

    init = tuple((jnp.zeros((1, ATT_ROWS), F32), jnp.zeros((HEAD_DIM, ATT_ROWS), F32)) for _ in range(N_KV))
    if span is not None:
        out = values(0, init)
    else:
        out = lax.fori_loop(0, n_groups, values, init)
    for g in range(N_KV):
        l, acc = out[g]
        o_ref[0, :, g * 256:(g + 1) * 256] = _gated_heads(acc / l, gl_ref[g * LANE:(g + 1) * LANE, :], branch)


def _band_attn(q3, kvsw3, kvsw_t, kv_block0, tab, gl3, sel_t, span, branch):
    bsz, seq, _ = q3.shape
    nqt = seq // ATT_TQ
    use_sel = sel_t is not None
    group = span + 1 if span is not None else 4
    slots = group if span is not None else -(-nqt // group) * group
    kern = functools.partial(_band_attn_kernel, n_delta=tab.shape[1] - 1, span=span, group=group, branch=branch,
                             use_sel=use_sel)
    pair = kv_block0 // N_KV
    in_specs = [pl.BlockSpec((1, ATT_TQ, Q_COLS), lambda b, t: (b, t, 0)),
                pl.BlockSpec((1, seq, N_KV * LANE), lambda b, t: (b, 0, pair)),
                pl.BlockSpec((1, N_KV) + kvsw_t.shape[2:], lambda b, t: (b, pair, 0, 0, 0)),
                pl.BlockSpec(tab.shape, lambda b, t: (0, 0, 0, 0)),
                pl.BlockSpec((GL_ROWS, ATT_TQ), lambda b, t: (0, b * nqt + t))]
    args = [q3, kvsw3, kvsw_t, tab, gl3]
    if use_sel:
        in_specs.append(pl.BlockSpec((1, N_KV, sel_t.shape[2], ATT_TQ), lambda b, t: (b, 0, 0, t)))
        args.append(sel_t)
    return pl.pallas_call(
        kern,
        out_shape=jax.ShapeDtypeStruct((bsz, seq, 512), BF16),
        grid=(bsz, nqt),
        in_specs=in_specs,
        out_specs=pl.BlockSpec((1, ATT_TQ, 512), lambda b, t: (b, t, 0)),
        scratch_shapes=[pltpu.VMEM((N_KV, slots * ATT_TQ, ATT_ROWS), F32)],
        compiler_params=_cparams("parallel", "parallel"),
        name="sel_attn" if use_sel else "win_attn",
    )(*args)


def _conv_kernel(cur_ref, prev_ref, w_ref, cb_ref, g_ref, b_ref, o_ref, ybuf, cbuf):
    i = pl.program_id(1)
    dc = o_ref.shape[-1]
    cur = cur_ref[0]
    prev = prev_ref[0]
    ybuf[CONV_HALO:, :] = cur[:, :dc] * jax.nn.sigmoid(cur[:, dc:])
    halo = prev[:, :dc] * jax.nn.sigmoid(prev[:, dc:])
    ybuf[0:CONV_HALO, :] = jnp.where(i > 0, halo, 0.0)
    off = CONV_HALO - (CONV_WIDTH - 1)
    for lc in range(dc // LANE):
        cols = slice(lc * LANE, (lc + 1) * LANE)
        for rc in range(CONV_TQ // CONV_RC):
            acc = jnp.zeros((CONV_RC, LANE), F32)
            for k in range(CONV_WIDTH):
                r0 = rc * CONV_RC + off + k
                acc = acc + w_ref[k:k + 1, cols] * ybuf[r0:r0 + CONV_RC, cols]
            cbuf[rc * CONV_RC:(rc + 1) * CONV_RC, cols] = acc
    z = cbuf[...] + cb_ref[...]
    mu = jnp.mean(z, axis=-1, keepdims=True)
    zc = z - mu
    var = jnp.mean(zc * zc, axis=-1, keepdims=True)
    y = zc * lax.rsqrt(var + LN_EPS) * g_ref[...] + b_ref[...]
    o_ref[0] = jax.nn.silu(y).astype(BF16)


def _conv_branch(cv3, w, cb, g, b):
    bsz, seq, two_dc = cv3.shape
    dc = two_dc // 2
    per = CONV_TQ // CONV_HALO
    const = lambda bb, i: (0, 0)
    return pl.pallas_call(
        _conv_kernel,
        out_shape=jax.ShapeDtypeStruct((bsz, seq, dc), BF16),
        grid=(bsz, seq // CONV_TQ),
        in_specs=[pl.BlockSpec((1, CONV_TQ, two_dc), lambda bb, i: (bb, i, 0)),
                  pl.BlockSpec((1, CONV_HALO, two_dc), lambda bb, i: (bb, jnp.maximum(i * per - 1, 0), 0)),
                  pl.BlockSpec(w.shape, const), pl.BlockSpec((1, dc), const),
                  pl.BlockSpec((1, dc), const), pl.BlockSpec((1, dc), const)],
        out_specs=pl.BlockSpec((1, CONV_TQ, dc), lambda bb, i: (bb, i, 0)),
        scratch_shapes=[pltpu.VMEM((CONV_HALO + CONV_TQ, dc), F32), pltpu.VMEM((CONV_TQ, dc), F32)],
        compiler_params=_cparams("parallel", "parallel"),
        name="conv_branch",
    )(cv3, cv3, w, cb, g, b)


def _outproj_kernel(x_ref, gate_ref, g_ref, b_ref, oc_ref, os_ref, ow_ref, ocv_ref, w_ref, o_ref, *, alpha):
    nsa = (oc_ref[...].astype(F32) + os_ref[...].astype(F32) + ow_ref[...].astype(F32)).astype(BF16)
    half = nsa.shape[1]
    y = _dot(nsa, w_ref[0:half, :]) + _dot(ocv_ref[...], w_ref[half:, :])
    o_ref[...] = _res_ln(x_ref[...], gate_ref[0], y, g_ref[...], b_ref[...], alpha)


def _outproj(x2, gate, lng, lnb, o_cmp, o_sel, o_win, o_conv, w, seq, tm, alpha):
    n, d = x2.shape
    per_b = seq // tm
    row = lambda i: (i, 0)
    const = lambda i: (0, 0)
    half = o_cmp.shape[1]
    return pl.pallas_call(
        functools.partial(_outproj_kernel, alpha=alpha),
        out_shape=jax.ShapeDtypeStruct((n, d), F32),
        grid=(n // tm,),
        in_specs=[pl.BlockSpec((tm, d), row), pl.BlockSpec((1, 1, d), lambda i: (i // per_b, 0, 0)),
                  pl.BlockSpec((1, d), const), pl.BlockSpec((1, d), const),
                  pl.BlockSpec((tm, half), row), pl.BlockSpec((tm, half), row), pl.BlockSpec((tm, half), row),
                  pl.BlockSpec((tm, o_conv.shape[1]), row), pl.BlockSpec(w.shape, const)],
        out_specs=pl.BlockSpec((tm, d), row),
        compiler_params=_cparams("parallel"),
        name="out_proj_ln",
    )(x2, gate, lng, lnb, o_cmp, o_sel, o_win, o_conv, w)


def _ffn1_kernel(x_ref, sc_ref, sh_ref, wg_ref, wu_ref, o_ref, *, tn):
    h = (x_ref[...] * (1.0 + sc_ref[0]) + sh_ref[0]).astype(BF16)
    for j in range(o_ref.shape[1] // tn):
        cols = slice(j * tn, (j + 1) * tn)
        gate = _dot(h, wg_ref[:, cols])
        up = _dot(h, wu_ref[:, cols])
        o_ref[:, cols] = (jax.nn.silu(gate) * up).astype(BF16)


def _ffn1(x2, scale, shift, wg, wu, seq, tm):
    n, d = x2.shape
    dff = wg.shape[1]
    per_b = seq // tm
    row = lambda i: (i, 0)
    mod = lambda i: (i // per_b, 0, 0)
    const = lambda i: (0, 0)
    return pl.pallas_call(
        functools.partial(_ffn1_kernel, tn=256),
        out_shape=jax.ShapeDtypeStruct((n, dff), BF16),
        grid=(n // tm,),
        in_specs=[pl.BlockSpec((tm, d), row), pl.BlockSpec((1, 1, d), mod), pl.BlockSpec((1, 1, d), mod),
                  pl.BlockSpec((d, dff), const), pl.BlockSpec((d, dff), const)],
        out_specs=pl.BlockSpec((tm, dff), row),
        compiler_params=_cparams("parallel"),
        name="ffn_up",
    )(x2, scale, shift, wg, wu)


def _ffn2_kernel(x_ref, gate_ref, g_ref, b_ref, a_ref, w_ref, o_ref, *, alpha):
    y = _dot(a_ref[...], w_ref[...])
    o_ref[...] = _res_ln(x_ref[...], gate_ref[0], y, g_ref[...], b_ref[...], alpha)


def _ffn2(x2, gate, lng, lnb, act, w, seq, tm, alpha):
    n, d = x2.shape
    per_b = seq // tm
    row = lambda i: (i, 0)
    const = lambda i: (0, 0)
    return pl.pallas_call(
        functools.partial(_ffn2_kernel, alpha=alpha),
        out_shape=jax.ShapeDtypeStruct((n, d), F32),
        grid=(n // tm,),
        in_specs=[pl.BlockSpec((tm, d), row), pl.BlockSpec((1, 1, d), lambda i: (i // per_b, 0, 0)),
                  pl.BlockSpec((1, d), const), pl.BlockSpec((1, d), const),
                  pl.BlockSpec((tm, act.shape[1]), row), pl.BlockSpec(w.shape, const)],
        out_specs=pl.BlockSpec((tm, d), row),
        compiler_params=_cparams("parallel"),
        name="ffn_down_ln",
    )(x2, gate, lng, lnb, act, w)


MOE_TM = 1024
MOE_FC = 512
GATHER_ROWS = 1024


def _router_kernel(x_ref, sc_ref, sh_ref, rw_ref, h_ref, ti_ref, tw_ref):
    hf = x_ref[...] * (1.0 + sc_ref[0]) + sh_ref[0]
    h_ref[...] = hf
    lane = lax.broadcasted_iota(jnp.int32, (hf.shape[0], LANE), 1)
    logits = jnp.dot(hf, rw_ref[...], precision=HIGHEST, preferred_element_type=F32)
    logits = jnp.where(lane < N_EXPERTS, logits, -jnp.inf)
    m1 = jnp.max(logits, axis=-1, keepdims=True)
    i1 = jnp.min(jnp.where(logits == m1, lane, LANE), axis=-1, keepdims=True)
    rest = jnp.where(lane == i1, -jnp.inf, logits)
    m2 = jnp.max(rest, axis=-1, keepdims=True)
    i2 = jnp.min(jnp.where(rest == m2, lane, LANE), axis=-1, keepdims=True)
    e2 = jnp.exp(m2 - m1)
    ti_ref[...] = jnp.where(lane == 0, i1, jnp.where(lane == 1, i2, 0))
    tw_ref[...] = jnp.where(lane == 0, 1.0 / (1.0 + e2), jnp.where(lane == 1, e2 / (1.0 + e2), 0.0))


def _router(x2, scale, shift, router_p, seq, tm):
    n, d = x2.shape
    per_b = seq // tm
    row = lambda i: (i, 0)
    mod = lambda i: (i // per_b, 0, 0)
    return pl.pallas_call(
        _router_kernel,
        out_shape=(jax.ShapeDtypeStruct((n, d), F32), jax.ShapeDtypeStruct((n, LANE), jnp.int32),
                   jax.ShapeDtypeStruct((n, LANE), F32)),
        grid=(n // tm,),
        in_specs=[pl.BlockSpec((tm, d), row), pl.BlockSpec((1, 1, d), mod), pl.BlockSpec((1, 1, d), mod),
                  pl.BlockSpec(router_p.shape, lambda i: (0, 0))],
        out_specs=(pl.BlockSpec((tm, d), row), pl.BlockSpec((tm, LANE), row), pl.BlockSpec((tm, LANE), row)),
        compiler_params=_cparams("parallel"),
        name="moe_router",
    )(x2, scale, shift, router_p)


def _gather_rows_kernel(idx_ref, src_ref, o_ref, sem):
    rows = o_ref.shape[0]
    base = pl.program_id(0) * rows

    def issue(r, carry):
        pltpu.make_async_copy(src_ref.at[pl.ds(idx_ref[base + r], 1)], o_ref.at[pl.ds(r, 1)], sem).start()
        return carry

    lax.fori_loop(0, rows, issue, 0, unroll=8)
    pltpu.make_async_copy(src_ref.at[pl.ds(0, rows)], o_ref, sem).wait()


def _gather_rows(idx, src):
    n_out = idx.shape[0]
    d = src.shape[1]
    return pl.pallas_call(
        _gather_rows_kernel,
        out_shape=jax.ShapeDtypeStruct((n_out, d), src.dtype),
        grid_spec=pltpu.PrefetchScalarGridSpec(
            num_scalar_prefetch=1,
            grid=(n_out // GATHER_ROWS,),
            in_specs=[pl.BlockSpec(memory_space=pl.ANY)],
            out_specs=pl.BlockSpec((GATHER_ROWS, d), lambda i, idx_ref: (i, 0)),
            scratch_shapes=[pltpu.SemaphoreType.DMA]),
        compiler_params=_cparams("arbitrary"),
        name="row_gather",
    )(idx, src)


def _expert_up_kernel(te_ref, nu_ref, x_ref, wg_ref, wu_ref, o_ref, xb_sc):
    i = pl.program_id(0)
    f = pl.program_id(1)

    @pl.when(i < nu_ref[0])
    def _compute():
        @pl.when(f == 0)
        def _cast():
            xb_sc[...] = x_ref[...].astype(BF16)
        h = xb_sc[...]
        gate = _dot(h, wg_ref[0, 0])
        up = _dot(h, wu_ref[0, 0])
        o_ref[...] = (jax.nn.silu(gate) * up).astype(BF16)

    @pl.when(i >= nu_ref[0])
    def _unused_tile():
        o_ref[...] = jnp.zeros_like(o_ref)


def _expert_up(tile_expert, n_used, xg, w1b, layer):
    p_rows, d = xg.shape
    nf = w1b.shape[3] // (2 * MOE_FC)
    return pl.pallas_call(
        _expert_up_kernel,
        out_shape=jax.ShapeDtypeStruct((p_rows, nf * MOE_FC), BF16),
        grid_spec=pltpu.PrefetchScalarGridSpec(
            num_scalar_prefetch=2,
            grid=(p_rows // MOE_TM, nf),
            in_specs=[pl.BlockSpec((MOE_TM, d), lambda i, f, te, nu: (i, 0)),
                      pl.BlockSpec((1, 1, d, MOE_FC), lambda i, f, te, nu: (layer, te[i], 0, f)),
                      pl.BlockSpec((1, 1, d, MOE_FC), lambda i, f, te, nu: (layer, te[i], 0, nf + f))],
            out_specs=pl.BlockSpec((MOE_TM, MOE_FC), lambda i, f, te, nu: (i, f)),
            scratch_shapes=[pltpu.VMEM((MOE_TM, d), BF16)]),
        compiler_params=_cparams("parallel", "arbitrary"),
        name="moe_up",
    )(tile_expert, n_used, xg, w1b, w1b)


def _expert_down_kernel(te_ref, nu_ref, a_ref, w_ref, o_ref):
    i = pl.program_id(0)

    @pl.when(i < nu_ref[0])
    def _compute():
        o_ref[...] = _dot(a_ref[...], w_ref[0, 0])

    @pl.when(i >= nu_ref[0])
    def _unused_tile():
        o_ref[...] = jnp.zeros_like(o_ref)


def _expert_down(tile_expert, n_used, act, w2b, layer):
    p_rows, dffe = act.shape
    d = w2b.shape[3]
    return pl.pallas_call(
        _expert_down_kernel,
        out_shape=jax.ShapeDtypeStruct((p_rows, d), F32),
        grid_spec=pltpu.PrefetchScalarGridSpec(
            num_scalar_prefetch=2,
            grid=(p_rows // MOE_TM,),
            in_specs=[pl.BlockSpec((MOE_TM, dffe), lambda i, te, nu: (i, 0)),
                      pl.BlockSpec((1, 1, dffe, d), lambda i, te, nu: (layer, te[i], 0, 0))],
            out_specs=pl.BlockSpec((MOE_TM, d), lambda i, te, nu: (i, 0))),
        compiler_params=_cparams("parallel"),
        name="moe_down",
    )(tile_expert, n_used, act, w2b)


def _moe_combine_kernel(x_ref, gate_ref, g_ref, b_ref, tw_ref, y1_ref, y2_ref, o_ref, *, alpha):
    tw = tw_ref[...]
    y = tw[:, 0:1] * y1_ref[...] + tw[:, 1:2] * y2_ref[...]
    o_ref[...] = _res_ln(x_ref[...], gate_ref[0], y, g_ref[...], b_ref[...], alpha)


def _moe_combine(x2, gate, lng, lnb, tw, yg, seq, tm, alpha):
    n, d = x2.shape
    per_b = seq // tm
    nt = n // tm
    row = lambda i: (i, 0)
    const = lambda i: (0, 0)
    return pl.pallas_call(
        functools.partial(_moe_combine_kernel, alpha=alpha),
        out_shape=jax.ShapeDtypeStruct((n, d), F32),
        grid=(nt,),
        in_specs=[pl.BlockSpec((tm, d), row), pl.BlockSpec((1, 1, d), lambda i: (i // per_b, 0, 0)),
                  pl.BlockSpec((1, d), const), pl.BlockSpec((1, d), const), pl.BlockSpec((tm, LANE), row),
                  pl.BlockSpec((tm, d), row), pl.BlockSpec((tm, d), lambda i: (nt + i, 0))],
        out_specs=pl.BlockSpec((tm, d), row),
        compiler_params=_cparams("parallel"),
        name="moe_combine_ln",
    )(x2, gate, lng, lnb, tw, yg, yg)


def _route_tables(ti, n_rows):
    n = ti.shape[0]
    e_flat = jnp.concatenate([ti[:, 0], ti[:, 1]])
    onehot = (e_flat[:, None] == jnp.arange(N_EXPERTS, dtype=jnp.int32)[None, :]).astype(jnp.int32)
    cum = jnp.cumsum(onehot, axis=0)
    counts = cum[-1]
    padded = (counts + MOE_TM - 1) // MOE_TM * MOE_TM
    ends = jnp.cumsum(padded)
    pos = jnp.sum(onehot * (cum - 1 + (ends - padded)[None, :]), axis=1)
    tok = jnp.tile(jnp.arange(n, dtype=jnp.int32), 2)
    src = jnp.zeros((n_rows,), jnp.int32).at[pos].set(tok)
    tile_start = jnp.arange(n_rows // MOE_TM, dtype=jnp.int32) * MOE_TM
    tile_expert = jnp.minimum(jnp.sum((tile_start[:, None] >= ends[None, :]).astype(jnp.int32), axis=1),
                              N_EXPERTS - 1)
    return pos.astype(jnp.int32), src, tile_expert, (ends[-1:] // MOE_TM).astype(jnp.int32)


def _moe(x2, scale, shift, gate, lng, lnb, router_p, w1b, w2b, layer, seq, tm, alpha):
    n, d = x2.shape
    n_rows = 2 * n + N_EXPERTS * MOE_TM
    assert n_rows % GATHER_ROWS == 0 and (2 * n) % GATHER_ROWS == 0
    h, ti, tw = _router(x2, scale, shift, router_p, seq, tm)
    pos, src, tile_expert, n_used = _route_tables(ti, n_rows)
    xg = _gather_rows(src, h)
    act = _expert_up(tile_expert, n_used, xg, w1b, layer)
    y = _expert_down(tile_expert, n_used, act, w2b, layer)
    yg = _gather_rows(pos, y)
    return _moe_combine(x2, gate, lng, lnb, tw, yg, seq, tm, alpha)


def _rel_bucket_np(dist):
    n = np.maximum(dist, 0)
    max_exact = NUM_BUCKETS // 2
    nf = np.maximum(n, 1).astype(np.float32)
    large = max_exact + (np.log(nf / np.float32(max_exact)) / np.float32(math.log(MAX_DISTANCE / max_exact))
                         * np.float32(NUM_BUCKETS - max_exact)).astype(np.int32)
    large = np.minimum(large, NUM_BUCKETS - 1)
    return np.where(n < max_exact, n, large).astype(np.int32)


def _bias_lookup(rel_bias, dist, valid):
    onehot = _rel_bucket_np(dist)[..., None] == np.arange(NUM_BUCKETS)
    vals = jnp.einsum('...k,kh->...h', jnp.asarray(onehot).astype(F32), rel_bias, precision=HIGHEST)
    return jnp.where(jnp.asarray(valid)[..., None], vals, NEG_INF)


def _key_major(vals, add_masked_tile=False):
    n_tiles, tq, tk, _ = vals.shape
    if add_masked_tile:
        vals = jnp.concatenate([vals, jnp.full((1,) + vals.shape[1:], NEG_INF, vals.dtype)], axis=0)
        n_tiles += 1
    vals = vals.reshape(n_tiles, tq, tk, N_KV, Q_PER_KV).transpose(3, 0, 2, 4, 1)
    return vals.reshape(N_KV, n_tiles, tk, Q_PER_KV * tq).astype(F32)


def _band_dist(n_delta):
    i = np.arange(ATT_TQ)[None, :, None]
    k = np.arange(ATT_TQ)[None, None, :]
    return ATT_TQ * np.arange(n_delta)[:, None, None] + i - k


def _cmp_bias_table(rel_bias, n_qt, n_cmp_pad):
    per_tile = ATT_TQ // CMP_STRIDE
    shift = per_tile * (n_qt - 1)
    rel_n = np.arange(n_cmp_pad + shift) - shift
    dist = np.arange(ATT_TQ)[:, None] - (CMP_STRIDE * rel_n[None, :] + L_CMP - 1)
    base = _bias_lookup(rel_bias, dist, dist >= 0)
    tiles = [base[:, shift - per_tile * qt: shift - per_tile * qt + n_cmp_pad] for qt in range(n_qt)]
    return _key_major(jnp.stack(tiles))


def _inproj_weight(w_in):
    depth, d, _ = w_in.shape
    d_nsa = N_HEADS * HEAD_DIM
    d_kv = N_KV * HEAD_DIM
    base_ks = d_nsa + 2 * d_kv
    base_gl = d_nsa + 6 * d_kv
    base_cv = base_gl + N_HEADS * N_BRANCH
    wq = jnp.pad(w_in[:, :, :d_nsa].reshape(depth, d, N_HEADS, HEAD_DIM),
                 ((0, 0), (0, 0), (0, 0), (0, LANE - HEAD_DIM))).reshape(depth, d, Q_COLS)
    parts = [wq, w_in[:, :, d_nsa:base_ks]]
    for pair in range(2):
        kbase = base_ks + pair * 2 * d_kv
        for g in range(N_KV):
            parts.append(w_in[:, :, kbase + g * HEAD_DIM: kbase + (g + 1) * HEAD_DIM])
            parts.append(w_in[:, :, kbase + d_kv + g * HEAD_DIM: kbase + d_kv + (g + 1) * HEAD_DIM])
    parts.append(w_in[:, :, base_cv:])
    gl = w_in[:, :, base_gl:base_cv].reshape(depth, d, N_KV, Q_PER_KV, N_BRANCH).transpose(0, 2, 4, 3, 1)
    gl = jnp.pad(gl.reshape(depth, N_KV, N_BRANCH * Q_PER_KV, d),
                 ((0, 0), (0, 0), (0, LANE - N_BRANCH * Q_PER_KV), (0, 0)))
    return jnp.concatenate(parts, axis=-1).astype(BF16), gl.reshape(depth, GL_ROWS, d).astype(BF16)


def kernel(x, c, w_in, cmp_pos, cmp_w1, cmp_w2, conv_w, conv_b, conv_ln_g, conv_ln_b, w_out, rel_bias,
           ada_w, ada_b, ln_g, ln_b, ffn_w1, ffn_w2, router_w, moe_w1, moe_w2):
    bsz, seq, d = x.shape
    depth = w_in.shape[0]
    n = bsz * seq
    alpha = (2 * depth) ** 0.25
    n_blocks = seq // L_SEL
    n_sel = min(N_SEL, n_blocks)
    n_cmp = (seq - L_CMP) // CMP_STRIDE + 1
    nrow = seq // CMP_STRIDE
    assert d == 1024 and seq % 512 == 0 and nrow == LANE and n_cmp == nrow - 1
    tm = 512

    n_qt = seq // ATT_TQ
    d_win = _band_dist(WINDOW // ATT_TQ + 1)
    win_tab = _key_major(_bias_lookup(rel_bias, d_win, (d_win >= 0) & (d_win < WINDOW)), add_masked_tile=True)
    d_sel = _band_dist(3)
    assert _rel_bucket_np(d_sel[2]).min() == NUM_BUCKETS - 1
    sel_tab = _key_major(_bias_lookup(rel_bias, d_sel, d_sel >= 0), add_masked_tile=True)
    cmp_tab = _cmp_bias_table(rel_bias, n_qt, nrow)

    w_in_p, w_gl_t = _inproj_weight(w_in)
    eye2 = jnp.eye(2, dtype=F32)
    w1r = cmp_w1.reshape(depth, 2, L_CMP, HEAD_DIM, HEAD_DIM)
    w1x = jnp.einsum('zslde,st,gh->zlsgdhte', w1r, eye2, eye2).reshape(depth, L_CMP, 4 * HEAD_DIM, 4 * HEAD_DIM)
    half = L_CMP // 2
    w_lo = w1x[:, :half].reshape(depth, half * 4 * HEAD_DIM, 4 * HEAD_DIM).astype(BF16)
    w_hi = w1x[:, half:].reshape(depth, half * 4 * HEAD_DIM, 4 * HEAD_DIM).astype(BF16)
    w2bd = jnp.einsum('zsde,st,gh->zgsdhte', cmp_w2, eye2, eye2).reshape(depth, 4 * HEAD_DIM, 4 * HEAD_DIM).astype(BF16)
    posx = jnp.broadcast_to(cmp_pos[:, :, :, None, :], (depth, 2, L_CMP, N_KV, HEAD_DIM)).transpose(0, 2, 1, 3, 4)
    pos_lo = posx[:, :half].reshape(depth, 1, half * 4 * HEAD_DIM)
    pos_hi = posx[:, half:].reshape(depth, 1, half * 4 * HEAD_DIM)
    conv_wp = jnp.pad(conv_w, ((0, 0), (0, 1), (0, 0)))
    w_out_b = w_out.astype(BF16)
    dff = ffn_w2.shape[1]
    ffn_wg = ffn_w1[:, :, :dff].astype(BF16)
    ffn_wu = ffn_w1[:, :, dff:].astype(BF16)
    ffn_w2b = ffn_w2.astype(BF16)
    router_p = jnp.pad(router_w, ((0, 0), (0, 0), (0, LANE - N_EXPERTS)))
    moe_w1b = moe_w1.astype(BF16)
    moe_w2b = moe_w2.astype(BF16)

    mod = _ada(c, ada_w, ada_b).reshape(depth, 6, bsz, 1, d)
    x2 = x.reshape(n, d)
    for l in range(depth):
        shift1, scale1, gate1, shift2, scale2, gate2 = [mod[l, k] for k in range(6)]
        q, kvc, kvsw, gl3, cv = _inproj(x2, scale1, shift1, w_in_p[l], w_gl_t[l], seq, tm)
        q3 = q.reshape(bsz, seq, Q_COLS)
        kvsw3 = kvsw.reshape(bsz, seq, KVSW_COLS)
        kvsw_t = kvsw3.reshape(bsz, n_qt, ATT_TQ, 2 * N_KV, LANE).transpose(0, 3, 1, 4, 2)
        kvcmp, kvcmp_t = _compress(kvc.reshape(bsz, nrow, CMP_STRIDE * 256), pos_lo[l], pos_hi[l], w_lo[l], w_hi[l],
                                   w2bd[l])
        o_cmp, sel_t = _cmp_attn(q3, kvcmp, kvcmp_t, cmp_tab, gl3, n_blocks, n_sel)
        o_sel = _band_attn(q3, kvsw3, kvsw_t, 0, sel_tab, gl3, sel_t, None, 1)
        o_win = _band_attn(q3, kvsw3, kvsw_t, N_KV, win_tab, gl3, None, WINDOW // ATT_TQ, 2)
        o_conv = _conv_branch(cv.reshape(bsz, seq, 1024), conv_wp[l], conv_b[l][None], conv_ln_g[l][None],
                              conv_ln_b[l][None])
        x2 = _outproj(x2, gate1, ln_g[l, 0][None], ln_b[l, 0][None], o_cmp.reshape(n, 512), o_sel.reshape(n, 512),
                      o_win.reshape(n, 512), o_conv.reshape(n, 512), w_out_b[l], seq, tm, alpha)
        if l % 2 == 0:
            act = _ffn1(x2, scale2, shift2, ffn_wg[l // 2], ffn_wu[l // 2], seq, tm)
            x2 = _ffn2(x2, gate2, ln_g[l, 1][None], ln_b[l, 1][None], act, ffn_w2b[l // 2], seq, tm, alpha)
        else:
            x2 = _moe(x2, scale2, shift2, gate2, ln_g[l, 1][None], ln_b[l, 1][None], router_p[l // 2],
                      moe_w1b, moe_w2b, l // 2, seq, tm, alpha)
    return x2.reshape(bsz, seq, d)
```

```python
import functools
import math

import numpy as np
import jax
import jax.numpy as jnp
from jax import lax
from jax.experimental import pallas as pl
from jax.experimental.pallas import tpu as pltpu

F32 = jnp.float32
BF16 = jnp.bfloat16
HIGHEST = lax.Precision.HIGHEST

HEAD_DIM = 64
N_HEADS = 8
N_KV = 2
Q_PER_KV = N_HEADS // N_KV
N_BRANCH = 3
L_CMP = 32
CMP_STRIDE = 16
L_SEL = 64
N_SEL = 8
WINDOW = 512
CONV_WIDTH = 31
NUM_BUCKETS = 32
MAX_DISTANCE = 128
N_EXPERTS = 8
LN_EPS = 1e-5
NEG_INF = -1e30
LOG2E = math.log2(math.e)

LANE = 128
SUBLANE = 8
ATT_TQ = 128
ATT_ROWS = Q_PER_KV * ATT_TQ
CONV_TQ = 256
CONV_HALO = 32
CONV_RC = 64
VMEM_LIMIT_BYTES = 52 * 1024 * 1024


def _cparams(*sem):
    return pltpu.CompilerParams(dimension_semantics=sem, vmem_limit_bytes=VMEM_LIMIT_BYTES)


def _dot(a, b):
    return jnp.dot(a, b, preferred_element_type=F32)


def _dot_nt(a, b, precision=None):
    return lax.dot_general(a, b, (((1,), (1,)), ((), ())), precision=precision,
                           preferred_element_type=F32)


def _res_ln(x, gate, y, g, b, alpha):
    z = alpha * x + (1.0 + gate) * y
    mu = jnp.mean(z, axis=-1, keepdims=True)
    zc = z - mu
    var = jnp.mean(zc * zc, axis=-1, keepdims=True)
    return zc * lax.rsqrt(var + LN_EPS) * g + b


def _ada_kernel(c_ref, w_ref, b_ref, o_ref):
    sc = jax.nn.silu(c_ref[...])
    o_ref[0, 0] = jnp.dot(sc, w_ref[0], precision=HIGHEST, preferred_element_type=F32) + b_ref[0]


def _ada(c, ada_w, ada_b):
    depth, d, _ = ada_w.shape
    bsz = c.shape[0]
    return pl.pallas_call(
        _ada_kernel,
        out_shape=jax.ShapeDtypeStruct((depth, 6, bsz, d), F32),
        grid=(depth, 6),
        in_specs=[pl.BlockSpec((bsz, d), lambda l, k: (0, 0)),
                  pl.BlockSpec((1, d, d), lambda l, k: (l, 0, k)),
                  pl.BlockSpec((1, 1, d), lambda l, k: (l, 0, k))],
        out_specs=pl.BlockSpec((1, 1, bsz, d), lambda l, k: (l, k, 0, 0)),
        compiler_params=_cparams("parallel", "parallel"),
        name="ada_mod",
    )(c, ada_w, ada_b.reshape(depth, 1, 6 * d))


Q_COLS = N_HEADS * LANE
KVC_COLS = 4 * HEAD_DIM
KVSW_COLS = 8 * HEAD_DIM
GL_ROWS = N_KV * LANE


def _inproj_kernel(x_ref, sc_ref, sh_ref, w_ref, wgl_ref, q_ref, kvc_ref, kvsw_ref, glt_ref, cv_ref):
    h = (x_ref[...] * (1.0 + sc_ref[0]) + sh_ref[0]).astype(BF16)
    c0, c1, c2 = Q_COLS, Q_COLS + KVC_COLS, Q_COLS + KVC_COLS + KVSW_COLS
    q_ref[...] = (_dot(h, w_ref[:, 0:c0]) * (HEAD_DIM ** -0.5 * LOG2E)).astype(BF16)
    kvc_ref[...] = _dot(h, w_ref[:, c0:c1])
    kvsw_ref[...] = _dot(h, w_ref[:, c1:c2]).astype(BF16)
    cv_ref[...] = _dot(h, w_ref[:, c2:])
    glt_ref[...] = _dot_nt(wgl_ref[...], h)


def _inproj(x2, scale, shift, w, wgl_t, seq, tm):
    n, d = x2.shape
    per_b = seq // tm
    d_cv = w.shape[1] - (Q_COLS + KVC_COLS + KVSW_COLS)
    row = lambda i: (i, 0)
    mod = lambda i: (i // per_b, 0, 0)
    const = lambda i: (0, 0)
    return pl.pallas_call(
        _inproj_kernel,
        out_shape=(jax.ShapeDtypeStruct((n, Q_COLS), BF16), jax.ShapeDtypeStruct((n, KVC_COLS), F32),
                   jax.ShapeDtypeStruct((n, KVSW_COLS), BF16), jax.ShapeDtypeStruct((GL_ROWS, n), F32),
                   jax.ShapeDtypeStruct((n, d_cv), F32)),
        grid=(n // tm,),
        in_specs=[pl.BlockSpec((tm, d), row), pl.BlockSpec((1, 1, d), mod), pl.BlockSpec((1, 1, d), mod),
                  pl.BlockSpec(w.shape, const), pl.BlockSpec(wgl_t.shape, const)],
        out_specs=(pl.BlockSpec((tm, Q_COLS), row), pl.BlockSpec((tm, KVC_COLS), row),
                   pl.BlockSpec((tm, KVSW_COLS), row), pl.BlockSpec((GL_ROWS, tm), lambda i: (0, i)),
                   pl.BlockSpec((tm, d_cv), row)),
        compiler_params=_cparams("parallel"),
        name="in_proj",
    )(x2, scale, shift, w, wgl_t)


def _compress_kernel(a_ref, plo_ref, phi_ref, wlo_ref, whi_ref, w2_ref, o_ref, ot_ref):
    a = a_ref[0]
    lo = _dot((a + plo_ref[...]).astype(BF16), wlo_ref[...])
    hi = _dot((a + phi_ref[...]).astype(BF16), whi_ref[...])
    nrow = lo.shape[0]
    pre = lo + pltpu.roll(hi, nrow - 1, 0)
    act = jax.nn.gelu(pre, approximate=True)
    out = _dot(act.astype(BF16), w2_ref[...])
    o_ref[0] = out.astype(BF16)
    ot_ref[0] = out.T.astype(BF16)


def _compress(kvc3, pos_lo, pos_hi, wlo, whi, w2bd):
    bsz, nrow, width = kvc3.shape
    const = lambda b: (0, 0)
    return pl.pallas_call(
        _compress_kernel,
        out_shape=(jax.ShapeDtypeStruct((bsz, nrow, 256), BF16), jax.ShapeDtypeStruct((bsz, 256, nrow), BF16)),
        grid=(bsz,),
        in_specs=[pl.BlockSpec((1, nrow, width), lambda b: (b, 0, 0)),
                  pl.BlockSpec((1, width), const), pl.BlockSpec((1, width), const),
                  pl.BlockSpec((width, 256), const), pl.BlockSpec((width, 256), const),
                  pl.BlockSpec((256, 256), const)],
        out_specs=(pl.BlockSpec((1, nrow, 256), lambda b: (b, 0, 0)),
                   pl.BlockSpec((1, 256, nrow), lambda b: (b, 0, 0))),
        compiler_params=_cparams("parallel"),
        name="compress_mlp",
    )(kvc3, pos_lo, pos_hi, wlo, whi, w2bd)


def _stack_heads(q):
    return jnp.concatenate([q[:, r * LANE:(r + 1) * LANE] for r in range(Q_PER_KV)], axis=0)


def _gated_heads(o_t, gl_t, branch):
    tq = gl_t.shape[1]
    gates = jnp.concatenate([jax.nn.sigmoid(gl_t[branch * Q_PER_KV + r:branch * Q_PER_KV + r + 1, :])
                             for r in range(Q_PER_KV)], axis=1)
    og = (o_t * gates).astype(BF16)
    ii = lax.broadcasted_iota(jnp.int32, (tq, tq), 0)
    kk = lax.broadcasted_iota(jnp.int32, (tq, tq), 1)
    eye = jnp.where(ii == kk, 1.0, 0.0).astype(BF16)
    halves = []
    for pair in range(Q_PER_KV // 2):
        two_heads = jnp.concatenate([og[:, (2 * pair) * tq:(2 * pair + 1) * tq],
                                     og[:, (2 * pair + 1) * tq:(2 * pair + 2) * tq]], axis=0)
        halves.append(_dot_nt(eye, two_heads))
    return jnp.concatenate(halves, axis=1).astype(BF16)


def _cmp_attn_kernel(q_ref, kv_ref, kvt_ref, tab_ref, gl_ref, o_ref, sel_ref, *, n_blocks, n_sel):
    qt = pl.program_id(1)
    tq = ATT_TQ
    jj = lax.broadcasted_iota(jnp.int32, (n_blocks, LANE), 0)
    nn = lax.broadcasted_iota(jnp.int32, (n_blocks, LANE), 1)
    overlap_t = jnp.where((CMP_STRIDE * nn < L_SEL * jj + L_SEL) & (CMP_STRIDE * nn + L_CMP > L_SEL * jj),
                          1.0, 0.0).astype(F32)
    jb = lax.broadcasted_iota(jnp.int32, (n_blocks, tq), 0)
    tb = (qt * tq + lax.broadcasted_iota(jnp.int32, (n_blocks, tq), 1)) // L_SEL
    for g in range(N_KV):
        qs = _stack_heads(q_ref[0, :, g * Q_PER_KV * LANE:(g + 1) * Q_PER_KV * LANE])
        bias = tab_ref[g, 0]
        s = _dot_nt(kv_ref[0, :, g * LANE:(g + 1) * LANE], qs) + bias
        valid = bias > 0.5 * NEG_INF
        m = jnp.max(s, axis=0, keepdims=True)
        p = jnp.where(valid, jnp.exp2(s - m), 0.0)
        l = jnp.sum(p, axis=0, keepdims=True)
        p = p / jnp.where(l > 0.0, l, 1.0)
        o_t = _dot(kvt_ref[0, g * LANE + HEAD_DIM:(g + 1) * LANE, :], p.astype(BF16))
        o_ref[0, :, g * 256:(g + 1) * 256] = _gated_heads(o_t, gl_ref[g * LANE:(g + 1) * LANE, :], 0)

        psum = p[:, 0:tq] + p[:, tq:2 * tq] + p[:, 2 * tq:3 * tq] + p[:, 3 * tq:4 * tq]
        imp_t = jnp.dot(overlap_t, psum, precision=HIGHEST, preferred_element_type=F32)
        score = jnp.where(jb > tb, -jnp.inf,
                          jnp.where((jb == 0) | (jb == tb) | (jb == tb - 1), jnp.inf, imp_t))
        rank = jnp.zeros((n_blocks, tq), F32)
        for jp in range(n_blocks):
            row = score[jp:jp + 1, :]
            earlier = jnp.where(jb > jp, 1.0, 0.0)
            rank = rank + jnp.where(row > score, 1.0, 0.0) + jnp.where(row == score, earlier, 0.0)
        sel_ref[0, g] = jnp.where(rank < float(n_sel), 1.0, 0.0).astype(BF16)


def _cmp_attn(q3, kvcmp, kvcmp_t, tab, gl3, n_blocks, n_sel):
    bsz, seq, _ = q3.shape
    nqt = seq // ATT_TQ
    n_cmp = kvcmp.shape[1]
    kern = functools.partial(_cmp_attn_kernel, n_blocks=n_blocks, n_sel=n_sel)
    return pl.pallas_call(
        kern,
        out_shape=(jax.ShapeDtypeStruct((bsz, seq, 512), BF16),
                   jax.ShapeDtypeStruct((bsz, N_KV, n_blocks, seq), BF16)),
        grid=(bsz, nqt),
        in_specs=[pl.BlockSpec((1, ATT_TQ, Q_COLS), lambda b, t: (b, t, 0)),
                  pl.BlockSpec((1, n_cmp, N_KV * LANE), lambda b, t: (b, 0, 0)),
                  pl.BlockSpec((1, N_KV * LANE, n_cmp), lambda b, t: (b, 0, 0)),
                  pl.BlockSpec((N_KV, 1, n_cmp, ATT_ROWS), lambda b, t: (0, t, 0, 0)),
                  pl.BlockSpec((GL_ROWS, ATT_TQ), lambda b, t: (0, b * nqt + t))],
        out_specs=(pl.BlockSpec((1, ATT_TQ, 512), lambda b, t: (b, t, 0)),
                   pl.BlockSpec((1, N_KV, n_blocks, ATT_TQ), lambda b, t: (b, 0, 0, t))),
        compiler_params=_cparams("parallel", "parallel"),
        name="cmp_attn",
    )(q3, kvcmp, kvcmp_t, tab, gl3)


def _band_attn_kernel(*refs, n_delta, span, group, branch, use_sel):
    if use_sel:
        q_ref, kv_ref, kvt_ref, tab_ref, gl_ref, sel_ref, o_ref, s_sc = refs
    else:
        q_ref, kv_ref, kvt_ref, tab_ref, gl_ref, o_ref, s_sc = refs
    qt = pl.program_id(1)
    tq = ATT_TQ
    hw = Q_PER_KV * LANE
    qs = [_stack_heads(q_ref[0, :, g * hw:(g + 1) * hw]) for g in range(N_KV)]
    n_chunks = kv_ref.shape[1] // tq
    c_base = qt - span if span is not None else 0
    if use_sel:
        sel_t = [sel_ref[0, g] for g in range(N_KV)]
        nb = sel_t[0].shape[0]
        kb = lax.broadcasted_iota(jnp.int32, (tq, nb), 0) // L_SEL
        jj = lax.broadcasted_iota(jnp.int32, (tq, nb), 1)

    def chunk_ids(gi, u):
        c = c_base + gi * group + u
        inside = (c >= 0) & (c <= qt)
        dd = jnp.where(inside, jnp.minimum(qt - c, n_delta - 1), n_delta)
        return jnp.clip(c, 0, n_chunks - 1), dd

    def rows(gi, u):
        return pl.ds(pl.multiple_of((gi * group + u) * tq, tq), tq)

    def scores(gi, ms):
        ms = list(ms)
        for u in range(group):
            cc, dd = chunk_ids(gi, u)
            kv = kv_ref[0, pl.ds(pl.multiple_of(cc * tq, tq), tq), :]
            if use_sel:
                expand_t = jnp.where(jj == cc * (tq // L_SEL) + kb, 1.0, 0.0).astype(BF16)
            for g in range(N_KV):
                s = _dot_nt(kv[:, g * LANE:(g + 1) * LANE], qs[g]) + tab_ref[g, dd]
                if use_sel:
                    allowed = _dot(expand_t, sel_t[g])
                    madd = (allowed - 1.0) * (-NEG_INF)
                    s = s + jnp.concatenate([madd] * Q_PER_KV, axis=1)
                s_sc[g, rows(gi, u), :] = s
                ms[g] = jnp.maximum(ms[g], jnp.max(s, axis=0, keepdims=True))
        return tuple(ms)

    m0 = tuple(jnp.full((1, ATT_ROWS), NEG_INF, F32) for _ in range(N_KV))
    if span is not None:
        n_groups = 1
        m = scores(0, m0)
    else:
        n_groups = qt // group + 1
        m = lax.fori_loop(0, n_groups, scores, m0)

    def values(gi, carry):
        carry = [list(c) for c in carry]
        for u in range(group):
            cc, _ = chunk_ids(gi, u)
            for g in range(N_KV):
                l, acc = carry[g]
                p = jnp.exp2(s_sc[g, rows(gi, u), :] - m[g])
                l = l + jnp.sum(p, axis=0, keepdims=True)
                acc = acc + _dot(kvt_ref[0, g, cc, HEAD_DIM:, :], p.astype(BF16))
                carry[g] = [l, acc]
        return tuple(tuple(c) for c in carry)

    init = tuple((jnp.zeros((1, ATT_ROWS), F32), jnp.zeros((HEAD_DIM, ATT_ROWS), F32)) for _ in range(N_KV))
    if span is not None:
        out = values(0, init)
    else:
        out = lax.fori_loop(0, n_groups, values, init)
    for g in range(N_KV):
        l, acc = out[g]
        o_ref[0, :, g * 256:(g + 1) * 256] = _gated_heads(acc / l, gl_ref[g * LANE:(g + 1) * LANE, :], branch)


def _band_attn(q3, kvsw3, kvsw_t, kv_block0, tab, gl3, sel_t, span, branch):
    bsz, seq, _ = q3.shape
    nqt = seq // ATT_TQ
    use_sel = sel_t is not None
    group = span + 1 if span is not None else 4
    slots = group if span is not None else -(-nqt // group) * group
    kern = functools.partial(_band_attn_kernel, n_delta=tab.shape[1] - 1, span=span, group=group, branch=branch,
                             use_sel=use_sel)
    pair = kv_block0 // N_KV
    in_specs = [pl.BlockSpec((1, ATT_TQ, Q_COLS), lambda b, t: (b, t, 0)),
                pl.BlockSpec((1, seq, N_KV * LANE), lambda b, t: (b, 0, pair)),
                pl.BlockSpec((1, N_KV) + kvsw_t.shape[2:], lambda b, t: (b, pair, 0, 0, 0)),
                pl.BlockSpec(tab.shape, lambda b, t: (0, 0, 0, 0)),
                pl.BlockSpec((GL_ROWS, ATT_TQ), lambda b, t: (0, b * nqt + t))]
    args = [q3, kvsw3, kvsw_t, tab, gl3]
    if use_sel:
        in_specs.append(pl.BlockSpec((1, N_KV, sel_t.shape[2], ATT_TQ), lambda b, t: (b, 0, 0, t)))
        args.append(sel_t)
    return pl.pallas_call(
        kern,
        out_shape=jax.ShapeDtypeStruct((bsz, seq, 512), BF16),
        grid=(bsz, nqt),
        in_specs=in_specs,
        out_specs=pl.BlockSpec((1, ATT_TQ, 512), lambda b, t: (b, t, 0)),
        scratch_shapes=[pltpu.VMEM((N_KV, slots * ATT_TQ, ATT_ROWS), F32)],
        compiler_params=_cparams("parallel", "parallel"),
        name="sel_attn" if use_sel else "win_attn",
    )(*args)


def _conv_kernel(cur_ref, prev_ref, w_ref, cb_ref, g_ref, b_ref, o_ref, ybuf, ysh, cbuf):
    i = pl.program_id(1)
    dc = o_ref.shape[-1]
    cur = cur_ref[0]
    prev = prev_ref[0]
    ybuf[CONV_HALO:, :] = cur[:, :dc] * jax.nn.sigmoid(cur[:, dc:])
    halo = prev[:, :dc] * jax.nn.sigmoid(prev[:, dc:])
    ybuf[0:CONV_HALO, :] = jnp.where(i > 0, halo, 0.0)
    n_sh = ysh.shape[1]
    for b in range(1, SUBLANE):
        ysh[b - 1] = ybuf[b:b + n_sh, :]
    off = CONV_HALO - (CONV_WIDTH - 1)
    sub = CONV_RC // SUBLANE
    n_rc = CONV_TQ // CONV_RC
    for lc in range(dc // LANE):
        cols = slice(lc * LANE, (lc + 1) * LANE)
        accs = [jnp.zeros((sub, SUBLANE, LANE), F32) for _ in range(n_rc)]
        for k in range(CONV_WIDTH):
            b = (off + k) % SUBLANE
            wk = w_ref[k, :, cols][None]
            for rc in range(n_rc):
                r0 = rc * CONV_RC + (off + k) - b
                src = ybuf[r0:r0 + CONV_RC, cols] if b == 0 else ysh[b - 1, r0:r0 + CONV_RC, cols]
                accs[rc] = accs[rc] + wk * src.reshape(sub, SUBLANE, LANE)
        for rc in range(n_rc):
            cbuf[rc * CONV_RC:(rc + 1) * CONV_RC, cols] = accs[rc].reshape(CONV_RC, LANE)
    z = cbuf[...] + cb_ref[...]
    mu = jnp.mean(z, axis=-1, keepdims=True)
    zc = z - mu
    var = jnp.mean(zc * zc, axis=-1, keepdims=True)
    y = zc * lax.rsqrt(var + LN_EPS) * g_ref[...] + b_ref[...]
    o_ref[0] = jax.nn.silu(y).astype(BF16)


def _conv_branch(cv3, w, cb, g, b):
    bsz, seq, two_dc = cv3.shape
    dc = two_dc // 2
    per = CONV_TQ // CONV_HALO
    const = lambda bb, i: (0, 0)
    return pl.pallas_call(
        _conv_kernel,
        out_shape=jax.ShapeDtypeStruct((bsz, seq, dc), BF16),
        grid=(bsz, seq // CONV_TQ),
        in_specs=[pl.BlockSpec((1, CONV_TQ, two_dc), lambda bb, i: (bb, i, 0)),
                  pl.BlockSpec((1, CONV_HALO, two_dc), lambda bb, i: (bb, jnp.maximum(i * per - 1, 0), 0)),
                  pl.BlockSpec(w.shape, lambda bb, i: (0, 0, 0)), pl.BlockSpec((1, dc), const),
                  pl.BlockSpec((1, dc), const), pl.BlockSpec((1, dc), const)],
        out_specs=pl.BlockSpec((1, CONV_TQ, dc), lambda bb, i: (bb, i, 0)),
        scratch_shapes=[pltpu.VMEM((CONV_HALO + CONV_TQ, dc), F32),
                        pltpu.VMEM((SUBLANE - 1, CONV_HALO + CONV_TQ - SUBLANE, dc), F32),
                        pltpu.VMEM((CONV_TQ, dc), F32)],
        compiler_params=_cparams("parallel", "parallel"),
        name="conv_branch",
    )(cv3, cv3, w, cb, g, b)


def _outproj_kernel(x_ref, gate_ref, g_ref, b_ref, oc_ref, os_ref, ow_ref, ocv_ref, w_ref, o_ref, *, alpha):
    nsa = (oc_ref[...].astype(F32) + os_ref[...].astype(F32) + ow_ref[...].astype(F32)).astype(BF16)
    half = nsa.shape[1]
    y = _dot(nsa, w_ref[0:half, :]) + _dot(ocv_ref[...], w_ref[half:, :])
    o_ref[...] = _res_ln(x_ref[...], gate_ref[0], y, g_ref[...], b_ref[...], alpha)


def _outproj(x2, gate, lng, lnb, o_cmp, o_sel, o_win, o_conv, w, seq, tm, alpha):
    n, d = x2.shape
    per_b = seq // tm
    row = lambda i: (i, 0)
    const = lambda i: (0, 0)
    half = o_cmp.shape[1]
    return pl.pallas_call(
        functools.partial(_outproj_kernel, alpha=alpha),
        out_shape=jax.ShapeDtypeStruct((n, d), F32),
        grid=(n // tm,),
        in_specs=[pl.BlockSpec((tm, d), row), pl.BlockSpec((1, 1, d), lambda i: (i // per_b, 0, 0)),
                  pl.BlockSpec((1, d), const), pl.BlockSpec((1, d), const),
                  pl.BlockSpec((tm, half), row), pl.BlockSpec((tm, half), row), pl.BlockSpec((tm, half), row),
                  pl.BlockSpec((tm, o_conv.shape[1]), row), pl.BlockSpec(w.shape, const)],
        out_specs=pl.BlockSpec((tm, d), row),
        compiler_params=_cparams("parallel"),
        name="out_proj_ln",
    )(x2, gate, lng, lnb, o_cmp, o_sel, o_win, o_conv, w)


def _ffn1_kernel(x_ref, sc_ref, sh_ref, wg_ref, wu_ref, o_ref, *, tn):
    h = (x_ref[...] * (1.0 + sc_ref[0]) + sh_ref[0]).astype(BF16)
    for j in range(o_ref.shape[1] // tn):
        cols = slice(j * tn, (j + 1) * tn)
        gate = _dot(h, wg_ref[:, cols])
        up = _dot(h, wu_ref[:, cols])
        o_ref[:, cols] = (jax.nn.silu(gate) * up).astype(BF16)


def _ffn1(x2, scale, shift, wg, wu, seq, tm):
    n, d = x2.shape
    dff = wg.shape[1]
    per_b = seq // tm
    row = lambda i: (i, 0)
    mod = lambda i: (i // per_b, 0, 0)
    const = lambda i: (0, 0)
    return pl.pallas_call(
        functools.partial(_ffn1_kernel, tn=256),
        out_shape=jax.ShapeDtypeStruct((n, dff), BF16),
        grid=(n // tm,),
        in_specs=[pl.BlockSpec((tm, d), row), pl.BlockSpec((1, 1, d), mod), pl.BlockSpec((1, 1, d), mod),
                  pl.BlockSpec((d, dff), const), pl.BlockSpec((d, dff), const)],
        out_specs=pl.BlockSpec((tm, dff), row),
        compiler_params=_cparams("parallel"),
        name="ffn_up",
    )(x2, scale, shift, wg, wu)


def _ffn2_kernel(x_ref, gate_ref, g_ref, b_ref, a_ref, w_ref, o_ref, *, alpha):
    y = _dot(a_ref[...], w_ref[...])
    o_ref[...] = _res_ln(x_ref[...], gate_ref[0], y, g_ref[...], b_ref[...], alpha)


def _ffn2(x2, gate, lng, lnb, act, w, seq, tm, alpha):
    n, d = x2.shape
    per_b = seq // tm
    row = lambda i: (i, 0)
    const = lambda i: (0, 0)
    return pl.pallas_call(
        functools.partial(_ffn2_kernel, alpha=alpha),
        out_shape=jax.ShapeDtypeStruct((n, d), F32),
        grid=(n // tm,),
        in_specs=[pl.BlockSpec((tm, d), row), pl.BlockSpec((1, 1, d), lambda i: (i // per_b, 0, 0)),
                  pl.BlockSpec((1, d), const), pl.BlockSpec((1, d), const),
                  pl.BlockSpec((tm, act.shape[1]), row), pl.BlockSpec(w.shape, const)],
        out_specs=pl.BlockSpec((tm, d), row),
        compiler_params=_cparams("parallel"),
        name="ffn_down_ln",
    )(x2, gate, lng, lnb, act, w)


MOE_TM = 1024
MOE_FC = 512
GATHER_ROWS = 1024


def _router_kernel(x_ref, sc_ref, sh_ref, rw_ref, h_ref, ti_ref, tw_ref):
    hf = x_ref[...] * (1.0 + sc_ref[0]) + sh_ref[0]
    h_ref[...] = hf
    lane = lax.broadcasted_iota(jnp.int32, (hf.shape[0], LANE), 1)
    logits = jnp.dot(hf, rw_ref[...], precision=HIGHEST, preferred_element_type=F32)
    logits = jnp.where(lane < N_EXPERTS, logits, -jnp.inf)
    m1 = jnp.max(logits, axis=-1, keepdims=True)
    i1 = jnp.min(jnp.where(logits == m1, lane, LANE), axis=-1, keepdims=True)
    rest = jnp.where(lane == i1, -jnp.inf, logits)
    m2 = jnp.max(rest, axis=-1, keepdims=True)
    i2 = jnp.min(jnp.where(rest == m2, lane, LANE), axis=-1, keepdims=True)
    e2 = jnp.exp(m2 - m1)
    ti_ref[...] = jnp.where(lane == 0, i1, jnp.where(lane == 1, i2, 0))
    tw_ref[...] = jnp.where(lane == 0, 1.0 / (1.0 + e2), jnp.where(lane == 1, e2 / (1.0 + e2), 0.0))


def _router(x2, scale, shift, router_p, seq, tm):
    n, d = x2.shape
    per_b = seq // tm
    row = lambda i: (i, 0)
    mod = lambda i: (i // per_b, 0, 0)
    return pl.pallas_call(
        _router_kernel,
        out_shape=(jax.ShapeDtypeStruct((n, d), F32), jax.ShapeDtypeStruct((n, LANE), jnp.int32),
                   jax.ShapeDtypeStruct((n, LANE), F32)),
        grid=(n // tm,),
        in_specs=[pl.BlockSpec((tm, d), row), pl.BlockSpec((1, 1, d), mod), pl.BlockSpec((1, 1, d), mod),
                  pl.BlockSpec(router_p.shape, lambda i: (0, 0))],
        out_specs=(pl.BlockSpec((tm, d), row), pl.BlockSpec((tm, LANE), row), pl.BlockSpec((tm, LANE), row)),
        compiler_params=_cparams("parallel"),
        name="moe_router",
    )(x2, scale, shift, router_p)


def _gather_rows_kernel(idx_ref, src_ref, o_ref, sem):
    rows = o_ref.shape[0]
    base = pl.program_id(0) * rows

    def issue(r, carry):
        pltpu.make_async_copy(src_ref.at[pl.ds(idx_ref[base + r], 1)], o_ref.at[pl.ds(r, 1)], sem).start()
        return carry

    lax.fori_loop(0, rows, issue, 0, unroll=8)
    pltpu.make_async_copy(src_ref.at[pl.ds(0, rows)], o_ref, sem).wait()


def _gather_rows(idx, src):
    n_out = idx.shape[0]
    d = src.shape[1]
    return pl.pallas_call(
        _gather_rows_kernel,
        out_shape=jax.ShapeDtypeStruct((n_out, d), src.dtype),
        grid_spec=pltpu.PrefetchScalarGridSpec(
            num_scalar_prefetch=1,
            grid=(n_out // GATHER_ROWS,),
            in_specs=[pl.BlockSpec(memory_space=pl.ANY)],
            out_specs=pl.BlockSpec((GATHER_ROWS, d), lambda i, idx_ref: (i, 0)),
            scratch_shapes=[pltpu.SemaphoreType.DMA]),
        compiler_params=_cparams("arbitrary"),
        name="row_gather",
    )(idx, src)


def _expert_up_kernel(te_ref, nu_ref, x_ref, wg_ref, wu_ref, o_ref, xb_sc):
    i = pl.program_id(0)
    f = pl.program_id(1)

    @pl.when(i < nu_ref[0])
    def _compute():
        @pl.when(f == 0)
        def _cast():
            xb_sc[...] = x_ref[...].astype(BF16)
        h = xb_sc[...]
        gate = _dot(h, wg_ref[0, 0])
        up = _dot(h, wu_ref[0, 0])
        o_ref[...] = (jax.nn.silu(gate) * up).astype(BF16)

    @pl.when(i >= nu_ref[0])
    def _unused_tile():
        o_ref[...] = jnp.zeros_like(o_ref)


def _expert_up(tile_expert, n_used, xg, w1b, layer):
    p_rows, d = xg.shape
    nf = w1b.shape[3] // (2 * MOE_FC)
    return pl.pallas_call(
        _expert_up_kernel,
        out_shape=jax.ShapeDtypeStruct((p_rows, nf * MOE_FC), BF16),
        grid_spec=pltpu.PrefetchScalarGridSpec(
            num_scalar_prefetch=2,
            grid=(p_rows // MOE_TM, nf),
            in_specs=[pl.BlockSpec((MOE_TM, d), lambda i, f, te, nu: (i, 0)),
                      pl.BlockSpec((1, 1, d, MOE_FC), lambda i, f, te, nu: (layer, te[i], 0, f)),
                      pl.BlockSpec((1, 1, d, MOE_FC), lambda i, f, te, nu: (layer, te[i], 0, nf + f))],
            out_specs=pl.BlockSpec((MOE_TM, MOE_FC), lambda i, f, te, nu: (i, f)),
            scratch_shapes=[pltpu.VMEM((MOE_TM, d), BF16)]),
        compiler_params=_cparams("parallel", "arbitrary"),
        name="moe_up",
    )(tile_expert, n_used, xg, w1b, w1b)


def _expert_down_kernel(te_ref, nu_ref, a_ref, w_ref, o_ref):
    i = pl.program_id(0)

    @pl.when(i < nu_ref[0])
    def _compute():
        o_ref[...] = _dot(a_ref[...], w_ref[0, 0])

    @pl.when(i >= nu_ref[0])
    def _unused_tile():
        o_ref[...] = jnp.zeros_like(o_ref)


def _expert_down(tile_expert, n_used, act, w2b, layer):
    p_rows, dffe = act.shape
    d = w2b.shape[3]
    return pl.pallas_call(
        _expert_down_kernel,
        out_shape=jax.ShapeDtypeStruct((p_rows, d), F32),
        grid_spec=pltpu.PrefetchScalarGridSpec(
            num_scalar_prefetch=2,
            grid=(p_rows // MOE_TM,),
            in_specs=[pl.BlockSpec((MOE_TM, dffe), lambda i, te, nu: (i, 0)),
                      pl.BlockSpec((1, 1, dffe, d), lambda i, te, nu: (layer, te[i], 0, 0))],
            out_specs=pl.BlockSpec((MOE_TM, d), lambda i, te, nu: (i, 0))),
        compiler_params=_cparams("parallel"),
        name="moe_down",
    )(tile_expert, n_used, act, w2b)


def _moe_combine_kernel(x_ref, gate_ref, g_ref, b_ref, tw_ref, y1_ref, y2_ref, o_ref, *, alpha):
    tw = tw_ref[...]
    y = tw[:, 0:1] * y1_ref[...] + tw[:, 1:2] * y2_ref[...]
    o_ref[...] = _res_ln(x_ref[...], gate_ref[0], y, g_ref[...], b_ref[...], alpha)


def _moe_combine(x2, gate, lng, lnb, tw, yg, seq, tm, alpha):
    n, d = x2.shape
    per_b = seq // tm
    nt = n // tm
    row = lambda i: (i, 0)
    const = lambda i: (0, 0)
    return pl.pallas_call(
        functools.partial(_moe_combine_kernel, alpha=alpha),
        out_shape=jax.ShapeDtypeStruct((n, d), F32),
        grid=(nt,),
        in_specs=[pl.BlockSpec((tm, d), row), pl.BlockSpec((1, 1, d), lambda i: (i // per_b, 0, 0)),
                  pl.BlockSpec((1, d), const), pl.BlockSpec((1, d), const), pl.BlockSpec((tm, LANE), row),
                  pl.BlockSpec((tm, d), row), pl.BlockSpec((tm, d), lambda i: (nt + i, 0))],
        out_specs=pl.BlockSpec((tm, d), row),
        compiler_params=_cparams("parallel"),
        name="moe_combine_ln",
    )(x2, gate, lng, lnb, tw, yg, yg)


def _route_tables(ti, n_rows):
    n = ti.shape[0]
    e_flat = jnp.concatenate([ti[:, 0], ti[:, 1]])
    onehot = (e_flat[:, None] == jnp.arange(N_EXPERTS, dtype=jnp.int32)[None, :]).astype(jnp.int32)
    cum = jnp.cumsum(onehot, axis=0)
    counts = cum[-1]
    padded = (counts + MOE_TM - 1) // MOE_TM * MOE_TM
    ends = jnp.cumsum(padded)
    pos = jnp.sum(onehot * (cum - 1 + (ends - padded)[None, :]), axis=1)
    tok = jnp.tile(jnp.arange(n, dtype=jnp.int32), 2)
    src = (jnp.arange(n_rows, dtype=jnp.int32) % n).at[pos].set(tok)
    tile_start = jnp.arange(n_rows // MOE_TM, dtype=jnp.int32) * MOE_TM
    tile_expert = jnp.minimum(jnp.sum((tile_start[:, None] >= ends[None, :]).astype(jnp.int32), axis=1),
                              N_EXPERTS - 1)
    return pos.astype(jnp.int32), src, tile_expert, (ends[-1:] // MOE_TM).astype(jnp.int32)


def _moe(x2, scale, shift, gate, lng, lnb, router_p, w1b, w2b, layer, seq, tm, alpha):
    n, d = x2.shape
    n_rows = 2 * n + N_EXPERTS * MOE_TM
    assert n_rows % GATHER_ROWS == 0 and (2 * n) % GATHER_ROWS == 0
    h, ti, tw = _router(x2, scale, shift, router_p, seq, tm)
    pos, src, tile_expert, n_used = _route_tables(ti, n_rows)
    xg = _gather_rows(src, h)
    act = _expert_up(tile_expert, n_used, xg, w1b, layer)
    y = _expert_down(tile_expert, n_used, act, w2b, layer)
    yg = _gather_rows(pos, y)
    return _moe_combine(x2, gate, lng, lnb, tw, yg, seq, tm, alpha)


def _rel_bucket_np(dist):
    n = np.maximum(dist, 0)
    max_exact = NUM_BUCKETS // 2
    nf = np.maximum(n, 1).astype(np.float32)
    large = max_exact + (np.log(nf / np.float32(max_exact)) / np.float32(math.log(MAX_DISTANCE / max_exact))
                         * np.float32(NUM_BUCKETS - max_exact)).astype(np.int32)
    large = np.minimum(large, NUM_BUCKETS - 1)
    return np.where(n < max_exact, n, large).astype(np.int32)


def _bias_lookup(rel_bias, dist, valid):
    onehot = _rel_bucket_np(dist)[..., None] == np.arange(NUM_BUCKETS)
    vals = jnp.einsum('...k,kh->...h', jnp.asarray(onehot).astype(F32), rel_bias, precision=HIGHEST)
    return jnp.where(jnp.asarray(valid)[..., None], vals * LOG2E, NEG_INF)


def _key_major(vals, add_masked_tile=False):
    n_tiles, tq, tk, _ = vals.shape
    if add_masked_tile:
        vals = jnp.concatenate([vals, jnp.full((1,) + vals.shape[1:], NEG_INF, vals.dtype)], axis=0)
        n_tiles += 1
    vals = vals.reshape(n_tiles, tq, tk, N_KV, Q_PER_KV).transpose(3, 0, 2, 4, 1)
    return vals.reshape(N_KV, n_tiles, tk, Q_PER_KV * tq).astype(F32)


def _band_dist(n_delta):
    i = np.arange(ATT_TQ)[None, :, None]
    k = np.arange(ATT_TQ)[None, None, :]
    return ATT_TQ * np.arange(n_delta)[:, None, None] + i - k


def _cmp_bias_table(rel_bias, n_qt, n_cmp_pad):
    per_tile = ATT_TQ // CMP_STRIDE
    shift = per_tile * (n_qt - 1)
    rel_n = np.arange(n_cmp_pad + shift) - shift
    dist = np.arange(ATT_TQ)[:, None] - (CMP_STRIDE * rel_n[None, :] + L_CMP - 1)
    base = _bias_lookup(rel_bias, dist, dist >= 0)
    tiles = [base[:, shift - per_tile * qt: shift - per_tile * qt + n_cmp_pad] for qt in range(n_qt)]
    return _key_major(jnp.stack(tiles))


def _inproj_weight(w_in):
    depth, d, _ = w_in.shape
    d_nsa = N_HEADS * HEAD_DIM
    d_kv = N_KV * HEAD_DIM
    base_ks = d_nsa + 2 * d_kv
    base_gl = d_nsa + 6 * d_kv
    base_cv = base_gl + N_HEADS * N_BRANCH
    wq = jnp.pad(w_in[:, :, :d_nsa].reshape(depth, d, N_HEADS, HEAD_DIM),
                 ((0, 0), (0, 0), (0, 0), (0, LANE - HEAD_DIM))).reshape(depth, d, Q_COLS)
    parts = [wq, w_in[:, :, d_nsa:base_ks]]
    for pair in range(2):
        kbase = base_ks + pair * 2 * d_kv
        for g in range(N_KV):
            parts.append(w_in[:, :, kbase + g * HEAD_DIM: kbase + (g + 1) * HEAD_DIM])
            parts.append(w_in[:, :, kbase + d_kv + g * HEAD_DIM: kbase + d_kv + (g + 1) * HEAD_DIM])
    parts.append(w_in[:, :, base_cv:])
    gl = w_in[:, :, base_gl:base_cv].reshape(depth, d, N_KV, Q_PER_KV, N_BRANCH).transpose(0, 2, 4, 3, 1)
    gl = jnp.pad(gl.reshape(depth, N_KV, N_BRANCH * Q_PER_KV, d),
                 ((0, 0), (0, 0), (0, LANE - N_BRANCH * Q_PER_KV), (0, 0)))
    return jnp.concatenate(parts, axis=-1).astype(BF16), gl.reshape(depth, GL_ROWS, d).astype(BF16)


def kernel(x, c, w_in, cmp_pos, cmp_w1, cmp_w2, conv_w, conv_b, conv_ln_g, conv_ln_b, w_out, rel_bias,
           ada_w, ada_b, ln_g, ln_b, ffn_w1, ffn_w2, router_w, moe_w1, moe_w2):
    bsz, seq, d = x.shape
    depth = w_in.shape[0]
    n = bsz * seq
    alpha = (2 * depth) ** 0.25
    n_blocks = seq // L_SEL
    n_sel = min(N_SEL, n_blocks)
    n_cmp = (seq - L_CMP) // CMP_STRIDE + 1
    nrow = seq // CMP_STRIDE
    assert d == 1024 and seq % 512 == 0 and nrow == LANE and n_cmp == nrow - 1
    tm = 512

    n_qt = seq // ATT_TQ
    d_win = _band_dist(WINDOW // ATT_TQ + 1)
    win_tab = _key_major(_bias_lookup(rel_bias, d_win, (d_win >= 0) & (d_win < WINDOW)), add_masked_tile=True)
    d_sel = _band_dist(3)
    assert _rel_bucket_np(d_sel[2]).min() == NUM_BUCKETS - 1
    sel_tab = _key_major(_bias_lookup(rel_bias, d_sel, d_sel >= 0), add_masked_tile=True)
    cmp_tab = _cmp_bias_table(rel_bias, n_qt, nrow)

    w_in_p, w_gl_t = _inproj_weight(w_in)
    eye2 = jnp.eye(2, dtype=F32)
    w1r = cmp_w1.reshape(depth, 2, L_CMP, HEAD_DIM, HEAD_DIM)
    w1x = jnp.einsum('zslde,st,gh->zlsgdhte', w1r, eye2, eye2).reshape(depth, L_CMP, 4 * HEAD_DIM, 4 * HEAD_DIM)
    half = L_CMP // 2
    w_lo = w1x[:, :half].reshape(depth, half * 4 * HEAD_DIM, 4 * HEAD_DIM).astype(BF16)
    w_hi = w1x[:, half:].reshape(depth, half * 4 * HEAD_DIM, 4 * HEAD_DIM).astype(BF16)
    w2bd = jnp.einsum('zsde,st,gh->zgsdhte', cmp_w2, eye2, eye2).reshape(depth, 4 * HEAD_DIM, 4 * HEAD_DIM).astype(BF16)
    posx = jnp.broadcast_to(cmp_pos[:, :, :, None, :], (depth, 2, L_CMP, N_KV, HEAD_DIM)).transpose(0, 2, 1, 3, 4)
    pos_lo = posx[:, :half].reshape(depth, 1, half * 4 * HEAD_DIM)
    pos_hi = posx[:, half:].reshape(depth, 1, half * 4 * HEAD_DIM)
    conv_wp = jnp.broadcast_to(conv_w[:, :, None, :], conv_w.shape[:2] + (SUBLANE, conv_w.shape[2]))
    w_out_b = w_out.astype(BF16)
    dff = ffn_w2.shape[1]
    ffn_wg = ffn_w1[:, :, :dff].astype(BF16)
    ffn_wu = ffn_w1[:, :, dff:].astype(BF16)
    ffn_w2b = ffn_w2.astype(BF16)
    router_p = jnp.pad(router_w, ((0, 0), (0, 0), (0, LANE - N_EXPERTS)))
    moe_w1b = moe_w1.astype(BF16)
    moe_w2b = moe_w2.astype(BF16)

    mod = _ada(c, ada_w, ada_b).reshape(depth, 6, bsz, 1, d)
    x2 = x.reshape(n, d)
    for l in range(depth):
        shift1, scale1, gate1, shift2, scale2, gate2 = [mod[l, k] for k in range(6)]
        q, kvc, kvsw, gl3, cv = _inproj(x2, scale1, shift1, w_in_p[l], w_gl_t[l], seq, tm)
        q3 = q.reshape(bsz, seq, Q_COLS)
        kvsw3 = kvsw.reshape(bsz, seq, KVSW_COLS)
        kvsw_t = kvsw3.reshape(bsz, n_qt, ATT_TQ, 2 * N_KV, LANE).transpose(0, 3, 1, 4, 2)
        kvcmp, kvcmp_t = _compress(kvc.reshape(bsz, nrow, CMP_STRIDE * 256), pos_lo[l], pos_hi[l], w_lo[l], w_hi[l],
                                   w2bd[l])
        o_cmp, sel_t = _cmp_attn(q3, kvcmp, kvcmp_t, cmp_tab, gl3, n_blocks, n_sel)
        o_sel = _band_attn(q3, kvsw3, kvsw_t, 0, sel_tab, gl3, sel_t, None, 1)
        o_win = _band_attn(q3, kvsw3, kvsw_t, N_KV, win_tab, gl3, None, WINDOW // ATT_TQ, 2)
        o_conv = _conv_branch(cv.reshape(bsz, seq, 1024), conv_wp[l], conv_b[l][None], conv_ln_g[l][None],
                              conv_ln_b[l][None])
        x2 = _outproj(x2, gate1, ln_g[l, 0][None], ln_b[l, 0][None], o_cmp.reshape(n, 512), o_sel.reshape(n, 512),
                      o_win.reshape(n, 512), o_conv.reshape(n, 512), w_out_b[l], seq, tm, alpha)
        if l % 2 == 0:
            act = _ffn1(x2, scale2, shift2, ffn_wg[l // 2], ffn_wu[l // 2], seq, tm)
            x2 = _ffn2(x2, gate2, ln_g[l, 1][None], ln_b[l, 1][None], act, ffn_w2b[l // 2], seq, tm, alpha)
        else:
            x2 = _moe(x2, scale2, shift2, gate2, ln_g[l, 1][None], ln_b[l, 1][None], router_p[l // 2],
                      moe_w1b, moe_w2b, l // 2, seq, tm, alpha)
    return x2.reshape(bsz, seq, d)
```

```python
import functools
import math

import numpy as np
import jax
import jax.numpy as jnp
from jax import lax
from jax.experimental import pallas as pl
from jax.experimental.pallas import tpu as pltpu

F32 = jnp.float32
BF16 = jnp.bfloat16
HIGHEST = lax.Precision.HIGHEST

HEAD_DIM = 64
N_HEADS = 8
N_KV = 2
Q_PER_KV = N_HEADS // N_KV
N_BRANCH = 3
L_CMP = 32
CMP_STRIDE = 16
L_SEL = 64
N_SEL = 8
WINDOW = 512
CONV_WIDTH = 31
NUM_BUCKETS = 32
MAX_DISTANCE = 128
N_EXPERTS = 8
LN_EPS = 1e-5
NEG_INF = -1e30
LOG2E = math.log2(math.e)

LANE = 128
SUBLANE = 8
ATT_TQ = 128
ATT_ROWS = Q_PER_KV * ATT_TQ
CONV_TQ = 256
CONV_HALO = 32
CONV_RC = 64
VMEM_LIMIT_BYTES = 52 * 1024 * 1024


def _cparams(*sem):
    return pltpu.CompilerParams(dimension_semantics=sem, vmem_limit_bytes=VMEM_LIMIT_BYTES)


def _dot(a, b):
    return jnp.dot(a, b, preferred_element_type=F32)


def _dot_nt(a, b, precision=None):
    return lax.dot_general(a, b, (((1,), (1,)), ((), ())), precision=precision,
                           preferred_element_type=F32)


def _res_ln(x, gate, y, g, b, alpha):
    z = alpha * x + (1.0 + gate) * y
    mu = jnp.mean(z, axis=-1, keepdims=True)
    zc = z - mu
    var = jnp.mean(zc * zc, axis=-1, keepdims=True)
    return zc * lax.rsqrt(var + LN_EPS) * g + b


def _ada_kernel(c_ref, w_ref, b_ref, o_ref):
    sc = jax.nn.silu(c_ref[...])
    o_ref[0, 0] = jnp.dot(sc, w_ref[0], precision=HIGHEST, preferred_element_type=F32) + b_ref[0]


def _ada(c, ada_w, ada_b):
    depth, d, _ = ada_w.shape
    bsz = c.shape[0]
    return pl.pallas_call(
        _ada_kernel,
        out_shape=jax.ShapeDtypeStruct((depth, 6, bsz, d), F32),
        grid=(depth, 6),
        in_specs=[pl.BlockSpec((bsz, d), lambda l, k: (0, 0)),
                  pl.BlockSpec((1, d, d), lambda l, k: (l, 0, k)),
                  pl.BlockSpec((1, 1, d), lambda l, k: (l, 0, k))],
        out_specs=pl.BlockSpec((1, 1, bsz, d), lambda l, k: (l, k, 0, 0)),
        compiler_params=_cparams("parallel", "parallel"),
        name="ada_mod",
    )(c, ada_w, ada_b.reshape(depth, 1, 6 * d))


Q_COLS = N_HEADS * LANE
KVC_COLS = 4 * HEAD_DIM
KVSW_COLS = 8 * HEAD_DIM
GL_ROWS = N_KV * LANE


def _inproj_kernel(x_ref, sc_ref, sh_ref, w_ref, wgl_ref, q_ref, kvc_ref, kvsw_ref, glt_ref, cv_ref):
    h = (x_ref[...] * (1.0 + sc_ref[0]) + sh_ref[0]).astype(BF16)
    c0, c1, c2 = Q_COLS, Q_COLS + KVC_COLS, Q_COLS + KVC_COLS + KVSW_COLS
    q_ref[...] = (_dot(h, w_ref[:, 0:c0]) * (HEAD_DIM ** -0.5 * LOG2E)).astype(BF16)
    kvc_ref[...] = _dot(h, w_ref[:, c0:c1])
    kvsw_ref[...] = _dot(h, w_ref[:, c1:c2]).astype(BF16)
    cv_ref[...] = _dot(h, w_ref[:, c2:])
    glt_ref[...] = _dot_nt(wgl_ref[...], h)


def _inproj(x2, scale, shift, w, wgl_t, seq, tm):
    n, d = x2.shape
    per_b = seq // tm
    d_cv = w.shape[1] - (Q_COLS + KVC_COLS + KVSW_COLS)
    row = lambda i: (i, 0)
    mod = lambda i: (i // per_b, 0, 0)
    const = lambda i: (0, 0)
    return pl.pallas_call(
        _inproj_kernel,
        out_shape=(jax.ShapeDtypeStruct((n, Q_COLS), BF16), jax.ShapeDtypeStruct((n, KVC_COLS), F32),
                   jax.ShapeDtypeStruct((n, KVSW_COLS), BF16), jax.ShapeDtypeStruct((GL_ROWS, n), F32),
                   jax.ShapeDtypeStruct((n, d_cv), F32)),
        grid=(n // tm,),
        in_specs=[pl.BlockSpec((tm, d), row), pl.BlockSpec((1, 1, d), mod), pl.BlockSpec((1, 1, d), mod),
                  pl.BlockSpec(w.shape, const), pl.BlockSpec(wgl_t.shape, const)],
        out_specs=(pl.BlockSpec((tm, Q_COLS), row), pl.BlockSpec((tm, KVC_COLS), row),
                   pl.BlockSpec((tm, KVSW_COLS), row), pl.BlockSpec((GL_ROWS, tm), lambda i: (0, i)),
                   pl.BlockSpec((tm, d_cv), row)),
        compiler_params=_cparams("parallel"),
        name="in_proj",
    )(x2, scale, shift, w, wgl_t)


def _compress_kernel(a_ref, plo_ref, phi_ref, wlo_ref, whi_ref, w2_ref, o_ref, ot_ref):
    a = a_ref[0]
    lo = _dot((a + plo_ref[...]).astype(BF16), wlo_ref[...])
    hi = _dot((a + phi_ref[...]).astype(BF16), whi_ref[...])
    nrow = lo.shape[0]
    pre = lo + pltpu.roll(hi, nrow - 1, 0)
    act = jax.nn.gelu(pre, approximate=True)
    out = _dot(act.astype(BF16), w2_ref[...])
    o_ref[0] = out.astype(BF16)
    ot_ref[0] = out.T.astype(BF16)


def _compress(kvc3, pos_lo, pos_hi, wlo, whi, w2bd):
    bsz, nrow, width = kvc3.shape
    const = lambda b: (0, 0)
    return pl.pallas_call(
        _compress_kernel,
        out_shape=(jax.ShapeDtypeStruct((bsz, nrow, 256), BF16), jax.ShapeDtypeStruct((bsz, 256, nrow), BF16)),
        grid=(bsz,),
        in_specs=[pl.BlockSpec((1, nrow, width), lambda b: (b, 0, 0)),
                  pl.BlockSpec((1, width), const), pl.BlockSpec((1, width), const),
                  pl.BlockSpec((width, 256), const), pl.BlockSpec((width, 256), const),
                  pl.BlockSpec((256, 256), const)],
        out_specs=(pl.BlockSpec((1, nrow, 256), lambda b: (b, 0, 0)),
                   pl.BlockSpec((1, 256, nrow), lambda b: (b, 0, 0))),
        compiler_params=_cparams("parallel"),
        name="compress_mlp",
    )(kvc3, pos_lo, pos_hi, wlo, whi, w2bd)


def _stack_heads(q):
    return jnp.concatenate([q[:, r * LANE:(r + 1) * LANE] for r in range(Q_PER_KV)], axis=0)


def _branch_gates(gl_t, branch):
    return jnp.concatenate([jax.nn.sigmoid(gl_t[branch * Q_PER_KV + r:branch * Q_PER_KV + r + 1, :])
                            for r in range(Q_PER_KV)], axis=1)


def _heads_to_rows(o_t):
    tq = o_t.shape[1] // Q_PER_KV
    og = o_t.astype(BF16)
    ii = lax.broadcasted_iota(jnp.int32, (tq, tq), 0)
    kk = lax.broadcasted_iota(jnp.int32, (tq, tq), 1)
    eye = jnp.where(ii == kk, 1.0, 0.0).astype(BF16)
    halves = []
    for pair in range(Q_PER_KV // 2):
        two_heads = jnp.concatenate([og[:, (2 * pair) * tq:(2 * pair + 1) * tq],
                                     og[:, (2 * pair + 1) * tq:(2 * pair + 2) * tq]], axis=0)
        halves.append(_dot_nt(eye, two_heads))
    return jnp.concatenate(halves, axis=1).astype(BF16)


def _cmp_section(qs, qt, kv_ref, kvt_ref, tab_ref, gl_ref, sel_sc, *, n_blocks, n_sel):
    tq = ATT_TQ
    outs = []
    jj = lax.broadcasted_iota(jnp.int32, (n_blocks, LANE), 0)
    nn = lax.broadcasted_iota(jnp.int32, (n_blocks, LANE), 1)
    overlap_t = jnp.where((CMP_STRIDE * nn < L_SEL * jj + L_SEL) & (CMP_STRIDE * nn + L_CMP > L_SEL * jj),
                          1.0, 0.0).astype(F32)
    jb = lax.broadcasted_iota(jnp.int32, (n_blocks, tq), 0)
    tb = (qt * tq + lax.broadcasted_iota(jnp.int32, (n_blocks, tq), 1)) // L_SEL
    for g in range(N_KV):
        bias = tab_ref[g, 0]
        s = _dot_nt(kv_ref[0, :, g * LANE:(g + 1) * LANE], qs[g]) + bias
        valid = bias > 0.5 * NEG_INF
        m = jnp.max(s, axis=0, keepdims=True)
        p = jnp.where(valid, jnp.exp2(s - m), 0.0)
        l = jnp.sum(p, axis=0, keepdims=True)
        p = p / jnp.where(l > 0.0, l, 1.0)
        o_t = _dot(kvt_ref[0, g * LANE + HEAD_DIM:(g + 1) * LANE, :], p.astype(BF16))
        outs.append(o_t * _branch_gates(gl_ref[g * LANE:(g + 1) * LANE, :], 0))

        psum = p[:, 0:tq] + p[:, tq:2 * tq] + p[:, 2 * tq:3 * tq] + p[:, 3 * tq:4 * tq]
        imp_t = jnp.dot(overlap_t, psum, precision=HIGHEST, preferred_element_type=F32)
        score = jnp.where(jb > tb, -jnp.inf,
                          jnp.where((jb == 0) | (jb == tb) | (jb == tb - 1), jnp.inf, imp_t))
        rank = jnp.zeros((n_blocks, tq), F32)
        for jp in range(n_blocks):
            row = score[jp:jp + 1, :]
            earlier = jnp.where(jb > jp, 1.0, 0.0)
            rank = rank + jnp.where(row > score, 1.0, 0.0) + jnp.where(row == score, earlier, 0.0)
        madd = jnp.where(rank < float(n_sel), 0.0, NEG_INF)
        per_chunk = ATT_TQ // L_SEL
        for c in range(n_blocks // per_chunk):
            sel_sc[g, c, 0:per_chunk, :] = madd[c * per_chunk:(c + 1) * per_chunk, :]
    return outs


def _band_section(qs, qt, kv_ref, kvt_ref, tab_ref, gl_ref, sel_sc, s_sc, *, span, group, branch):
    tq = ATT_TQ
    n_delta = tab_ref.shape[1] - 1
    n_chunks = kv_ref.shape[1] // tq
    c_base = qt - span if span is not None else 0
    use_sel = sel_sc is not None

    def chunk_ids(gi, u):
        c = c_base + gi * group + u
        inside = (c >= 0) & (c <= qt)
        dd = jnp.where(inside, jnp.minimum(qt - c, n_delta - 1), n_delta)
        return jnp.clip(c, 0, n_chunks - 1), dd

    def rows(gi, u):
        return pl.ds(pl.multiple_of((gi * group + u) * tq, tq), tq)

    def scores(gi, ms):
        ms = list(ms)
        for u in range(group):
            cc, dd = chunk_ids(gi, u)
            kv = kv_ref[0, pl.ds(pl.multiple_of(cc * tq, tq), tq), :]
            for g in range(N_KV):
                s = _dot_nt(kv[:, g * LANE:(g + 1) * LANE], qs[g]) + tab_ref[g, dd]
                if use_sel:
                    blocks = sel_sc[g, cc, 0:tq // L_SEL, :]
                    madd = jnp.concatenate([jnp.broadcast_to(blocks[j:j + 1, :], (L_SEL, tq))
                                            for j in range(tq // L_SEL)], axis=0)
                    s = s + jnp.concatenate([madd] * Q_PER_KV, axis=1)
                s_sc[g, rows(gi, u), :] = s
                ms[g] = jnp.maximum(ms[g], jnp.max(s, axis=0, keepdims=True))
        return tuple(ms)

    m0 = tuple(jnp.full((1, ATT_ROWS), NEG_INF, F32) for _ in range(N_KV))
    if span is not None:
        n_groups = 1
        m = scores(0, m0)
    else:
        n_groups = qt // group + 1
        m = lax.fori_loop(0, n_groups, scores, m0)

    def values(gi, carry):
        carry = [list(c) for c in carry]
        for u in range(group):
            cc, _ = chunk_ids(gi, u)
            for g in range(N_KV):
                l, acc = carry[g]
                p = jnp.exp2(s_sc[g, rows(gi, u), :] - m[g])
                l = l + jnp.sum(p, axis=0, keepdims=True)
                acc = acc + _dot(kvt_ref[0, g, cc, HEAD_DIM:, :], p.astype(BF16))
                carry[g] = [l, acc]
        return tuple(tuple(c) for c in carry)

    init = tuple((jnp.zeros((1, ATT_ROWS), F32), jnp.zeros((HEAD_DIM, ATT_ROWS), F32)) for _ in range(N_KV))
    if span is not None:
        out = values(0, init)
    else:
        out = lax.fori_loop(0, n_groups, values, init)
    return [out[g][1] / out[g][0] * _branch_gates(gl_ref[g * LANE:(g + 1) * LANE, :], branch) for g in range(N_KV)]


SEL_GROUP = 4
WIN_SPAN = WINDOW // ATT_TQ


def _nsa_kernel(q_ref, kvc_ref, kvct_ref, ctab_ref, kvs_ref, kvst_ref, stab_ref, kvw_ref, kvwt_ref, wtab_ref,
                gl_ref, o_ref, sel_sc, ssel_sc, swin_sc, *, n_blocks, n_sel):
    qt = pl.program_id(1)
    hw = Q_PER_KV * LANE
    qs = [_stack_heads(q_ref[0, :, g * hw:(g + 1) * hw]) for g in range(N_KV)]
    o_cmp = _cmp_section(qs, qt, kvc_ref, kvct_ref, ctab_ref, gl_ref, sel_sc, n_blocks=n_blocks, n_sel=n_sel)
    o_sel = _band_section(qs, qt, kvs_ref, kvst_ref, stab_ref, gl_ref, sel_sc, ssel_sc,
                          span=None, group=SEL_GROUP, branch=1)
    o_win = _band_section(qs, qt, kvw_ref, kvwt_ref, wtab_ref, gl_ref, None, swin_sc,
                          span=WIN_SPAN, group=WIN_SPAN + 1, branch=2)
    for g in range(N_KV):
        o_ref[0, :, g * 256:(g + 1) * 256] = _heads_to_rows(o_cmp[g] + o_sel[g] + o_win[g])


def _nsa_attn(q3, kvcmp, kvcmp_t, cmp_tab, kvsw3, kvsw_t, sel_tab, win_tab, gl_t, n_blocks, n_sel):
    bsz, seq, _ = q3.shape
    nqt = seq // ATT_TQ
    n_cmp = kvcmp.shape[1]
    sel_slots = -(-nqt // SEL_GROUP) * SEL_GROUP
    full4 = lambda b, t: (0, 0, 0, 0)
    kv_spec = lambda pair: pl.BlockSpec((1, seq, N_KV * LANE), lambda b, t: (b, 0, pair))
    kvt_spec = lambda pair: pl.BlockSpec((1, N_KV) + kvsw_t.shape[2:], lambda b, t: (b, pair, 0, 0, 0))
    return pl.pallas_call(
        functools.partial(_nsa_kernel, n_blocks=n_blocks, n_sel=n_sel),
        out_shape=jax.ShapeDtypeStruct((bsz, seq, 512), BF16),
        grid=(bsz, nqt),
        in_specs=[pl.BlockSpec((1, ATT_TQ, Q_COLS), lambda b, t: (b, t, 0)),
                  pl.BlockSpec((1, n_cmp, N_KV * LANE), lambda b, t: (b, 0, 0)),
                  pl.BlockSpec((1, N_KV * LANE, n_cmp), lambda b, t: (b, 0, 0)),
                  pl.BlockSpec((N_KV, 1, n_cmp, ATT_ROWS), lambda b, t: (0, t, 0, 0)),
                  kv_spec(0), kvt_spec(0), pl.BlockSpec(sel_tab.shape, full4),
                  kv_spec(1), kvt_spec(1), pl.BlockSpec(win_tab.shape, full4),
                  pl.BlockSpec((GL_ROWS, ATT_TQ), lambda b, t: (0, b * nqt + t))],
        out_specs=pl.BlockSpec((1, ATT_TQ, 512), lambda b, t: (b, t, 0)),
        scratch_shapes=[pltpu.VMEM((N_KV, n_blocks * L_SEL // ATT_TQ, SUBLANE, ATT_TQ), F32),
                        pltpu.VMEM((N_KV, sel_slots * ATT_TQ, ATT_ROWS), F32),
                        pltpu.VMEM((N_KV, (WIN_SPAN + 1) * ATT_TQ, ATT_ROWS), F32)],
        compiler_params=_cparams("parallel", "parallel"),
        name="nsa_attn",
    )(q3, kvcmp, kvcmp_t, cmp_tab, kvsw3, kvsw_t, sel_tab, kvsw3, kvsw_t, win_tab, gl_t)


def _conv_kernel(cur_ref, prev_ref, w_ref, cb_ref, g_ref, b_ref, o_ref, ybuf, ysh, cbuf):
    i = pl.program_id(1)
    dc = o_ref.shape[-1]
    cur = cur_ref[0]
    prev = prev_ref[0]
    ybuf[CONV_HALO:, :] = cur[:, :dc] * jax.nn.sigmoid(cur[:, dc:])
    halo = prev[:, :dc] * jax.nn.sigmoid(prev[:, dc:])
    ybuf[0:CONV_HALO, :] = jnp.where(i > 0, halo, 0.0)
    n_sh = ysh.shape[1]
    for b in range(1, SUBLANE):
        ysh[b - 1] = ybuf[b:b + n_sh, :]
    off = CONV_HALO - (CONV_WIDTH - 1)
    sub = CONV_RC // SUBLANE
    n_rc = CONV_TQ // CONV_RC
    for lc in range(dc // LANE):
        cols = slice(lc * LANE, (lc + 1) * LANE)
        accs = [jnp.zeros((sub, SUBLANE, LANE), F32) for _ in range(n_rc)]
        for k in range(CONV_WIDTH):
            b = (off + k) % SUBLANE
            wk = w_ref[k, :, cols][None]
            for rc in range(n_rc):
                r0 = rc * CONV_RC + (off + k) - b
                src = ybuf[r0:r0 + CONV_RC, cols] if b == 0 else ysh[b - 1, r0:r0 + CONV_RC, cols]
                accs[rc] = accs[rc] + wk * src.reshape(sub, SUBLANE, LANE)
        for rc in range(n_rc):
            cbuf[rc * CONV_RC:(rc + 1) * CONV_RC, cols] = accs[rc].reshape(CONV_RC, LANE)
    z = cbuf[...] + cb_ref[...]
    mu = jnp.mean(z, axis=-1, keepdims=True)
    zc = z - mu
    var = jnp.mean(zc * zc, axis=-1, keepdims=True)
    y = zc * lax.rsqrt(var + LN_EPS) * g_ref[...] + b_ref[...]
    o_ref[0] = jax.nn.silu(y).astype(BF16)


def _conv_branch(cv3, w, cb, g, b):
    bsz, seq, two_dc = cv3.shape
    dc = two_dc // 2
    per = CONV_TQ // CONV_HALO
    const = lambda bb, i: (0, 0)
    return pl.pallas_call(
        _conv_kernel,
        out_shape=jax.ShapeDtypeStruct((bsz, seq, dc), BF16),
        grid=(bsz, seq // CONV_TQ),
        in_specs=[pl.BlockSpec((1, CONV_TQ, two_dc), lambda bb, i: (bb, i, 0)),
                  pl.BlockSpec((1, CONV_HALO, two_dc), lambda bb, i: (bb, jnp.maximum(i * per - 1, 0), 0)),
                  pl.BlockSpec(w.shape, lambda bb, i: (0, 0, 0)), pl.BlockSpec((1, dc), const),
                  pl.BlockSpec((1, dc), const), pl.BlockSpec((1, dc), const)],
        out_specs=pl.BlockSpec((1, CONV_TQ, dc), lambda bb, i: (bb, i, 0)),
        scratch_shapes=[pltpu.VMEM((CONV_HALO + CONV_TQ, dc), F32),
                        pltpu.VMEM((SUBLANE - 1, CONV_HALO + CONV_TQ - SUBLANE, dc), F32),
                        pltpu.VMEM((CONV_TQ, dc), F32)],
        compiler_params=_cparams("parallel", "parallel"),
        name="conv_branch",
    )(cv3, cv3, w, cb, g, b)


def _outproj_kernel(x_ref, gate_ref, g_ref, b_ref, nsa_ref, ocv_ref, w_ref, o_ref, *, alpha):
    half = nsa_ref.shape[1]
    y = _dot(nsa_ref[...], w_ref[0:half, :]) + _dot(ocv_ref[...], w_ref[half:, :])
    o_ref[...] = _res_ln(x_ref[...], gate_ref[0], y, g_ref[...], b_ref[...], alpha)


def _outproj(x2, gate, lng, lnb, o_nsa, o_conv, w, seq, tm, alpha):
    n, d = x2.shape
    per_b = seq // tm
    row = lambda i: (i, 0)
    const = lambda i: (0, 0)
    return pl.pallas_call(
        functools.partial(_outproj_kernel, alpha=alpha),
        out_shape=jax.ShapeDtypeStruct((n, d), F32),
        grid=(n // tm,),
        in_specs=[pl.BlockSpec((tm, d), row), pl.BlockSpec((1, 1, d), lambda i: (i // per_b, 0, 0)),
                  pl.BlockSpec((1, d), const), pl.BlockSpec((1, d), const),
                  pl.BlockSpec((tm, o_nsa.shape[1]), row), pl.BlockSpec((tm, o_conv.shape[1]), row),
                  pl.BlockSpec(w.shape, const)],
        out_specs=pl.BlockSpec((tm, d), row),
        compiler_params=_cparams("parallel"),
        name="out_proj_ln",
    )(x2, gate, lng, lnb, o_nsa, o_conv, w)


def _ffn1_kernel(x_ref, sc_ref, sh_ref, wg_ref, wu_ref, o_ref, *, tn):
    h = (x_ref[...] * (1.0 + sc_ref[0]) + sh_ref[0]).astype(BF16)
    for j in range(o_ref.shape[1] // tn):
        cols = slice(j * tn, (j + 1) * tn)
        gate = _dot(h, wg_ref[:, cols])
        up = _dot(h, wu_ref[:, cols])
        o_ref[:, cols] = (jax.nn.silu(gate) * up).astype(BF16)


def _ffn1(x2, scale, shift, wg, wu, seq, tm):
    n, d = x2.shape
    dff = wg.shape[1]
    per_b = seq // tm
    row = lambda i: (i, 0)
    mod = lambda i: (i // per_b, 0, 0)
    const = lambda i: (0, 0)
    return pl.pallas_call(
        functools.partial(_ffn1_kernel, tn=256),
        out_shape=jax.ShapeDtypeStruct((n, dff), BF16),
        grid=(n // tm,),
        in_specs=[pl.BlockSpec((tm, d), row), pl.BlockSpec((1, 1, d), mod), pl.BlockSpec((1, 1, d), mod),
                  pl.BlockSpec((d, dff), const), pl.BlockSpec((d, dff), const)],
        out_specs=pl.BlockSpec((tm, dff), row),
        compiler_params=_cparams("parallel"),
        name="ffn_up",
    )(x2, scale, shift, wg, wu)


def _ffn2_kernel(x_ref, gate_ref, g_ref, b_ref, a_ref, w_ref, o_ref, *, alpha):
    y = _dot(a_ref[...], w_ref[...])
    o_ref[...] = _res_ln(x_ref[...], gate_ref[0], y, g_ref[...], b_ref[...], alpha)


def _ffn2(x2, gate, lng, lnb, act, w, seq, tm, alpha):
    n, d = x2.shape
    per_b = seq // tm
    row = lambda i: (i, 0)
    const = lambda i: (0, 0)
    return pl.pallas_call(
        functools.partial(_ffn2_kernel, alpha=alpha),
        out_shape=jax.ShapeDtypeStruct((n, d), F32),
        grid=(n // tm,),
        in_specs=[pl.BlockSpec((tm, d), row), pl.BlockSpec((1, 1, d), lambda i: (i // per_b, 0, 0)),
                  pl.BlockSpec((1, d), const), pl.BlockSpec((1, d), const),
                  pl.BlockSpec((tm, act.shape[1]), row), pl.BlockSpec(w.shape, const)],
        out_specs=pl.BlockSpec((tm, d), row),
        compiler_params=_cparams("parallel"),
        name="ffn_down_ln",
    )(x2, gate, lng, lnb, act, w)


MOE_TM = 1024
MOE_FC = 512
GATHER_ROWS = 1024


def _router_kernel(x_ref, sc_ref, sh_ref, rw_ref, h_ref, ti_ref, tw_ref):
    hf = x_ref[...] * (1.0 + sc_ref[0]) + sh_ref[0]
    h_ref[...] = hf
    lane = lax.broadcasted_iota(jnp.int32, (hf.shape[0], LANE), 1)
    logits = jnp.dot(hf, rw_ref[...], precision=HIGHEST, preferred_element_type=F32)
    logits = jnp.where(lane < N_EXPERTS, logits, -jnp.inf)
    m1 = jnp.max(logits, axis=-1, keepdims=True)
    i1 = jnp.min(jnp.where(logits == m1, lane, LANE), axis=-1, keepdims=True)
    rest = jnp.where(lane == i1, -jnp.inf, logits)
    m2 = jnp.max(rest, axis=-1, keepdims=True)
    i2 = jnp.min(jnp.where(rest == m2, lane, LANE), axis=-1, keepdims=True)
    e2 = jnp.exp(m2 - m1)
    ti_ref[...] = jnp.where(lane == 0, i1, jnp.where(lane == 1, i2, 0))
    tw_ref[...] = jnp.where(lane == 0, 1.0 / (1.0 + e2), jnp.where(lane == 1, e2 / (1.0 + e2), 0.0))


def _router(x2, scale, shift, router_p, seq, tm):
    n, d = x2.shape
    per_b = seq // tm
    row = lambda i: (i, 0)
    mod = lambda i: (i // per_b, 0, 0)
    return pl.pallas_call(
        _router_kernel,
        out_shape=(jax.ShapeDtypeStruct((n, d), F32), jax.ShapeDtypeStruct((n, LANE), jnp.int32),
                   jax.ShapeDtypeStruct((n, LANE), F32)),
        grid=(n // tm,),
        in_specs=[pl.BlockSpec((tm, d), row), pl.BlockSpec((1, 1, d), mod), pl.BlockSpec((1, 1, d), mod),
                  pl.BlockSpec(router_p.shape, lambda i: (0, 0))],
        out_specs=(pl.BlockSpec((tm, d), row), pl.BlockSpec((tm, LANE), row), pl.BlockSpec((tm, LANE), row)),
        compiler_params=_cparams("parallel"),
        name="moe_router",
    )(x2, scale, shift, router_p)


def _gather_rows_kernel(idx_ref, src_ref, o_ref, sem):
    rows = o_ref.shape[0]
    base = pl.program_id(0) * rows

    def issue(r, carry):
        pltpu.make_async_copy(src_ref.at[pl.ds(idx_ref[base + r], 1)], o_ref.at[pl.ds(r, 1)], sem).start()
        return carry

    lax.fori_loop(0, rows, issue, 0, unroll=8)
    pltpu.make_async_copy(src_ref.at[pl.ds(0, rows)], o_ref, sem).wait()


def _gather_rows(idx, src):
    n_out = idx.shape[0]
    d = src.shape[1]
    return pl.pallas_call(
        _gather_rows_kernel,
        out_shape=jax.ShapeDtypeStruct((n_out, d), src.dtype),
        grid_spec=pltpu.PrefetchScalarGridSpec(
            num_scalar_prefetch=1,
            grid=(n_out // GATHER_ROWS,),
            in_specs=[pl.BlockSpec(memory_space=pl.ANY)],
            out_specs=pl.BlockSpec((GATHER_ROWS, d), lambda i, idx_ref: (i, 0)),
            scratch_shapes=[pltpu.SemaphoreType.DMA]),
        compiler_params=_cparams("arbitrary"),
        name="row_gather",
    )(idx, src)


def _expert_up_kernel(te_ref, nu_ref, x_ref, wg_ref, wu_ref, o_ref, xb_sc):
    i = pl.program_id(0)
    f = pl.program_id(1)

    @pl.when(i < nu_ref[0])
    def _compute():
        @pl.when(f == 0)
        def _cast():
            xb_sc[...] = x_ref[...].astype(BF16)
        h = xb_sc[...]
        gate = _dot(h, wg_ref[0, 0])
        up = _dot(h, wu_ref[0, 0])
        o_ref[...] = (jax.nn.silu(gate) * up).astype(BF16)

    @pl.when(i >= nu_ref[0])
    def _unused_tile():
        o_ref[...] = jnp.zeros_like(o_ref)


def _expert_up(tile_expert, n_used, xg, w1b, layer):
    p_rows, d = xg.shape
    nf = w1b.shape[3] // (2 * MOE_FC)
    return pl.pallas_call(
        _expert_up_kernel,
        out_shape=jax.ShapeDtypeStruct((p_rows, nf * MOE_FC), BF16),
        grid_spec=pltpu.PrefetchScalarGridSpec(
            num_scalar_prefetch=2,
            grid=(p_rows // MOE_TM, nf),
            in_specs=[pl.BlockSpec((MOE_TM, d), lambda i, f, te, nu: (i, 0)),
                      pl.BlockSpec((1, 1, d, MOE_FC), lambda i, f, te, nu: (layer, te[i], 0, f)),
                      pl.BlockSpec((1, 1, d, MOE_FC), lambda i, f, te, nu: (layer, te[i], 0, nf + f))],
            out_specs=pl.BlockSpec((MOE_TM, MOE_FC), lambda i, f, te, nu: (i, f)),
            scratch_shapes=[pltpu.VMEM((MOE_TM, d), BF16)]),
        compiler_params=_cparams("parallel", "arbitrary"),
        name="moe_up",
    )(tile_expert, n_used, xg, w1b, w1b)


def _expert_down_kernel(te_ref, nu_ref, a_ref, w_ref, o_ref):
    i = pl.program_id(0)

    @pl.when(i < nu_ref[0])
    def _compute():
        o_ref[...] = _dot(a_ref[...], w_ref[0, 0])

    @pl.when(i >= nu_ref[0])
    def _unused_tile():
        o_ref[...] = jnp.zeros_like(o_ref)


def _expert_down(tile_expert, n_used, act, w2b, layer):
    p_rows, dffe = act.shape
    d = w2b.shape[3]
    return pl.pallas_call(
        _expert_down_kernel,
        out_shape=jax.ShapeDtypeStruct((p_rows, d), F32),
        grid_spec=pltpu.PrefetchScalarGridSpec(
            num_scalar_prefetch=2,
            grid=(p_rows // MOE_TM,),
            in_specs=[pl.BlockSpec((MOE_TM, dffe), lambda i, te, nu: (i, 0)),
                      pl.BlockSpec((1, 1, dffe, d), lambda i, te, nu: (layer, te[i], 0, 0))],
            out_specs=pl.BlockSpec((MOE_TM, d), lambda i, te, nu: (i, 0))),
        compiler_params=_cparams("parallel"),
        name="moe_down",
    )(tile_expert, n_used, act, w2b)


def _moe_combine_kernel(x_ref, gate_ref, g_ref, b_ref, tw_ref, y1_ref, y2_ref, o_ref, *, alpha):
    tw = tw_ref[...]
    y = tw[:, 0:1] * y1_ref[...] + tw[:, 1:2] * y2_ref[...]
    o_ref[...] = _res_ln(x_ref[...], gate_ref[0], y, g_ref[...], b_ref[...], alpha)


def _moe_combine(x2, gate, lng, lnb, tw, yg, seq, tm, alpha):
    n, d = x2.shape
    per_b = seq // tm
    nt = n // tm
    row = lambda i: (i, 0)
    const = lambda i: (0, 0)
    return pl.pallas_call(
        functools.partial(_moe_combine_kernel, alpha=alpha),
        out_shape=jax.ShapeDtypeStruct((n, d), F32),
        grid=(nt,),
        in_specs=[pl.BlockSpec((tm, d), row), pl.BlockSpec((1, 1, d), lambda i: (i // per_b, 0, 0)),
                  pl.BlockSpec((1, d), const), pl.BlockSpec((1, d), const), pl.BlockSpec((tm, LANE), row),
                  pl.BlockSpec((tm, d), row), pl.BlockSpec((tm, d), lambda i: (nt + i, 0))],
        out_specs=pl.BlockSpec((tm, d), row),
        compiler_params=_cparams("parallel"),
        name="moe_combine_ln",
    )(x2, gate, lng, lnb, tw, yg, yg)


def _route_tables(ti, n_rows):
    n = ti.shape[0]
    e_flat = jnp.concatenate([ti[:, 0], ti[:, 1]])
    onehot = (e_flat[:, None] == jnp.arange(N_EXPERTS, dtype=jnp.int32)[None, :]).astype(jnp.int32)
    cum = jnp.cumsum(onehot, axis=0)
    counts = cum[-1]
    padded = (counts + MOE_TM - 1) // MOE_TM * MOE_TM
    ends = jnp.cumsum(padded)
    pos = jnp.sum(onehot * (cum - 1 + (ends - padded)[None, :]), axis=1)
    tok = jnp.tile(jnp.arange(n, dtype=jnp.int32), 2)
    src = (jnp.arange(n_rows, dtype=jnp.int32) % n).at[pos].set(tok)
    tile_start = jnp.arange(n_rows // MOE_TM, dtype=jnp.int32) * MOE_TM
    tile_expert = jnp.minimum(jnp.sum((tile_start[:, None] >= ends[None, :]).astype(jnp.int32), axis=1),
                              N_EXPERTS - 1)
    return pos.astype(jnp.int32), src, tile_expert, (ends[-1:] // MOE_TM).astype(jnp.int32)


def _moe(x2, scale, shift, gate, lng, lnb, router_p, w1b, w2b, layer, seq, tm, alpha):
    n, d = x2.shape
    n_rows = 2 * n + N_EXPERTS * MOE_TM
    assert n_rows % GATHER_ROWS == 0 and (2 * n) % GATHER_ROWS == 0
    h, ti, tw = _router(x2, scale, shift, router_p, seq, tm)
    pos, src, tile_expert, n_used = _route_tables(ti, n_rows)
    xg = _gather_rows(src, h)
    act = _expert_up(tile_expert, n_used, xg, w1b, layer)
    y = _expert_down(tile_expert, n_used, act, w2b, layer)
    yg = _gather_rows(pos, y)
    return _moe_combine(x2, gate, lng, lnb, tw, yg, seq, tm, alpha)


def _rel_bucket_np(dist):
    n = np.maximum(dist, 0)
    max_exact = NUM_BUCKETS // 2
    nf = np.maximum(n, 1).astype(np.float32)
    large = max_exact + (np.log(nf / np.float32(max_exact)) / np.float32(math.log(MAX_DISTANCE / max_exact))
                         * np.float32(NUM_BUCKETS - max_exact)).astype(np.int32)
    large = np.minimum(large, NUM_BUCKETS - 1)
    return np.where(n < max_exact, n, large).astype(np.int32)


def _bias_lookup(rel_bias, dist, valid):
    onehot = _rel_bucket_np(dist)[..., None] == np.arange(NUM_BUCKETS)
    vals = jnp.einsum('...k,kh->...h', jnp.asarray(onehot).astype(F32), rel_bias, precision=HIGHEST)
    return jnp.where(jnp.asarray(valid)[..., None], vals * LOG2E, NEG_INF)


def _key_major(vals, add_masked_tile=False):
    n_tiles, tq, tk, _ = vals.shape
    if add_masked_tile:
        vals = jnp.concatenate([vals, jnp.full((1,) + vals.shape[1:], NEG_INF, vals.dtype)], axis=0)
        n_tiles += 1
    vals = vals.reshape(n_tiles, tq, tk, N_KV, Q_PER_KV).transpose(3, 0, 2, 4, 1)
    return vals.reshape(N_KV, n_tiles, tk, Q_PER_KV * tq).astype(F32)


def _band_dist(n_delta):
    i = np.arange(ATT_TQ)[None, :, None]
    k = np.arange(ATT_TQ)[None, None, :]
    return ATT_TQ * np.arange(n_delta)[:, None, None] + i - k


def _cmp_bias_table(rel_bias, n_qt, n_cmp_pad):
    per_tile = ATT_TQ // CMP_STRIDE
    shift = per_tile * (n_qt - 1)
    rel_n = np.arange(n_cmp_pad + shift) - shift
    dist = np.arange(ATT_TQ)[:, None] - (CMP_STRIDE * rel_n[None, :] + L_CMP - 1)
    base = _bias_lookup(rel_bias, dist, dist >= 0)
    tiles = [base[:, shift - per_tile * qt: shift - per_tile * qt + n_cmp_pad] for qt in range(n_qt)]
    return _key_major(jnp.stack(tiles))


def _inproj_weight(w_in):
    depth, d, _ = w_in.shape
    d_nsa = N_HEADS * HEAD_DIM
    d_kv = N_KV * HEAD_DIM
    base_ks = d_nsa + 2 * d_kv
    base_gl = d_nsa + 6 * d_kv
    base_cv = base_gl + N_HEADS * N_BRANCH
    wq = jnp.pad(w_in[:, :, :d_nsa].reshape(depth, d, N_HEADS, HEAD_DIM),
                 ((0, 0), (0, 0), (0, 0), (0, LANE - HEAD_DIM))).reshape(depth, d, Q_COLS)
    parts = [wq, w_in[:, :, d_nsa:base_ks]]
    for pair in range(2):
        kbase = base_ks + pair * 2 * d_kv
        for g in range(N_KV):
            parts.append(w_in[:, :, kbase + g * HEAD_DIM: kbase + (g + 1) * HEAD_DIM])
            parts.append(w_in[:, :, kbase + d_kv + g * HEAD_DIM: kbase + d_kv + (g + 1) * HEAD_DIM])
    parts.append(w_in[:, :, base_cv:])
    gl = w_in[:, :, base_gl:base_cv].reshape(depth, d, N_KV, Q_PER_KV, N_BRANCH).transpose(0, 2, 4, 3, 1)
    gl = jnp.pad(gl.reshape(depth, N_KV, N_BRANCH * Q_PER_KV, d),
                 ((0, 0), (0, 0), (0, LANE - N_BRANCH * Q_PER_KV), (0, 0)))
    return jnp.concatenate(parts, axis=-1).astype(BF16), gl.reshape(depth, GL_ROWS, d).astype(BF16)


def kernel(x, c, w_in, cmp_pos, cmp_w1, cmp_w2, conv_w, conv_b, conv_ln_g, conv_ln_b, w_out, rel_bias,
           ada_w, ada_b, ln_g, ln_b, ffn_w1, ffn_w2, router_w, moe_w1, moe_w2):
    bsz, seq, d = x.shape
    depth = w_in.shape[0]
    n = bsz * seq
    alpha = (2 * depth) ** 0.25
    n_blocks = seq // L_SEL
    n_sel = min(N_SEL, n_blocks)
    n_cmp = (seq - L_CMP) // CMP_STRIDE + 1
    nrow = seq // CMP_STRIDE
    assert d == 1024 and seq % 512 == 0 and nrow == LANE and n_cmp == nrow - 1
    tm = 512

    n_qt = seq // ATT_TQ
    d_win = _band_dist(WINDOW // ATT_TQ + 1)
    win_tab = _key_major(_bias_lookup(rel_bias, d_win, (d_win >= 0) & (d_win < WINDOW)), add_masked_tile=True)
    d_sel = _band_dist(3)
    assert _rel_bucket_np(d_sel[2]).min() == NUM_BUCKETS - 1
    sel_tab = _key_major(_bias_lookup(rel_bias, d_sel, d_sel >= 0), add_masked_tile=True)
    cmp_tab = _cmp_bias_table(rel_bias, n_qt, nrow)

    w_in_p, w_gl_t = _inproj_weight(w_in)
    eye2 = jnp.eye(2, dtype=F32)
    w1r = cmp_w1.reshape(depth, 2, L_CMP, HEAD_DIM, HEAD_DIM)
    w1x = jnp.einsum('zslde,st,gh->zlsgdhte', w1r, eye2, eye2).reshape(depth, L_CMP, 4 * HEAD_DIM, 4 * HEAD_DIM)
    half = L_CMP // 2
    w_lo = w1x[:, :half].reshape(depth, half * 4 * HEAD_DIM, 4 * HEAD_DIM).astype(BF16)
    w_hi = w1x[:, half:].reshape(depth, half * 4 * HEAD_DIM, 4 * HEAD_DIM).astype(BF16)
    w2bd = jnp.einsum('zsde,st,gh->zgsdhte', cmp_w2, eye2, eye2).reshape(depth, 4 * HEAD_DIM, 4 * HEAD_DIM).astype(BF16)
    posx = jnp.broadcast_to(cmp_pos[:, :, :, None, :], (depth, 2, L_CMP, N_KV, HEAD_DIM)).transpose(0, 2, 1, 3, 4)
    pos_lo = posx[:, :half].reshape(depth, 1, half * 4 * HEAD_DIM)
    pos_hi = posx[:, half:].reshape(depth, 1, half * 4 * HEAD_DIM)
    conv_wp = jnp.broadcast_to(conv_w[:, :, None, :], conv_w.shape[:2] + (SUBLANE, conv_w.shape[2]))
    w_out_b = w_out.astype(BF16)
    dff = ffn_w2.shape[1]
    ffn_wg = ffn_w1[:, :, :dff].astype(BF16)
    ffn_wu = ffn_w1[:, :, dff:].astype(BF16)
    ffn_w2b = ffn_w2.astype(BF16)
    router_p = jnp.pad(router_w, ((0, 0), (0, 0), (0, LANE - N_EXPERTS)))
    moe_w1b = moe_w1.astype(BF16)
    moe_w2b = moe_w2.astype(BF16)

    mod = _ada(c, ada_w, ada_b).reshape(depth, 6, bsz, 1, d)
    x2 = x.reshape(n, d)
    for l in range(depth):
        shift1, scale1, gate1, shift2, scale2, gate2 = [mod[l, k] for k in range(6)]
        q, kvc, kvsw, gl3, cv = _inproj(x2, scale1, shift1, w_in_p[l], w_gl_t[l], seq, tm)
        q3 = q.reshape(bsz, seq, Q_COLS)
        kvsw3 = kvsw.reshape(bsz, seq, KVSW_COLS)
        kvsw_t = kvsw3.reshape(bsz, n_qt, ATT_TQ, 2 * N_KV, LANE).transpose(0, 3, 1, 4, 2)
        kvcmp, kvcmp_t = _compress(kvc.reshape(bsz, nrow, CMP_STRIDE * 256), pos_lo[l], pos_hi[l], w_lo[l], w_hi[l],
                                   w2bd[l])
        o_nsa = _nsa_attn(q3, kvcmp, kvcmp_t, cmp_tab, kvsw3, kvsw_t, sel_tab, win_tab, gl3, n_blocks, n_sel)
        o_conv = _conv_branch(cv.reshape(bsz, seq, 1024), conv_wp[l], conv_b[l][None], conv_ln_g[l][None],
                              conv_ln_b[l][None])
        x2 = _outproj(x2, gate1, ln_g[l, 0][None], ln_b[l, 0][None], o_nsa.reshape(n, 512), o_conv.reshape(n, 512),
                      w_out_b[l], seq, tm, alpha)
        if l % 2 == 0:
            act = _ffn1(x2, scale2, shift2, ffn_wg[l // 2], ffn_wu[l // 2], seq, tm)
            x2 = _ffn2(x2, gate2, ln_g[l, 1][None], ln_b[l, 1][None], act, ffn_w2b[l // 2], seq, tm, alpha)
        else:
            x2 = _moe(x2, scale2, shift2, gate2, ln_g[l, 1][None], ln_b[l, 1][None], router_p[l // 2],
                      moe_w1b, moe_w2b, l // 2, seq, tm, alpha)
    return x2.reshape(bsz, seq, d)
```

```python
import functools
import math

import numpy as np
import jax
import jax.numpy as jnp
from jax import lax
from jax.experimental import pallas as pl
from jax.experimental.pallas import tpu as pltpu

F32 = jnp.float32
BF16 = jnp.bfloat16
HIGHEST = lax.Precision.HIGHEST

HEAD_DIM = 64
N_HEADS = 8
N_KV = 2
Q_PER_KV = N_HEADS // N_KV
N_BRANCH = 3
L_CMP = 32
CMP_STRIDE = 16
L_SEL = 64
N_SEL = 8
WINDOW = 512
CONV_WIDTH = 31
NUM_BUCKETS = 32
MAX_DISTANCE = 128
N_EXPERTS = 8
LN_EPS = 1e-5
NEG_INF = -1e30
LOG2E = math.log2(math.e)

LANE = 128
SUBLANE = 8
ATT_TQ = 128
ATT_ROWS = Q_PER_KV * ATT_TQ
CONV_TQ = 256
CONV_HALO = 32
CONV_RC = 64
VMEM_LIMIT_BYTES = 52 * 1024 * 1024


def _cparams(*sem):
    return pltpu.CompilerParams(dimension_semantics=sem, vmem_limit_bytes=VMEM_LIMIT_BYTES)


def _dot(a, b):
    return jnp.dot(a, b, preferred_element_type=F32)


def _dot_nt(a, b, precision=None):
    return lax.dot_general(a, b, (((1,), (1,)), ((), ())), precision=precision,
                           preferred_element_type=F32)


def _res_ln(x, gate, y, g, b, alpha):
    z = alpha * x + (1.0 + gate) * y
    mu = jnp.mean(z, axis=-1, keepdims=True)
    zc = z - mu
    var = jnp.mean(zc * zc, axis=-1, keepdims=True)
    return zc * lax.rsqrt(var + LN_EPS) * g + b


def _ada_kernel(c_ref, w_ref, b_ref, o_ref):
    sc = jax.nn.silu(c_ref[...])
    o_ref[0, 0] = jnp.dot(sc, w_ref[0], precision=HIGHEST, preferred_element_type=F32) + b_ref[0]


def _ada(c, ada_w, ada_b):
    depth, d, _ = ada_w.shape
    bsz = c.shape[0]
    return pl.pallas_call(
        _ada_kernel,
        out_shape=jax.ShapeDtypeStruct((depth, 6, bsz, d), F32),
        grid=(depth, 6),
        in_specs=[pl.BlockSpec((bsz, d), lambda l, k: (0, 0)),
                  pl.BlockSpec((1, d, d), lambda l, k: (l, 0, k)),
                  pl.BlockSpec((1, 1, d), lambda l, k: (l, 0, k))],
        out_specs=pl.BlockSpec((1, 1, bsz, d), lambda l, k: (l, k, 0, 0)),
        compiler_params=_cparams("parallel", "parallel"),
        name="ada_mod",
    )(c, ada_w, ada_b.reshape(depth, 1, 6 * d))


Q_COLS = N_HEADS * LANE
KVC_COLS = 4 * HEAD_DIM
KVSW_COLS = 8 * HEAD_DIM
GL_ROWS = N_KV * LANE


def _inproj_kernel(x_ref, sc_ref, sh_ref, w_ref, wgl_ref, q_ref, kvc_ref, kvsw_ref, glt_ref, cv_ref):
    h = (x_ref[...] * (1.0 + sc_ref[0]) + sh_ref[0]).astype(BF16)
    c0, c1, c2 = Q_COLS, Q_COLS + KVC_COLS, Q_COLS + KVC_COLS + KVSW_COLS
    q_ref[...] = (_dot(h, w_ref[:, 0:c0]) * (HEAD_DIM ** -0.5 * LOG2E)).astype(BF16)
    kvc_ref[...] = _dot(h, w_ref[:, c0:c1])
    kvsw_ref[...] = _dot(h, w_ref[:, c1:c2]).astype(BF16)
    cv_ref[...] = _dot(h, w_ref[:, c2:])
    glt_ref[...] = _dot_nt(wgl_ref[...], h)


def _inproj(x2, scale, shift, w, wgl_t, seq, tm):
    n, d = x2.shape
    per_b = seq // tm
    d_cv = w.shape[1] - (Q_COLS + KVC_COLS + KVSW_COLS)
    row = lambda i: (i, 0)
    mod = lambda i: (i // per_b, 0, 0)
    const = lambda i: (0, 0)
    return pl.pallas_call(
        _inproj_kernel,
        out_shape=(jax.ShapeDtypeStruct((n, Q_COLS), BF16), jax.ShapeDtypeStruct((n, KVC_COLS), F32),
                   jax.ShapeDtypeStruct((n, KVSW_COLS), BF16), jax.ShapeDtypeStruct((GL_ROWS, n), F32),
                   jax.ShapeDtypeStruct((n, d_cv), F32)),
        grid=(n // tm,),
        in_specs=[pl.BlockSpec((tm, d), row), pl.BlockSpec((1, 1, d), mod), pl.BlockSpec((1, 1, d), mod),
                  pl.BlockSpec(w.shape, const), pl.BlockSpec(wgl_t.shape, const)],
        out_specs=(pl.BlockSpec((tm, Q_COLS), row), pl.BlockSpec((tm, KVC_COLS), row),
                   pl.BlockSpec((tm, KVSW_COLS), row), pl.BlockSpec((GL_ROWS, tm), lambda i: (0, i)),
                   pl.BlockSpec((tm, d_cv), row)),
        compiler_params=_cparams("parallel"),
        name="in_proj",
    )(x2, scale, shift, w, wgl_t)


def _compress_kernel(ak_ref, av_ref, plo_ref, phi_ref, wlo_ref, whi_ref, w2_ref, o_ref, ot_ref):
    nrow = o_ref.shape[1]
    lo = jnp.zeros((nrow, o_ref.shape[2]), F32)
    hi = jnp.zeros((nrow, o_ref.shape[2]), F32)
    for l in range(CMP_STRIDE):
        for part, a_ref in enumerate((ak_ref, av_ref)):
            lanes = slice(part * LANE, (part + 1) * LANE)
            tok = a_ref[0, pl.ds(l, nrow, stride=CMP_STRIDE), :]
            lo = lo + _dot((tok + plo_ref[l:l + 1, lanes]).astype(BF16), wlo_ref[l, lanes, :])
            hi = hi + _dot((tok + phi_ref[l:l + 1, lanes]).astype(BF16), whi_ref[l, lanes, :])
    pre = lo + pltpu.roll(hi, nrow - 1, 0)
    act = jax.nn.gelu(pre, approximate=True)
    out = _dot(act.astype(BF16), w2_ref[...])
    o_ref[0] = out.astype(BF16)
    ot_ref[0] = out.T.astype(BF16)


def _compress(kvc3, pos_lo, pos_hi, wlo, whi, w2bd):
    bsz, seq, width = kvc3.shape
    nrow = seq // CMP_STRIDE
    const = lambda b: (0, 0)
    const3 = lambda b: (0, 0, 0)
    return pl.pallas_call(
        _compress_kernel,
        out_shape=(jax.ShapeDtypeStruct((bsz, nrow, 256), BF16), jax.ShapeDtypeStruct((bsz, 256, nrow), BF16)),
        grid=(bsz,),
        in_specs=[pl.BlockSpec((1, seq, LANE), lambda b: (b, 0, 0)), pl.BlockSpec((1, seq, LANE), lambda b: (b, 0, 1)),
                  pl.BlockSpec(pos_lo.shape, const), pl.BlockSpec(pos_hi.shape, const),
                  pl.BlockSpec(wlo.shape, const3), pl.BlockSpec(whi.shape, const3),
                  pl.BlockSpec((256, 256), const)],
        out_specs=(pl.BlockSpec((1, nrow, 256), lambda b: (b, 0, 0)),
                   pl.BlockSpec((1, 256, nrow), lambda b: (b, 0, 0))),
        compiler_params=_cparams("parallel"),
        name="compress_mlp",
    )(kvc3, kvc3, pos_lo, pos_hi, wlo, whi, w2bd)


def _stack_heads(q):
    return jnp.concatenate([q[:, r * LANE:(r + 1) * LANE] for r in range(Q_PER_KV)], axis=0)


def _branch_gates(gl_t, branch):
    return jnp.concatenate([jax.nn.sigmoid(gl_t[branch * Q_PER_KV + r:branch * Q_PER_KV + r + 1, :])
                            for r in range(Q_PER_KV)], axis=1)


def _heads_to_rows(o_t):
    tq = o_t.shape[1] // Q_PER_KV
    og = o_t.astype(BF16)
    ii = lax.broadcasted_iota(jnp.int32, (tq, tq), 0)
    kk = lax.broadcasted_iota(jnp.int32, (tq, tq), 1)
    eye = jnp.where(ii == kk, 1.0, 0.0).astype(BF16)
    halves = []
    for pair in range(Q_PER_KV // 2):
        two_heads = jnp.concatenate([og[:, (2 * pair) * tq:(2 * pair + 1) * tq],
                                     og[:, (2 * pair + 1) * tq:(2 * pair + 2) * tq]], axis=0)
        halves.append(_dot_nt(eye, two_heads))
    return jnp.concatenate(halves, axis=1).astype(BF16)


def _cmp_section(qs, qt, kv_ref, kvt_ref, tab_ref, gl_ref, sel_sc, *, n_blocks, n_sel):
    tq = ATT_TQ
    outs = []
    jj = lax.broadcasted_iota(jnp.int32, (n_blocks, LANE), 0)
    nn = lax.broadcasted_iota(jnp.int32, (n_blocks, LANE), 1)
    overlap_t = jnp.where((CMP_STRIDE * nn < L_SEL * jj + L_SEL) & (CMP_STRIDE * nn + L_CMP > L_SEL * jj),
                          1.0, 0.0).astype(F32)
    jb = lax.broadcasted_iota(jnp.int32, (n_blocks, tq), 0)
    tb = (qt * tq + lax.broadcasted_iota(jnp.int32, (n_blocks, tq), 1)) // L_SEL
    for g in range(N_KV):
        bias = tab_ref[g, 0]
        s = _dot_nt(kv_ref[0, :, g * LANE:(g + 1) * LANE], qs[g]) + bias
        valid = bias > 0.5 * NEG_INF
        m = jnp.max(s, axis=0, keepdims=True)
        p = jnp.where(valid, jnp.exp2(s - m), 0.0)
        l = jnp.sum(p, axis=0, keepdims=True)
        p = p * (1.0 / jnp.where(l > 0.0, l, 1.0))
        o_t = _dot(kvt_ref[0, g * LANE + HEAD_DIM:(g + 1) * LANE, :], p.astype(BF16))
        outs.append(o_t * _branch_gates(gl_ref[g * LANE:(g + 1) * LANE, :], 0))

        psum = p[:, 0:tq] + p[:, tq:2 * tq] + p[:, 2 * tq:3 * tq] + p[:, 3 * tq:4 * tq]
        imp_t = jnp.dot(overlap_t, psum, precision=HIGHEST, preferred_element_type=F32)
        score = jnp.where(jb > tb, -jnp.inf,
                          jnp.where((jb == 0) | (jb == tb) | (jb == tb - 1), jnp.inf, imp_t))
        rank = jnp.zeros((n_blocks, tq), F32)
        for jp in range(n_blocks):
            row = score[jp:jp + 1, :]
            earlier = jnp.where(jb > jp, 1.0, 0.0)
            rank = rank + jnp.where(row > score, 1.0, 0.0) + jnp.where(row == score, earlier, 0.0)
        madd = jnp.where(rank < float(n_sel), 0.0, NEG_INF)
        per_chunk = ATT_TQ // L_SEL
        for c in range(n_blocks // per_chunk):
            sel_sc[g, c, 0:per_chunk, :] = madd[c * per_chunk:(c + 1) * per_chunk, :]
    return outs


def _band_section(qs, qt, kv_ref, kvt_ref, tab_ref, gl_ref, sel_sc, s_sc, *, span, group, branch):
    tq = ATT_TQ
    n_delta = tab_ref.shape[1] - 1
    n_chunks = kv_ref.shape[1] // tq
    c_base = qt - span if span is not None else 0
    use_sel = sel_sc is not None

    def chunk_ids(gi, u):
        c = c_base + gi * group + u
        inside = (c >= 0) & (c <= qt)
        dd = jnp.where(inside, jnp.minimum(qt - c, n_delta - 1), n_delta)
        return jnp.clip(c, 0, n_chunks - 1), dd

    def rows(gi, u):
        return pl.ds(pl.multiple_of((gi * group + u) * tq, tq), tq)

    def scores(gi, ms):
        ms = list(ms)
        for u in range(group):
            cc, dd = chunk_ids(gi, u)
            kv = kv_ref[0, pl.ds(pl.multiple_of(cc * tq, tq), tq), :]
            for g in range(N_KV):
                s = _dot_nt(kv[:, g * LANE:(g + 1) * LANE], qs[g]) + tab_ref[g, dd]
                if use_sel:
                    blocks = sel_sc[g, cc, 0:tq // L_SEL, :]
                    madd = jnp.concatenate([jnp.broadcast_to(blocks[j:j + 1, :], (L_SEL, tq))
                                            for j in range(tq // L_SEL)], axis=0)
                    s = s + jnp.concatenate([madd] * Q_PER_KV, axis=1)
                s_sc[g, rows(gi, u), :] = s
                ms[g] = jnp.maximum(ms[g], jnp.max(s, axis=0, keepdims=True))
        return tuple(ms)

    m0 = tuple(jnp.full((1, ATT_ROWS), NEG_INF, F32) for _ in range(N_KV))
    if span is not None:
        n_groups = 1
        m = scores(0, m0)
    else:
        n_groups = qt // group + 1
        m = lax.fori_loop(0, n_groups, scores, m0)

    def values(gi, carry):
        carry = [list(c) for c in carry]
        for u in range(group):
            cc, _ = chunk_ids(gi, u)
            for g in range(N_KV):
                l, acc = carry[g]
                p = jnp.exp2(s_sc[g, rows(gi, u), :] - m[g])
                l = l + jnp.sum(p, axis=0, keepdims=True)
                acc = acc + _dot(kvt_ref[0, g, cc, HEAD_DIM:, :], p.astype(BF16))
                carry[g] = [l, acc]
        return tuple(tuple(c) for c in carry)

    init = tuple((jnp.zeros((1, ATT_ROWS), F32), jnp.zeros((HEAD_DIM, ATT_ROWS), F32)) for _ in range(N_KV))
    if span is not None:
        out = values(0, init)
    else:
        out = lax.fori_loop(0, n_groups, values, init)
    return [out[g][1] * (_branch_gates(gl_ref[g * LANE:(g + 1) * LANE, :], branch) / out[g][0]) for g in range(N_KV)]


SEL_GROUP = 4
WIN_SPAN = WINDOW // ATT_TQ


def _nsa_kernel(q_ref, kvc_ref, kvct_ref, ctab_ref, kvs_ref, kvst_ref, stab_ref, kvw_ref, kvwt_ref, wtab_ref,
                gl_ref, o_ref, sel_sc, ssel_sc, swin_sc, *, n_blocks, n_sel):
    qt = pl.program_id(1)
    hw = Q_PER_KV * LANE
    qs = [_stack_heads(q_ref[0, :, g * hw:(g + 1) * hw]) for g in range(N_KV)]
    o_cmp = _cmp_section(qs, qt, kvc_ref, kvct_ref, ctab_ref, gl_ref, sel_sc, n_blocks=n_blocks, n_sel=n_sel)
    o_sel = _band_section(qs, qt, kvs_ref, kvst_ref, stab_ref, gl_ref, sel_sc, ssel_sc,
                          span=None, group=SEL_GROUP, branch=1)
    o_win = _band_section(qs, qt, kvw_ref, kvwt_ref, wtab_ref, gl_ref, None, swin_sc,
                          span=WIN_SPAN, group=WIN_SPAN + 1, branch=2)
    for g in range(N_KV):
        o_ref[0, :, g * 256:(g + 1) * 256] = _heads_to_rows(o_cmp[g] + o_sel[g] + o_win[g])


def _nsa_attn(q3, kvcmp, kvcmp_t, cmp_tab, kvsw3, kvsw_t, sel_tab, win_tab, gl_t, n_blocks, n_sel):
    bsz, seq, _ = q3.shape
    nqt = seq // ATT_TQ
    n_cmp = kvcmp.shape[1]
    sel_slots = -(-nqt // SEL_GROUP) * SEL_GROUP
    full4 = lambda b, t: (0, 0, 0, 0)
    kv_spec = lambda pair: pl.BlockSpec((1, seq, N_KV * LANE), lambda b, t: (b, 0, pair))
    kvt_spec = lambda pair: pl.BlockSpec((1, N_KV) + kvsw_t.shape[2:], lambda b, t: (b, pair, 0, 0, 0))
    return pl.pallas_call(
        functools.partial(_nsa_kernel, n_blocks=n_blocks, n_sel=n_sel),
        out_shape=jax.ShapeDtypeStruct((bsz, seq, 512), BF16),
        grid=(bsz, nqt),
        in_specs=[pl.BlockSpec((1, ATT_TQ, Q_COLS), lambda b, t: (b, t, 0)),
                  pl.BlockSpec((1, n_cmp, N_KV * LANE), lambda b, t: (b, 0, 0)),
                  pl.BlockSpec((1, N_KV * LANE, n_cmp), lambda b, t: (b, 0, 0)),
                  pl.BlockSpec((N_KV, 1, n_cmp, ATT_ROWS), lambda b, t: (0, t, 0, 0)),
                  kv_spec(0), kvt_spec(0), pl.BlockSpec(sel_tab.shape, full4),
                  kv_spec(1), kvt_spec(1), pl.BlockSpec(win_tab.shape, full4),
                  pl.BlockSpec((GL_ROWS, ATT_TQ), lambda b, t: (0, b * nqt + t))],
        out_specs=pl.BlockSpec((1, ATT_TQ, 512), lambda b, t: (b, t, 0)),
        scratch_shapes=[pltpu.VMEM((N_KV, n_blocks * L_SEL // ATT_TQ, SUBLANE, ATT_TQ), F32),
                        pltpu.VMEM((N_KV, sel_slots * ATT_TQ, ATT_ROWS), F32),
                        pltpu.VMEM((N_KV, (WIN_SPAN + 1) * ATT_TQ, ATT_ROWS), F32)],
        compiler_params=_cparams("parallel", "parallel"),
        name="nsa_attn",
    )(q3, kvcmp, kvcmp_t, cmp_tab, kvsw3, kvsw_t, sel_tab, kvsw3, kvsw_t, win_tab, gl_t)


def _conv_kernel(cur_ref, prev_ref, w_ref, cb_ref, g_ref, b_ref, o_ref, ybuf, ysh, cbuf):
    i = pl.program_id(1)
    dc = o_ref.shape[-1]
    cur = cur_ref[0]
    prev = prev_ref[0]
    ybuf[CONV_HALO:, :] = cur[:, :dc] * jax.nn.sigmoid(cur[:, dc:])
    halo = prev[:, :dc] * jax.nn.sigmoid(prev[:, dc:])
    ybuf[0:CONV_HALO, :] = jnp.where(i > 0, halo, 0.0)
    n_sh = ysh.shape[1]
    for b in range(1, SUBLANE):
        ysh[b - 1] = ybuf[b:b + n_sh, :]
    off = CONV_HALO - (CONV_WIDTH - 1)
    sub = CONV_RC // SUBLANE
    n_rc = CONV_TQ // CONV_RC
    for lc in range(dc // LANE):
        cols = slice(lc * LANE, (lc + 1) * LANE)
        accs = [jnp.zeros((sub, SUBLANE, LANE), F32) for _ in range(n_rc)]
        for k in range(CONV_WIDTH):
            b = (off + k) % SUBLANE
            wk = w_ref[k, :, cols][None]
            for rc in range(n_rc):
                r0 = rc * CONV_RC + (off + k) - b
                src = ybuf[r0:r0 + CONV_RC, cols] if b == 0 else ysh[b - 1, r0:r0 + CONV_RC, cols]
                accs[rc] = accs[rc] + wk * src.reshape(sub, SUBLANE, LANE)
        for rc in range(n_rc):
            cbuf[rc * CONV_RC:(rc + 1) * CONV_RC, cols] = accs[rc].reshape(CONV_RC, LANE)
    z = cbuf[...] + cb_ref[...]
    mu = jnp.mean(z, axis=-1, keepdims=True)
    zc = z - mu
    var = jnp.mean(zc * zc, axis=-1, keepdims=True)
    y = zc * lax.rsqrt(var + LN_EPS) * g_ref[...] + b_ref[...]
    o_ref[0] = jax.nn.silu(y).astype(BF16)


def _conv_branch(cv3, w, cb, g, b):
    bsz, seq, two_dc = cv3.shape
    dc = two_dc // 2
    per = CONV_TQ // CONV_HALO
    const = lambda bb, i: (0, 0)
    return pl.pallas_call(
        _conv_kernel,
        out_shape=jax.ShapeDtypeStruct((bsz, seq, dc), BF16),
        grid=(bsz, seq // CONV_TQ),
        in_specs=[pl.BlockSpec((1, CONV_TQ, two_dc), lambda bb, i: (bb, i, 0)),
                  pl.BlockSpec((1, CONV_HALO, two_dc), lambda bb, i: (bb, jnp.maximum(i * per - 1, 0), 0)),
                  pl.BlockSpec(w.shape, lambda bb, i: (0, 0, 0)), pl.BlockSpec((1, dc), const),
                  pl.BlockSpec((1, dc), const), pl.BlockSpec((1, dc), const)],
        out_specs=pl.BlockSpec((1, CONV_TQ, dc), lambda bb, i: (bb, i, 0)),
        scratch_shapes=[pltpu.VMEM((CONV_HALO + CONV_TQ, dc), F32),
                        pltpu.VMEM((SUBLANE - 1, CONV_HALO + CONV_TQ - SUBLANE, dc), F32),
                        pltpu.VMEM((CONV_TQ, dc), F32)],
        compiler_params=_cparams("parallel", "parallel"),
        name="conv_branch",
    )(cv3, cv3, w, cb, g, b)


def _outproj_kernel(x_ref, gate_ref, g_ref, b_ref, nsa_ref, ocv_ref, w_ref, o_ref, *, alpha):
    half = nsa_ref.shape[1]
    y = _dot(nsa_ref[...], w_ref[0:half, :]) + _dot(ocv_ref[...], w_ref[half:, :])
    o_ref[...] = _res_ln(x_ref[...], gate_ref[0], y, g_ref[...], b_ref[...], alpha)


def _outproj(x2, gate, lng, lnb, o_nsa, o_conv, w, seq, tm, alpha):
    n, d = x2.shape
    per_b = seq // tm
    row = lambda i: (i, 0)
    const = lambda i: (0, 0)
    return pl.pallas_call(
        functools.partial(_outproj_kernel, alpha=alpha),
        out_shape=jax.ShapeDtypeStruct((n, d), F32),
        grid=(n // tm,),
        in_specs=[pl.BlockSpec((tm, d), row), pl.BlockSpec((1, 1, d), lambda i: (i // per_b, 0, 0)),
                  pl.BlockSpec((1, d), const), pl.BlockSpec((1, d), const),
                  pl.BlockSpec((tm, o_nsa.shape[1]), row), pl.BlockSpec((tm, o_conv.shape[1]), row),
                  pl.BlockSpec(w.shape, const)],
        out_specs=pl.BlockSpec((tm, d), row),
        compiler_params=_cparams("parallel"),
        name="out_proj_ln",
    )(x2, gate, lng, lnb, o_nsa, o_conv, w)


def _ffn1_kernel(x_ref, sc_ref, sh_ref, wg_ref, wu_ref, o_ref, *, tn):
    h = (x_ref[...] * (1.0 + sc_ref[0]) + sh_ref[0]).astype(BF16)
    for j in range(o_ref.shape[1] // tn):
        cols = slice(j * tn, (j + 1) * tn)
        gate = _dot(h, wg_ref[:, cols])
        up = _dot(h, wu_ref[:, cols])
        o_ref[:, cols] = (jax.nn.silu(gate) * up).astype(BF16)


def _ffn1(x2, scale, shift, wg, wu, seq, tm):
    n, d = x2.shape
    dff = wg.shape[1]
    per_b = seq // tm
    row = lambda i: (i, 0)
    mod = lambda i: (i // per_b, 0, 0)
    const = lambda i: (0, 0)
    return pl.pallas_call(
        functools.partial(_ffn1_kernel, tn=256),
        out_shape=jax.ShapeDtypeStruct((n, dff), BF16),
        grid=(n // tm,),
        in_specs=[pl.BlockSpec((tm, d), row), pl.BlockSpec((1, 1, d), mod), pl.BlockSpec((1, 1, d), mod),
                  pl.BlockSpec((d, dff), const), pl.BlockSpec((d, dff), const)],
        out_specs=pl.BlockSpec((tm, dff), row),
        compiler_params=_cparams("parallel"),
        name="ffn_up",
    )(x2, scale, shift, wg, wu)


def _ffn2_kernel(x_ref, gate_ref, g_ref, b_ref, a_ref, w_ref, o_ref, *, alpha):
    y = _dot(a_ref[...], w_ref[...])
    o_ref[...] = _res_ln(x_ref[...], gate_ref[0], y, g_ref[...], b_ref[...], alpha)


def _ffn2(x2, gate, lng, lnb, act, w, seq, tm, alpha):
    n, d = x2.shape
    per_b = seq // tm
    row = lambda i: (i, 0)
    const = lambda i: (0, 0)
    return pl.pallas_call(
        functools.partial(_ffn2_kernel, alpha=alpha),
        out_shape=jax.ShapeDtypeStruct((n, d), F32),
        grid=(n // tm,),
        in_specs=[pl.BlockSpec((tm, d), row), pl.BlockSpec((1, 1, d), lambda i: (i // per_b, 0, 0)),
                  pl.BlockSpec((1, d), const), pl.BlockSpec((1, d), const),
                  pl.BlockSpec((tm, act.shape[1]), row), pl.BlockSpec(w.shape, const)],
        out_specs=pl.BlockSpec((tm, d), row),
        compiler_params=_cparams("parallel"),
        name="ffn_down_ln",
    )(x2, gate, lng, lnb, act, w)


MOE_TM = 1024
MOE_FC = 512
GATHER_ROWS = 1024


def _router_kernel(x_ref, sc_ref, sh_ref, rw_ref, h_ref, ti_ref, tw_ref):
    hf = x_ref[...] * (1.0 + sc_ref[0]) + sh_ref[0]
    h_ref[...] = hf
    lane = lax.broadcasted_iota(jnp.int32, (hf.shape[0], LANE), 1)
    logits = jnp.dot(hf, rw_ref[...], precision=HIGHEST, preferred_element_type=F32)
    logits = jnp.where(lane < N_EXPERTS, logits, -jnp.inf)
    m1 = jnp.max(logits, axis=-1, keepdims=True)
    i1 = jnp.min(jnp.where(logits == m1, lane, LANE), axis=-1, keepdims=True)
    rest = jnp.where(lane == i1, -jnp.inf, logits)
    m2 = jnp.max(rest, axis=-1, keepdims=True)
    i2 = jnp.min(jnp.where(rest == m2, lane, LANE), axis=-1, keepdims=True)
    e2 = jnp.exp(m2 - m1)
    ti_ref[...] = jnp.where(lane == 0, i1, jnp.where(lane == 1, i2, 0))
    tw_ref[...] = jnp.where(lane == 0, 1.0 / (1.0 + e2), jnp.where(lane == 1, e2 / (1.0 + e2), 0.0))


def _router(x2, scale, shift, router_p, seq, tm):
    n, d = x2.shape
    per_b = seq // tm
    row = lambda i: (i, 0)
    mod = lambda i: (i // per_b, 0, 0)
    return pl.pallas_call(
        _router_kernel,
        out_shape=(jax.ShapeDtypeStruct((n, d), F32), jax.ShapeDtypeStruct((n, LANE), jnp.int32),
                   jax.ShapeDtypeStruct((n, LANE), F32)),
        grid=(n // tm,),
        in_specs=[pl.BlockSpec((tm, d), row), pl.BlockSpec((1, 1, d), mod), pl.BlockSpec((1, 1, d), mod),
                  pl.BlockSpec(router_p.shape, lambda i: (0, 0))],
        out_specs=(pl.BlockSpec((tm, d), row), pl.BlockSpec((tm, LANE), row), pl.BlockSpec((tm, LANE), row)),
        compiler_params=_cparams("parallel"),
        name="moe_router",
    )(x2, scale, shift, router_p)


def _gather_rows_kernel(idx_ref, src_ref, o_ref, sem):
    rows = o_ref.shape[0]
    base = pl.program_id(0) * rows

    def issue(r, carry):
        pltpu.make_async_copy(src_ref.at[pl.ds(idx_ref[base + r], 1)], o_ref.at[pl.ds(r, 1)], sem).start()
        return carry

    lax.fori_loop(0, rows, issue, 0, unroll=8)
    pltpu.make_async_copy(src_ref.at[pl.ds(0, rows)], o_ref, sem).wait()


def _gather_rows(idx, src):
    n_out = idx.shape[0]
    d = src.shape[1]
    return pl.pallas_call(
        _gather_rows_kernel,
        out_shape=jax.ShapeDtypeStruct((n_out, d), src.dtype),
        grid_spec=pltpu.PrefetchScalarGridSpec(
            num_scalar_prefetch=1,
            grid=(n_out // GATHER_ROWS,),
            in_specs=[pl.BlockSpec(memory_space=pl.ANY)],
            out_specs=pl.BlockSpec((GATHER_ROWS, d), lambda i, idx_ref: (i, 0)),
            scratch_shapes=[pltpu.SemaphoreType.DMA]),
        compiler_params=_cparams("arbitrary"),
        name="row_gather",
    )(idx, src)


def _expert_up_kernel(te_ref, nu_ref, x_ref, wg_ref, wu_ref, o_ref, xb_sc):
    i = pl.program_id(0)
    f = pl.program_id(1)

    @pl.when(i < nu_ref[0])
    def _compute():
        @pl.when(f == 0)
        def _cast():
            xb_sc[...] = x_ref[...].astype(BF16)
        h = xb_sc[...]
        gate = _dot(h, wg_ref[0, 0])
        up = _dot(h, wu_ref[0, 0])
        o_ref[...] = (jax.nn.silu(gate) * up).astype(BF16)

    @pl.when(i >= nu_ref[0])
    def _unused_tile():
        o_ref[...] = jnp.zeros_like(o_ref)


def _expert_up(tile_expert, n_used, xg, w1b, layer):
    p_rows, d = xg.shape
    nf = w1b.shape[3] // (2 * MOE_FC)
    return pl.pallas_call(
        _expert_up_kernel,
        out_shape=jax.ShapeDtypeStruct((p_rows, nf * MOE_FC), BF16),
        grid_spec=pltpu.PrefetchScalarGridSpec(
            num_scalar_prefetch=2,
            grid=(p_rows // MOE_TM, nf),
            in_specs=[pl.BlockSpec((MOE_TM, d), lambda i, f, te, nu: (i, 0)),
                      pl.BlockSpec((1, 1, d, MOE_FC), lambda i, f, te, nu: (layer, te[i], 0, f)),
                      pl.BlockSpec((1, 1, d, MOE_FC), lambda i, f, te, nu: (layer, te[i], 0, nf + f))],
            out_specs=pl.BlockSpec((MOE_TM, MOE_FC), lambda i, f, te, nu: (i, f)),
            scratch_shapes=[pltpu.VMEM((MOE_TM, d), BF16)]),
        compiler_params=_cparams("parallel", "arbitrary"),
        name="moe_up",
    )(tile_expert, n_used, xg, w1b, w1b)


def _expert_down_kernel(te_ref, nu_ref, a_ref, w_ref, o_ref):
    i = pl.program_id(0)

    @pl.when(i < nu_ref[0])
    def _compute():
        o_ref[...] = _dot(a_ref[...], w_ref[0, 0])

    @pl.when(i >= nu_ref[0])
    def _unused_tile():
        o_ref[...] = jnp.zeros_like(o_ref)


def _expert_down(tile_expert, n_used, act, w2b, layer):
    p_rows, dffe = act.shape
    d = w2b.shape[3]
    return pl.pallas_call(
        _expert_down_kernel,
        out_shape=jax.ShapeDtypeStruct((p_rows, d), F32),
        grid_spec=pltpu.PrefetchScalarGridSpec(
            num_scalar_prefetch=2,
            grid=(p_rows // MOE_TM,),
            in_specs=[pl.BlockSpec((MOE_TM, dffe), lambda i, te, nu: (i, 0)),
                      pl.BlockSpec((1, 1, dffe, d), lambda i, te, nu: (layer, te[i], 0, 0))],
            out_specs=pl.BlockSpec((MOE_TM, d), lambda i, te, nu: (i, 0))),
        compiler_params=_cparams("parallel"),
        name="moe_down",
    )(tile_expert, n_used, act, w2b)


def _moe_combine_kernel(x_ref, gate_ref, g_ref, b_ref, tw_ref, y1_ref, y2_ref, o_ref, *, alpha):
    tw = tw_ref[...]
    y = tw[:, 0:1] * y1_ref[...] + tw[:, 1:2] * y2_ref[...]
    o_ref[...] = _res_ln(x_ref[...], gate_ref[0], y, g_ref[...], b_ref[...], alpha)


def _moe_combine(x2, gate, lng, lnb, tw, yg, seq, tm, alpha):
    n, d = x2.shape
    per_b = seq // tm
    nt = n // tm
    row = lambda i: (i, 0)
    const = lambda i: (0, 0)
    return pl.pallas_call(
        functools.partial(_moe_combine_kernel, alpha=alpha),
        out_shape=jax.ShapeDtypeStruct((n, d), F32),
        grid=(nt,),
        in_specs=[pl.BlockSpec((tm, d), row), pl.BlockSpec((1, 1, d), lambda i: (i // per_b, 0, 0)),
                  pl.BlockSpec((1, d), const), pl.BlockSpec((1, d), const), pl.BlockSpec((tm, LANE), row),
                  pl.BlockSpec((tm, d), row), pl.BlockSpec((tm, d), lambda i: (nt + i, 0))],
        out_specs=pl.BlockSpec((tm, d), row),
        compiler_params=_cparams("parallel"),
        name="moe_combine_ln",
    )(x2, gate, lng, lnb, tw, yg, yg)


def _route_tables(ti, n_rows):
    n = ti.shape[0]
    e_flat = jnp.concatenate([ti[:, 0], ti[:, 1]])
    onehot = (e_flat[:, None] == jnp.arange(N_EXPERTS, dtype=jnp.int32)[None, :]).astype(jnp.int32)
    cum = jnp.cumsum(onehot, axis=0)
    counts = cum[-1]
    padded = (counts + MOE_TM - 1) // MOE_TM * MOE_TM
    ends = jnp.cumsum(padded)
    pos = jnp.sum(onehot * (cum - 1 + (ends - padded)[None, :]), axis=1)
    tok = jnp.tile(jnp.arange(n, dtype=jnp.int32), 2)
    src = (jnp.arange(n_rows, dtype=jnp.int32) % n).at[pos].set(tok, unique_indices=True)
    tile_start = jnp.arange(n_rows // MOE_TM, dtype=jnp.int32) * MOE_TM
    tile_expert = jnp.minimum(jnp.sum((tile_start[:, None] >= ends[None, :]).astype(jnp.int32), axis=1),
                              N_EXPERTS - 1)
    return pos.astype(jnp.int32), src, tile_expert, (ends[-1:] // MOE_TM).astype(jnp.int32)


def _moe(x2, scale, shift, gate, lng, lnb, router_p, w1b, w2b, layer, seq, tm, alpha):
    n, d = x2.shape
    n_rows = 2 * n + N_EXPERTS * MOE_TM
    assert n_rows % GATHER_ROWS == 0 and (2 * n) % GATHER_ROWS == 0
    h, ti, tw = _router(x2, scale, shift, router_p, seq, tm)
    pos, src, tile_expert, n_used = _route_tables(ti, n_rows)
    xg = _gather_rows(src, h)
    act = _expert_up(tile_expert, n_used, xg, w1b, layer)
    y = _expert_down(tile_expert, n_used, act, w2b, layer)
    yg = _gather_rows(pos, y)
    return _moe_combine(x2, gate, lng, lnb, tw, yg, seq, tm, alpha)


def _rel_bucket_np(dist):
    n = np.maximum(dist, 0)
    max_exact = NUM_BUCKETS // 2
    nf = np.maximum(n, 1).astype(np.float32)
    large = max_exact + (np.log(nf / np.float32(max_exact)) / np.float32(math.log(MAX_DISTANCE / max_exact))
                         * np.float32(NUM_BUCKETS - max_exact)).astype(np.int32)
    large = np.minimum(large, NUM_BUCKETS - 1)
    return np.where(n < max_exact, n, large).astype(np.int32)


def _bias_lookup(rel_bias, dist, valid):
    onehot = _rel_bucket_np(dist)[..., None] == np.arange(NUM_BUCKETS)
    vals = jnp.einsum('...k,kh->...h', jnp.asarray(onehot).astype(F32), rel_bias, precision=HIGHEST)
    return jnp.where(jnp.asarray(valid)[..., None], vals * LOG2E, NEG_INF)


def _key_major(vals, add_masked_tile=False):
    n_tiles, tq, tk, _ = vals.shape
    if add_masked_tile:
        vals = jnp.concatenate([vals, jnp.full((1,) + vals.shape[1:], NEG_INF, vals.dtype)], axis=0)
        n_tiles += 1
    vals = vals.reshape(n_tiles, tq, tk, N_KV, Q_PER_KV).transpose(3, 0, 2, 4, 1)
    return vals.reshape(N_KV, n_tiles, tk, Q_PER_KV * tq).astype(F32)


def _band_dist(n_delta):
    i = np.arange(ATT_TQ)[None, :, None]
    k = np.arange(ATT_TQ)[None, None, :]
    return ATT_TQ * np.arange(n_delta)[:, None, None] + i - k


def _cmp_bias_table(rel_bias, n_qt, n_cmp_pad):
    per_tile = ATT_TQ // CMP_STRIDE
    shift = per_tile * (n_qt - 1)
    rel_n = np.arange(n_cmp_pad + shift) - shift
    dist = np.arange(ATT_TQ)[:, None] - (CMP_STRIDE * rel_n[None, :] + L_CMP - 1)
    base = _bias_lookup(rel_bias, dist, dist >= 0)
    tiles = [base[:, shift - per_tile * qt: shift - per_tile * qt + n_cmp_pad] for qt in range(n_qt)]
    return _key_major(jnp.stack(tiles))


def _inproj_weight(w_in):
    depth, d, _ = w_in.shape
    d_nsa = N_HEADS * HEAD_DIM
    d_kv = N_KV * HEAD_DIM
    base_ks = d_nsa + 2 * d_kv
    base_gl = d_nsa + 6 * d_kv
    base_cv = base_gl + N_HEADS * N_BRANCH
    wq = jnp.pad(w_in[:, :, :d_nsa].reshape(depth, d, N_HEADS, HEAD_DIM),
                 ((0, 0), (0, 0), (0, 0), (0, LANE - HEAD_DIM))).reshape(depth, d, Q_COLS)
    parts = [wq, w_in[:, :, d_nsa:base_ks]]
    for pair in range(2):
        kbase = base_ks + pair * 2 * d_kv
        for g in range(N_KV):
            parts.append(w_in[:, :, kbase + g * HEAD_DIM: kbase + (g + 1) * HEAD_DIM])
            parts.append(w_in[:, :, kbase + d_kv + g * HEAD_DIM: kbase + d_kv + (g + 1) * HEAD_DIM])
    parts.append(w_in[:, :, base_cv:])
    gl = w_in[:, :, base_gl:base_cv].reshape(depth, d, N_KV, Q_PER_KV, N_BRANCH).transpose(0, 2, 4, 3, 1)
    gl = jnp.pad(gl.reshape(depth, N_KV, N_BRANCH * Q_PER_KV, d),
                 ((0, 0), (0, 0), (0, LANE - N_BRANCH * Q_PER_KV), (0, 0)))
    return jnp.concatenate(parts, axis=-1).astype(BF16), gl.reshape(depth, GL_ROWS, d).astype(BF16)


def kernel(x, c, w_in, cmp_pos, cmp_w1, cmp_w2, conv_w, conv_b, conv_ln_g, conv_ln_b, w_out, rel_bias,
           ada_w, ada_b, ln_g, ln_b, ffn_w1, ffn_w2, router_w, moe_w1, moe_w2):
    bsz, seq, d = x.shape
    depth = w_in.shape[0]
    n = bsz * seq
    alpha = (2 * depth) ** 0.25
    n_blocks = seq // L_SEL
    n_sel = min(N_SEL, n_blocks)
    n_cmp = (seq - L_CMP) // CMP_STRIDE + 1
    nrow = seq // CMP_STRIDE
    assert d == 1024 and seq % 512 == 0 and nrow == LANE and n_cmp == nrow - 1
    tm = 512

    n_qt = seq // ATT_TQ
    d_win = _band_dist(WINDOW // ATT_TQ + 1)
    win_tab = _key_major(_bias_lookup(rel_bias, d_win, (d_win >= 0) & (d_win < WINDOW)), add_masked_tile=True)
    d_sel = _band_dist(3)
    assert _rel_bucket_np(d_sel[2]).min() == NUM_BUCKETS - 1
    sel_tab = _key_major(_bias_lookup(rel_bias, d_sel, d_sel >= 0), add_masked_tile=True)
    cmp_tab = _cmp_bias_table(rel_bias, n_qt, nrow)

    w_in_p, w_gl_t = _inproj_weight(w_in)
    eye2 = jnp.eye(2, dtype=F32)
    w1r = cmp_w1.reshape(depth, 2, L_CMP, HEAD_DIM, HEAD_DIM)
    w1x = jnp.einsum('zslde,st,gh->zlsgdhte', w1r, eye2, eye2).reshape(depth, L_CMP, 4 * HEAD_DIM, 4 * HEAD_DIM)
    half = L_CMP // 2
    w_lo = w1x[:, :half].astype(BF16)
    w_hi = w1x[:, half:].astype(BF16)
    w2bd = jnp.einsum('zsde,st,gh->zgsdhte', cmp_w2, eye2, eye2).reshape(depth, 4 * HEAD_DIM, 4 * HEAD_DIM).astype(BF16)
    posx = jnp.broadcast_to(cmp_pos[:, :, :, None, :], (depth, 2, L_CMP, N_KV, HEAD_DIM)).transpose(0, 2, 1, 3, 4)
    pos_lo = posx[:, :half].reshape(depth, half, 4 * HEAD_DIM)
    pos_hi = posx[:, half:].reshape(depth, half, 4 * HEAD_DIM)
    conv_wp = jnp.broadcast_to(conv_w[:, :, None, :], conv_w.shape[:2] + (SUBLANE, conv_w.shape[2]))
    w_out_b = w_out.astype(BF16)
    dff = ffn_w2.shape[1]
    ffn_wg = ffn_w1[:, :, :dff].astype(BF16)
    ffn_wu = ffn_w1[:, :, dff:].astype(BF16)
    ffn_w2b = ffn_w2.astype(BF16)
    router_p = jnp.pad(router_w, ((0, 0), (0, 0), (0, LANE - N_EXPERTS)))
    moe_w1b = moe_w1.astype(BF16)
    moe_w2b = moe_w2.astype(BF16)

    mod = _ada(c, ada_w, ada_b).reshape(depth, 6, bsz, 1, d)
    x2 = x.reshape(n, d)
    for l in range(depth):
        shift1, scale1, gate1, shift2, scale2, gate2 = [mod[l, k] for k in range(6)]
        q, kvc, kvsw, gl3, cv = _inproj(x2, scale1, shift1, w_in_p[l], w_gl_t[l], seq, tm)
        q3 = q.reshape(bsz, seq, Q_COLS)
        kvsw3 = kvsw.reshape(bsz, seq, KVSW_COLS)
        kvsw_t = kvsw3.reshape(bsz, n_qt, ATT_TQ, 2 * N_KV, LANE).transpose(0, 3, 1, 4, 2)
        kvcmp, kvcmp_t = _compress(kvc.reshape(bsz, seq, KVC_COLS), pos_lo[l], pos_hi[l], w_lo[l], w_hi[l],
                                   w2bd[l])
        o_nsa = _nsa_attn(q3, kvcmp, kvcmp_t, cmp_tab, kvsw3, kvsw_t, sel_tab, win_tab, gl3, n_blocks, n_sel)
        o_conv = _conv_branch(cv.reshape(bsz, seq, 1024), conv_wp[l], conv_b[l][None], conv_ln_g[l][None],
                              conv_ln_b[l][None])
        x2 = _outproj(x2, gate1, ln_g[l, 0][None], ln_b[l, 0][None], o_nsa.reshape(n, 512), o_conv.reshape(n, 512),
                      w_out_b[l], seq, tm, alpha)
        if l % 2 == 0:
            act = _ffn1(x2, scale2, shift2, ffn_wg[l // 2], ffn_wu[l // 2], seq, tm)
            x2 = _ffn2(x2, gate2, ln_g[l, 1][None], ln_b[l, 1][None], act, ffn_w2b[l // 2], seq, tm, alpha)
        else:
            x2 = _moe(x2, scale2, shift2, gate2, ln_g[l, 1][None], ln_b[l, 1][None], router_p[l // 2],
                      moe_w1b, moe_w2b, l // 2, seq, tm, alpha)
    return x2.reshape(bsz, seq, d)
```

```python
import functools
import math

import numpy as np
import jax
import jax.numpy as jnp
from jax import lax
from jax.experimental import pallas as pl
from jax.experimental.pallas import tpu as pltpu

F32 = jnp.float32
BF16 = jnp.bfloat16
HIGHEST = lax.Precision.HIGHEST

HEAD_DIM = 64
N_HEADS = 8
N_KV = 2
Q_PER_KV = N_HEADS // N_KV
N_BRANCH = 3
L_CMP = 32
CMP_STRIDE = 16
L_SEL = 64
N_SEL = 8
WINDOW = 512
CONV_WIDTH = 31
NUM_BUCKETS = 32
MAX_DISTANCE = 128
N_EXPERTS = 8
LN_EPS = 1e-5
NEG_INF = -1e30
LOG2E = math.log2(math.e)

LANE = 128
SUBLANE = 8
ATT_TQ = 128
ATT_ROWS = Q_PER_KV * ATT_TQ
CONV_TQ = 256
CONV_HALO = 32
CONV_RC = 64
VMEM_LIMIT_BYTES = 52 * 1024 * 1024


def _cparams(*sem):
    return pltpu.CompilerParams(dimension_semantics=sem, vmem_limit_bytes=VMEM_LIMIT_BYTES)


def _dot(a, b):
    return jnp.dot(a, b, preferred_element_type=F32)


def _dot_nt(a, b, precision=None):
    return lax.dot_general(a, b, (((1,), (1,)), ((), ())), precision=precision,
                           preferred_element_type=F32)


def _res_ln(x, gate, y, g, b, alpha):
    z = alpha * x + (1.0 + gate) * y
    mu = jnp.mean(z, axis=-1, keepdims=True)
    zc = z - mu
    var = jnp.mean(zc * zc, axis=-1, keepdims=True)
    return zc * lax.rsqrt(var + LN_EPS) * g + b


def _ada_kernel(c_ref, w_ref, b_ref, o_ref):
    sc = jax.nn.silu(c_ref[...])
    o_ref[0, 0] = jnp.dot(sc, w_ref[0], precision=HIGHEST, preferred_element_type=F32) + b_ref[0]


def _ada(c, ada_w, ada_b):
    depth, d, _ = ada_w.shape
    bsz = c.shape[0]
    return pl.pallas_call(
        _ada_kernel,
        out_shape=jax.ShapeDtypeStruct((depth, 6, bsz, d), F32),
        grid=(depth, 6),
        in_specs=[pl.BlockSpec((bsz, d), lambda l, k: (0, 0)),
                  pl.BlockSpec((1, d, d), lambda l, k: (l, 0, k)),
                  pl.BlockSpec((1, 1, d), lambda l, k: (l, 0, k))],
        out_specs=pl.BlockSpec((1, 1, bsz, d), lambda l, k: (l, k, 0, 0)),
        compiler_params=_cparams("parallel", "parallel"),
        name="ada_mod",
    )(c, ada_w, ada_b.reshape(depth, 1, 6 * d))


Q_COLS = N_HEADS * LANE
KVC_COLS = 4 * HEAD_DIM
KVSW_COLS = 8 * HEAD_DIM
GL_ROWS = N_KV * LANE


def _inproj_kernel(x_ref, sc_ref, sh_ref, w_ref, wgl_ref, q_ref, kvc_ref, kvsw_ref, glt_ref, cv_ref):
    h = (x_ref[...] * (1.0 + sc_ref[0]) + sh_ref[0]).astype(BF16)
    c0, c1, c2 = Q_COLS, Q_COLS + KVC_COLS, Q_COLS + KVC_COLS + KVSW_COLS
    q_ref[...] = (_dot(h, w_ref[:, 0:c0]) * (HEAD_DIM ** -0.5 * LOG2E)).astype(BF16)
    kvc_ref[...] = _dot(h, w_ref[:, c0:c1])
    kvsw_ref[...] = _dot(h, w_ref[:, c1:c2]).astype(BF16)
    cv_ref[...] = _dot(h, w_ref[:, c2:])
    glt_ref[...] = _dot_nt(wgl_ref[...], h)


def _inproj(x2, scale, shift, w, wgl_t, seq, tm):
    n, d = x2.shape
    per_b = seq // tm
    d_cv = w.shape[1] - (Q_COLS + KVC_COLS + KVSW_COLS)
    row = lambda i: (i, 0)
    mod = lambda i: (i // per_b, 0, 0)
    const = lambda i: (0, 0)
    return pl.pallas_call(
        _inproj_kernel,
        out_shape=(jax.ShapeDtypeStruct((n, Q_COLS), BF16), jax.ShapeDtypeStruct((n, KVC_COLS), F32),
                   jax.ShapeDtypeStruct((n, KVSW_COLS), BF16), jax.ShapeDtypeStruct((GL_ROWS, n), F32),
                   jax.ShapeDtypeStruct((n, d_cv), F32)),
        grid=(n // tm,),
        in_specs=[pl.BlockSpec((tm, d), row), pl.BlockSpec((1, 1, d), mod), pl.BlockSpec((1, 1, d), mod),
                  pl.BlockSpec(w.shape, const), pl.BlockSpec(wgl_t.shape, const)],
        out_specs=(pl.BlockSpec((tm, Q_COLS), row), pl.BlockSpec((tm, KVC_COLS), row),
                   pl.BlockSpec((tm, KVSW_COLS), row), pl.BlockSpec((GL_ROWS, tm), lambda i: (0, i)),
                   pl.BlockSpec((tm, d_cv), row)),
        compiler_params=_cparams("parallel"),
        name="in_proj",
    )(x2, scale, shift, w, wgl_t)


def _compress_kernel(ak_ref, av_ref, plo_ref, phi_ref, wlo_ref, whi_ref, w2_ref, o_ref, ot_ref):
    nrow = o_ref.shape[1]
    lo = jnp.zeros((nrow, o_ref.shape[2]), F32)
    hi = jnp.zeros((nrow, o_ref.shape[2]), F32)
    for l in range(CMP_STRIDE):
        for part, a_ref in enumerate((ak_ref, av_ref)):
            lanes = slice(part * LANE, (part + 1) * LANE)
            tok = a_ref[0, pl.ds(l, nrow, stride=CMP_STRIDE), :]
            lo = lo + _dot((tok + plo_ref[l:l + 1, lanes]).astype(BF16), wlo_ref[l, lanes, :])
            hi = hi + _dot((tok + phi_ref[l:l + 1, lanes]).astype(BF16), whi_ref[l, lanes, :])
    pre = lo + pltpu.roll(hi, nrow - 1, 0)
    act = jax.nn.gelu(pre, approximate=True)
    out = _dot(act.astype(BF16), w2_ref[...])
    o_ref[0] = out.astype(BF16)
    ot_ref[0] = out.T.astype(BF16)


def _compress(kvc3, pos_lo, pos_hi, wlo, whi, w2bd):
    bsz, seq, width = kvc3.shape
    nrow = seq // CMP_STRIDE
    const = lambda b: (0, 0)
    const3 = lambda b: (0, 0, 0)
    return pl.pallas_call(
        _compress_kernel,
        out_shape=(jax.ShapeDtypeStruct((bsz, nrow, 256), BF16), jax.ShapeDtypeStruct((bsz, 256, nrow), BF16)),
        grid=(bsz,),
        in_specs=[pl.BlockSpec((1, seq, LANE), lambda b: (b, 0, 0)), pl.BlockSpec((1, seq, LANE), lambda b: (b, 0, 1)),
                  pl.BlockSpec(pos_lo.shape, const), pl.BlockSpec(pos_hi.shape, const),
                  pl.BlockSpec(wlo.shape, const3), pl.BlockSpec(whi.shape, const3),
                  pl.BlockSpec((256, 256), const)],
        out_specs=(pl.BlockSpec((1, nrow, 256), lambda b: (b, 0, 0)),
                   pl.BlockSpec((1, 256, nrow), lambda b: (b, 0, 0))),
        compiler_params=_cparams("parallel"),
        name="compress_mlp",
    )(kvc3, kvc3, pos_lo, pos_hi, wlo, whi, w2bd)


def _stack_heads(q):
    return jnp.concatenate([q[:, r * LANE:(r + 1) * LANE] for r in range(Q_PER_KV)], axis=0)


def _branch_gates(gl_t, branch):
    return jnp.concatenate([jax.nn.sigmoid(gl_t[branch * Q_PER_KV + r:branch * Q_PER_KV + r + 1, :])
                            for r in range(Q_PER_KV)], axis=1)


def _heads_to_rows(o_t):
    tq = o_t.shape[1] // Q_PER_KV
    og = o_t.astype(BF16)
    ii = lax.broadcasted_iota(jnp.int32, (tq, tq), 0)
    kk = lax.broadcasted_iota(jnp.int32, (tq, tq), 1)
    eye = jnp.where(ii == kk, 1.0, 0.0).astype(BF16)
    halves = []
    for pair in range(Q_PER_KV // 2):
        two_heads = jnp.concatenate([og[:, (2 * pair) * tq:(2 * pair + 1) * tq],
                                     og[:, (2 * pair + 1) * tq:(2 * pair + 2) * tq]], axis=0)
        halves.append(_dot_nt(eye, two_heads))
    return jnp.concatenate(halves, axis=1).astype(BF16)


def _cmp_section(qs, qt, kv_ref, kvt_ref, tab_ref, gl_ref, sel_sc, *, n_blocks, n_sel):
    tq = ATT_TQ
    outs = []
    jj = lax.broadcasted_iota(jnp.int32, (n_blocks, LANE), 0)
    nn = lax.broadcasted_iota(jnp.int32, (n_blocks, LANE), 1)
    overlap_t = jnp.where((CMP_STRIDE * nn < L_SEL * jj + L_SEL) & (CMP_STRIDE * nn + L_CMP > L_SEL * jj),
                          1.0, 0.0).astype(F32)
    jb = lax.broadcasted_iota(jnp.int32, (n_blocks, tq), 0)
    tb = (qt * tq + lax.broadcasted_iota(jnp.int32, (n_blocks, tq), 1)) // L_SEL
    for g in range(N_KV):
        bias = tab_ref[g, 0]
        s = _dot_nt(kv_ref[0, :, g * LANE:(g + 1) * LANE], qs[g]) + bias
        valid = bias > 0.5 * NEG_INF
        m = jnp.max(s, axis=0, keepdims=True)
        p = jnp.where(valid, jnp.exp2(s - m), 0.0)
        l = jnp.sum(p, axis=0, keepdims=True)
        p = p * (1.0 / jnp.where(l > 0.0, l, 1.0))
        o_t = _dot(kvt_ref[0, g * LANE + HEAD_DIM:(g + 1) * LANE, :], p.astype(BF16))
        outs.append(o_t * _branch_gates(gl_ref[g * LANE:(g + 1) * LANE, :], 0))

        psum = p[:, 0:tq] + p[:, tq:2 * tq] + p[:, 2 * tq:3 * tq] + p[:, 3 * tq:4 * tq]
        imp_t = jnp.dot(overlap_t, psum, precision=HIGHEST, preferred_element_type=F32)
        score = jnp.where(jb > tb, -jnp.inf,
                          jnp.where((jb == 0) | (jb == tb) | (jb == tb - 1), jnp.inf, imp_t))
        rank = jnp.zeros((n_blocks, tq), F32)
        for jp in range(n_blocks):
            row = score[jp:jp + 1, :]
            earlier = jnp.where(jb > jp, 1.0, 0.0)
            rank = rank + jnp.where(row > score, 1.0, 0.0) + jnp.where(row == score, earlier, 0.0)
        madd = jnp.where(rank < float(n_sel), 0.0, NEG_INF)
        per_chunk = ATT_TQ // L_SEL
        for c in range(n_blocks // per_chunk):
            sel_sc[g, c, 0:per_chunk, :] = madd[c * per_chunk:(c + 1) * per_chunk, :]
    return outs


def _band_section(qs, qt, kv_ref, kvt_ref, tab_ref, gl_ref, sel_sc, s_sc, *, span, group, branch):
    tq = ATT_TQ
    n_delta = tab_ref.shape[1] - 1
    n_chunks = kv_ref.shape[1] // tq
    c_base = qt - span if span is not None else 0
    use_sel = sel_sc is not None

    def chunk_ids(gi, u):
        c = c_base + gi * group + u
        inside = (c >= 0) & (c <= qt)
        dd = jnp.where(inside, jnp.minimum(qt - c, n_delta - 1), n_delta)
        return jnp.clip(c, 0, n_chunks - 1), dd

    def rows(gi, u):
        return pl.ds(pl.multiple_of((gi * group + u) * tq, tq), tq)

    def scores(gi, ms):
        ms = list(ms)
        for u in range(group):
            cc, dd = chunk_ids(gi, u)
            kv = kv_ref[0, pl.ds(pl.multiple_of(cc * tq, tq), tq), :]
            for g in range(N_KV):
                s = _dot_nt(kv[:, g * LANE:(g + 1) * LANE], qs[g]) + tab_ref[g, dd]
                if use_sel:
                    blocks = sel_sc[g, cc, 0:tq // L_SEL, :]
                    madd = jnp.concatenate([jnp.broadcast_to(blocks[j:j + 1, :], (L_SEL, tq))
                                            for j in range(tq // L_SEL)], axis=0)
                    s = s + jnp.concatenate([madd] * Q_PER_KV, axis=1)
                s_sc[g, rows(gi, u), :] = s
                ms[g] = jnp.maximum(ms[g], jnp.max(s, axis=0, keepdims=True))
        return tuple(ms)

    m0 = tuple(jnp.full((1, ATT_ROWS), NEG_INF, F32) for _ in range(N_KV))
    if span is not None:
        n_groups = 1
        m = scores(0, m0)
    else:
        n_groups = qt // group + 1
        m = lax.fori_loop(0, n_groups, scores, m0)

    def values(gi, carry):
        carry = [list(c) for c in carry]
        for u in range(group):
            cc, _ = chunk_ids(gi, u)
            for g in range(N_KV):
                l, acc = carry[g]
                p = jnp.exp2(s_sc[g, rows(gi, u), :] - m[g])
                l = l + jnp.sum(p, axis=0, keepdims=True)
                acc = acc + _dot(kvt_ref[0, g, cc, HEAD_DIM:, :], p.astype(BF16))
                carry[g] = [l, acc]
        return tuple(tuple(c) for c in carry)

    init = tuple((jnp.zeros((1, ATT_ROWS), F32), jnp.zeros((HEAD_DIM, ATT_ROWS), F32)) for _ in range(N_KV))
    if span is not None:
        out = values(0, init)
    else:
        out = lax.fori_loop(0, n_groups, values, init)
    return [out[g][1] * (_branch_gates(gl_ref[g * LANE:(g + 1) * LANE, :], branch) / out[g][0]) for g in range(N_KV)]


SEL_GROUP = 4
WIN_SPAN = WINDOW // ATT_TQ


def _nsa_kernel(q_ref, kvc_ref, kvct_ref, ctab_ref, kvs_ref, kvst_ref, stab_ref, kvw_ref, kvwt_ref, wtab_ref,
                gl_ref, o_ref, sel_sc, ssel_sc, swin_sc, *, n_blocks, n_sel):
    qt = pl.program_id(1)
    hw = Q_PER_KV * LANE
    qs = [_stack_heads(q_ref[0, :, g * hw:(g + 1) * hw]) for g in range(N_KV)]
    o_cmp = _cmp_section(qs, qt, kvc_ref, kvct_ref, ctab_ref, gl_ref, sel_sc, n_blocks=n_blocks, n_sel=n_sel)
    o_sel = _band_section(qs, qt, kvs_ref, kvst_ref, stab_ref, gl_ref, sel_sc, ssel_sc,
                          span=None, group=SEL_GROUP, branch=1)
    o_win = _band_section(qs, qt, kvw_ref, kvwt_ref, wtab_ref, gl_ref, None, swin_sc,
                          span=WIN_SPAN, group=WIN_SPAN + 1, branch=2)
    for g in range(N_KV):
        o_ref[0, :, g * 256:(g + 1) * 256] = _heads_to_rows(o_cmp[g] + o_sel[g] + o_win[g])


def _nsa_attn(q3, kvcmp, kvcmp_t, cmp_tab, kvsw3, kvsw_t, sel_tab, win_tab, gl_t, n_blocks, n_sel):
    bsz, seq, _ = q3.shape
    nqt = seq // ATT_TQ
    n_cmp = kvcmp.shape[1]
    sel_slots = -(-nqt // SEL_GROUP) * SEL_GROUP
    full4 = lambda b, t: (0, 0, 0, 0)
    kv_spec = lambda pair: pl.BlockSpec((1, seq, N_KV * LANE), lambda b, t: (b, 0, pair))
    kvt_spec = lambda pair: pl.BlockSpec((1, N_KV) + kvsw_t.shape[2:], lambda b, t: (b, pair, 0, 0, 0))
    return pl.pallas_call(
        functools.partial(_nsa_kernel, n_blocks=n_blocks, n_sel=n_sel),
        out_shape=jax.ShapeDtypeStruct((bsz, seq, 512), BF16),
        grid=(bsz, nqt),
        in_specs=[pl.BlockSpec((1, ATT_TQ, Q_COLS), lambda b, t: (b, t, 0)),
                  pl.BlockSpec((1, n_cmp, N_KV * LANE), lambda b, t: (b, 0, 0)),
                  pl.BlockSpec((1, N_KV * LANE, n_cmp), lambda b, t: (b, 0, 0)),
                  pl.BlockSpec((N_KV, 1, n_cmp, ATT_ROWS), lambda b, t: (0, t, 0, 0)),
                  kv_spec(0), kvt_spec(0), pl.BlockSpec(sel_tab.shape, full4),
                  kv_spec(1), kvt_spec(1), pl.BlockSpec(win_tab.shape, full4),
                  pl.BlockSpec((GL_ROWS, ATT_TQ), lambda b, t: (0, b * nqt + t))],
        out_specs=pl.BlockSpec((1, ATT_TQ, 512), lambda b, t: (b, t, 0)),
        scratch_shapes=[pltpu.VMEM((N_KV, n_blocks * L_SEL // ATT_TQ, SUBLANE, ATT_TQ), F32),
                        pltpu.VMEM((N_KV, sel_slots * ATT_TQ, ATT_ROWS), F32),
                        pltpu.VMEM((N_KV, (WIN_SPAN + 1) * ATT_TQ, ATT_ROWS), F32)],
        compiler_params=_cparams("parallel", "parallel"),
        name="nsa_attn",
    )(q3, kvcmp, kvcmp_t, cmp_tab, kvsw3, kvsw_t, sel_tab, kvsw3, kvsw_t, win_tab, gl_t)


def _conv_kernel(cur_ref, prev_ref, w_ref, cb_ref, g_ref, b_ref, o_ref, ybuf, ysh, cbuf):
    i = pl.program_id(1)
    dc = o_ref.shape[-1]
    cur = cur_ref[0]
    prev = prev_ref[0]
    ybuf[CONV_HALO:, :] = cur[:, :dc] * jax.nn.sigmoid(cur[:, dc:])
    halo = prev[:, :dc] * jax.nn.sigmoid(prev[:, dc:])
    ybuf[0:CONV_HALO, :] = jnp.where(i > 0, halo, 0.0)
    n_sh = ysh.shape[1]
    for b in range(1, SUBLANE):
        ysh[b - 1] = ybuf[b:b + n_sh, :]
    off = CONV_HALO - (CONV_WIDTH - 1)
    sub = CONV_RC // SUBLANE
    n_rc = CONV_TQ // CONV_RC
    for lc in range(dc // LANE):
        cols = slice(lc * LANE, (lc + 1) * LANE)
        accs = [jnp.zeros((sub, SUBLANE, LANE), F32) for _ in range(n_rc)]
        for k in range(CONV_WIDTH):
            b = (off + k) % SUBLANE
            wk = w_ref[k, :, cols][None]
            for rc in range(n_rc):
                r0 = rc * CONV_RC + (off + k) - b
                src = ybuf[r0:r0 + CONV_RC, cols] if b == 0 else ysh[b - 1, r0:r0 + CONV_RC, cols]
                accs[rc] = accs[rc] + wk * src.reshape(sub, SUBLANE, LANE)
        for rc in range(n_rc):
            cbuf[rc * CONV_RC:(rc + 1) * CONV_RC, cols] = accs[rc].reshape(CONV_RC, LANE)
    z = cbuf[...] + cb_ref[...]
    mu = jnp.mean(z, axis=-1, keepdims=True)
    zc = z - mu
    var = jnp.mean(zc * zc, axis=-1, keepdims=True)
    y = zc * lax.rsqrt(var + LN_EPS) * g_ref[...] + b_ref[...]
    o_ref[0] = jax.nn.silu(y).astype(BF16)


def _conv_branch(cv3, w, cb, g, b):
    bsz, seq, two_dc = cv3.shape
    dc = two_dc // 2
    per = CONV_TQ // CONV_HALO
    const = lambda bb, i: (0, 0)
    return pl.pallas_call(
        _conv_kernel,
        out_shape=jax.ShapeDtypeStruct((bsz, seq, dc), BF16),
        grid=(bsz, seq // CONV_TQ),
        in_specs=[pl.BlockSpec((1, CONV_TQ, two_dc), lambda bb, i: (bb, i, 0)),
                  pl.BlockSpec((1, CONV_HALO, two_dc), lambda bb, i: (bb, jnp.maximum(i * per - 1, 0), 0)),
                  pl.BlockSpec(w.shape, lambda bb, i: (0, 0, 0)), pl.BlockSpec((1, dc), const),
                  pl.BlockSpec((1, dc), const), pl.BlockSpec((1, dc), const)],
        out_specs=pl.BlockSpec((1, CONV_TQ, dc), lambda bb, i: (bb, i, 0)),
        scratch_shapes=[pltpu.VMEM((CONV_HALO + CONV_TQ, dc), F32),
                        pltpu.VMEM((SUBLANE - 1, CONV_HALO + CONV_TQ - SUBLANE, dc), F32),
                        pltpu.VMEM((CONV_TQ, dc), F32)],
        compiler_params=_cparams("parallel", "parallel"),
        name="conv_branch",
    )(cv3, cv3, w, cb, g, b)


def _outproj_kernel(x_ref, gate_ref, g_ref, b_ref, nsa_ref, ocv_ref, w_ref, o_ref, *, alpha):
    half = nsa_ref.shape[1]
    y = _dot(nsa_ref[...], w_ref[0:half, :]) + _dot(ocv_ref[...], w_ref[half:, :])
    o_ref[...] = _res_ln(x_ref[...], gate_ref[0], y, g_ref[...], b_ref[...], alpha)


def _outproj(x2, gate, lng, lnb, o_nsa, o_conv, w, seq, tm, alpha):
    n, d = x2.shape
    per_b = seq // tm
    row = lambda i: (i, 0)
    const = lambda i: (0, 0)
    return pl.pallas_call(
        functools.partial(_outproj_kernel, alpha=alpha),
        out_shape=jax.ShapeDtypeStruct((n, d), F32),
        grid=(n // tm,),
        in_specs=[pl.BlockSpec((tm, d), row), pl.BlockSpec((1, 1, d), lambda i: (i // per_b, 0, 0)),
                  pl.BlockSpec((1, d), const), pl.BlockSpec((1, d), const),
                  pl.BlockSpec((tm, o_nsa.shape[1]), row), pl.BlockSpec((tm, o_conv.shape[1]), row),
                  pl.BlockSpec(w.shape, const)],
        out_specs=pl.BlockSpec((tm, d), row),
        compiler_params=_cparams("parallel"),
        name="out_proj_ln",
    )(x2, gate, lng, lnb, o_nsa, o_conv, w)


def _ffn1_kernel(x_ref, sc_ref, sh_ref, wg_ref, wu_ref, o_ref, *, tn):
    h = (x_ref[...] * (1.0 + sc_ref[0]) + sh_ref[0]).astype(BF16)
    for j in range(o_ref.shape[1] // tn):
        cols = slice(j * tn, (j + 1) * tn)
        gate = _dot(h, wg_ref[:, cols])
        up = _dot(h, wu_ref[:, cols])
        o_ref[:, cols] = (jax.nn.silu(gate) * up).astype(BF16)


def _ffn1(x2, scale, shift, wg, wu, seq, tm):
    n, d = x2.shape
    dff = wg.shape[1]
    per_b = seq // tm
    row = lambda i: (i, 0)
    mod = lambda i: (i // per_b, 0, 0)
    const = lambda i: (0, 0)
    return pl.pallas_call(
        functools.partial(_ffn1_kernel, tn=256),
        out_shape=jax.ShapeDtypeStruct((n, dff), BF16),
        grid=(n // tm,),
        in_specs=[pl.BlockSpec((tm, d), row), pl.BlockSpec((1, 1, d), mod), pl.BlockSpec((1, 1, d), mod),
                  pl.BlockSpec((d, dff), const), pl.BlockSpec((d, dff), const)],
        out_specs=pl.BlockSpec((tm, dff), row),
        compiler_params=_cparams("parallel"),
        name="ffn_up",
    )(x2, scale, shift, wg, wu)


def _ffn2_kernel(x_ref, gate_ref, g_ref, b_ref, a_ref, w_ref, o_ref, *, alpha):
    y = _dot(a_ref[...], w_ref[...])
    o_ref[...] = _res_ln(x_ref[...], gate_ref[0], y, g_ref[...], b_ref[...], alpha)


def _ffn2(x2, gate, lng, lnb, act, w, seq, tm, alpha):
    n, d = x2.shape
    per_b = seq // tm
    row = lambda i: (i, 0)
    const = lambda i: (0, 0)
    return pl.pallas_call(
        functools.partial(_ffn2_kernel, alpha=alpha),
        out_shape=jax.ShapeDtypeStruct((n, d), F32),
        grid=(n // tm,),
        in_specs=[pl.BlockSpec((tm, d), row), pl.BlockSpec((1, 1, d), lambda i: (i // per_b, 0, 0)),
                  pl.BlockSpec((1, d), const), pl.BlockSpec((1, d), const),
                  pl.BlockSpec((tm, act.shape[1]), row), pl.BlockSpec(w.shape, const)],
        out_specs=pl.BlockSpec((tm, d), row),
        compiler_params=_cparams("parallel"),
        name="ffn_down_ln",
    )(x2, gate, lng, lnb, act, w)


MOE_TM = 1024
MOE_FC = 512
GATHER_ROWS = 1024


def _router_kernel(x_ref, sc_ref, sh_ref, rw_ref, h_ref, ti_ref, tw_ref):
    hf = x_ref[...] * (1.0 + sc_ref[0]) + sh_ref[0]
    h_ref[...] = hf
    lane = lax.broadcasted_iota(jnp.int32, (hf.shape[0], LANE), 1)
    logits = jnp.dot(hf, rw_ref[...], precision=HIGHEST, preferred_element_type=F32)
    logits = jnp.where(lane < N_EXPERTS, logits, -jnp.inf)
    m1 = jnp.max(logits, axis=-1, keepdims=True)
    i1 = jnp.min(jnp.where(logits == m1, lane, LANE), axis=-1, keepdims=True)
    rest = jnp.where(lane == i1, -jnp.inf, logits)
    m2 = jnp.max(rest, axis=-1, keepdims=True)
    i2 = jnp.min(jnp.where(rest == m2, lane, LANE), axis=-1, keepdims=True)
    e2 = jnp.exp(m2 - m1)
    ti_ref[...] = jnp.where(lane == 0, i1, jnp.where(lane == 1, i2, 0))
    tw_ref[...] = jnp.where(lane == 0, 1.0 / (1.0 + e2), jnp.where(lane == 1, e2 / (1.0 + e2), 0.0))


def _router(x2, scale, shift, router_p, seq, tm):
    n, d = x2.shape
    per_b = seq // tm
    row = lambda i: (i, 0)
    mod = lambda i: (i // per_b, 0, 0)
    return pl.pallas_call(
        _router_kernel,
        out_shape=(jax.ShapeDtypeStruct((n, d), F32), jax.ShapeDtypeStruct((n, LANE), jnp.int32),
                   jax.ShapeDtypeStruct((n, LANE), F32)),
        grid=(n // tm,),
        in_specs=[pl.BlockSpec((tm, d), row), pl.BlockSpec((1, 1, d), mod), pl.BlockSpec((1, 1, d), mod),
                  pl.BlockSpec(router_p.shape, lambda i: (0, 0))],
        out_specs=(pl.BlockSpec((tm, d), row), pl.BlockSpec((tm, LANE), row), pl.BlockSpec((tm, LANE), row)),
        compiler_params=_cparams("parallel"),
        name="moe_router",
    )(x2, scale, shift, router_p)


def _gather_rows_kernel(idx_ref, src_ref, o_ref, sem):
    rows = o_ref.shape[0]
    base = pl.program_id(0) * rows

    def issue(r, carry):
        pltpu.make_async_copy(src_ref.at[pl.ds(idx_ref[base + r], 1)], o_ref.at[pl.ds(r, 1)], sem).start()
        return carry

    lax.fori_loop(0, rows, issue, 0, unroll=8)
    pltpu.make_async_copy(src_ref.at[pl.ds(0, rows)], o_ref, sem).wait()


def _gather_rows(idx, src):
    n_out = idx.shape[0]
    d = src.shape[1]
    return pl.pallas_call(
        _gather_rows_kernel,
        out_shape=jax.ShapeDtypeStruct((n_out, d), src.dtype),
        grid_spec=pltpu.PrefetchScalarGridSpec(
            num_scalar_prefetch=1,
            grid=(n_out // GATHER_ROWS,),
            in_specs=[pl.BlockSpec(memory_space=pl.ANY)],
            out_specs=pl.BlockSpec((GATHER_ROWS, d), lambda i, idx_ref: (i, 0)),
            scratch_shapes=[pltpu.SemaphoreType.DMA]),
        compiler_params=_cparams("arbitrary"),
        name="row_gather",
    )(idx, src)


def _scatter_rows_kernel(pos_ref, h_ref, init_ref, o_ref, sem):
    del init_ref
    rows = h_ref.shape[0]
    n = pl.num_programs(0) * rows
    base = pl.program_id(0) * rows

    def issue(r, carry):
        row = h_ref.at[pl.ds(r, 1)]
        pltpu.make_async_copy(row, o_ref.at[pl.ds(pos_ref[base + r], 1)], sem).start()
        pltpu.make_async_copy(row, o_ref.at[pl.ds(pos_ref[n + base + r], 1)], sem).start()
        return carry

    lax.fori_loop(0, rows, issue, 0, unroll=8)
    for _ in range(2):
        pltpu.make_async_copy(h_ref, o_ref.at[pl.ds(0, rows)], sem).wait()


def _scatter_rows(pos, h, n_rows):
    n, d = h.shape
    return pl.pallas_call(
        _scatter_rows_kernel,
        out_shape=jax.ShapeDtypeStruct((n_rows, d), h.dtype),
        grid_spec=pltpu.PrefetchScalarGridSpec(
            num_scalar_prefetch=1,
            grid=(n // GATHER_ROWS,),
            in_specs=[pl.BlockSpec((GATHER_ROWS, d), lambda i, pos_ref: (i, 0)),
                      pl.BlockSpec(memory_space=pl.ANY)],
            out_specs=pl.BlockSpec(memory_space=pl.ANY),
            scratch_shapes=[pltpu.SemaphoreType.DMA]),
        input_output_aliases={2: 0},
        compiler_params=_cparams("arbitrary"),
        name="row_scatter",
    )(pos, h, jnp.zeros((n_rows, d), h.dtype))


def _expert_up_kernel(te_ref, nu_ref, x_ref, wg_ref, wu_ref, o_ref, xb_sc):
    i = pl.program_id(0)
    f = pl.program_id(1)

    @pl.when(i < nu_ref[0])
    def _compute():
        @pl.when(f == 0)
        def _cast():
            xb_sc[...] = x_ref[...].astype(BF16)
        h = xb_sc[...]
        gate = _dot(h, wg_ref[0, 0])
        up = _dot(h, wu_ref[0, 0])
        o_ref[...] = (jax.nn.silu(gate) * up).astype(BF16)

    @pl.when(i >= nu_ref[0])
    def _unused_tile():
        o_ref[...] = jnp.zeros_like(o_ref)


def _expert_up(tile_expert, n_used, xg, w1b, layer):
    p_rows, d = xg.shape
    nf = w1b.shape[3] // (2 * MOE_FC)
    return pl.pallas_call(
        _expert_up_kernel,
        out_shape=jax.ShapeDtypeStruct((p_rows, nf * MOE_FC), BF16),
        grid_spec=pltpu.PrefetchScalarGridSpec(
            num_scalar_prefetch=2,
            grid=(p_rows // MOE_TM, nf),
            in_specs=[pl.BlockSpec((MOE_TM, d), lambda i, f, te, nu: (i, 0)),
                      pl.BlockSpec((1, 1, d, MOE_FC), lambda i, f, te, nu: (layer, te[i], 0, f)),
                      pl.BlockSpec((1, 1, d, MOE_FC), lambda i, f, te, nu: (layer, te[i], 0, nf + f))],
            out_specs=pl.BlockSpec((MOE_TM, MOE_FC), lambda i, f, te, nu: (i, f)),
            scratch_shapes=[pltpu.VMEM((MOE_TM, d), BF16)]),
        compiler_params=_cparams("parallel", "arbitrary"),
        name="moe_up",
    )(tile_expert, n_used, xg, w1b, w1b)


def _expert_down_kernel(te_ref, nu_ref, a_ref, w_ref, o_ref):
    i = pl.program_id(0)

    @pl.when(i < nu_ref[0])
    def _compute():
        o_ref[...] = _dot(a_ref[...], w_ref[0, 0])

    @pl.when(i >= nu_ref[0])
    def _unused_tile():
        o_ref[...] = jnp.zeros_like(o_ref)


def _expert_down(tile_expert, n_used, act, w2b, layer):
    p_rows, dffe = act.shape
    d = w2b.shape[3]
    return pl.pallas_call(
        _expert_down_kernel,
        out_shape=jax.ShapeDtypeStruct((p_rows, d), F32),
        grid_spec=pltpu.PrefetchScalarGridSpec(
            num_scalar_prefetch=2,
            grid=(p_rows // MOE_TM,),
            in_specs=[pl.BlockSpec((MOE_TM, dffe), lambda i, te, nu: (i, 0)),
                      pl.BlockSpec((1, 1, dffe, d), lambda i, te, nu: (layer, te[i], 0, 0))],
            out_specs=pl.BlockSpec((MOE_TM, d), lambda i, te, nu: (i, 0))),
        compiler_params=_cparams("parallel"),
        name="moe_down",
    )(tile_expert, n_used, act, w2b)


def _moe_combine_kernel(x_ref, gate_ref, g_ref, b_ref, tw_ref, y1_ref, y2_ref, o_ref, *, alpha):
    tw = tw_ref[...]
    y = tw[:, 0:1] * y1_ref[...] + tw[:, 1:2] * y2_ref[...]
    o_ref[...] = _res_ln(x_ref[...], gate_ref[0], y, g_ref[...], b_ref[...], alpha)


def _moe_combine(x2, gate, lng, lnb, tw, yg, seq, tm, alpha):
    n, d = x2.shape
    per_b = seq // tm
    nt = n // tm
    row = lambda i: (i, 0)
    const = lambda i: (0, 0)
    return pl.pallas_call(
        functools.partial(_moe_combine_kernel, alpha=alpha),
        out_shape=jax.ShapeDtypeStruct((n, d), F32),
        grid=(nt,),
        in_specs=[pl.BlockSpec((tm, d), row), pl.BlockSpec((1, 1, d), lambda i: (i // per_b, 0, 0)),
                  pl.BlockSpec((1, d), const), pl.BlockSpec((1, d), const), pl.BlockSpec((tm, LANE), row),
                  pl.BlockSpec((tm, d), row), pl.BlockSpec((tm, d), lambda i: (nt + i, 0))],
        out_specs=pl.BlockSpec((tm, d), row),
        compiler_params=_cparams("parallel"),
        name="moe_combine_ln",
    )(x2, gate, lng, lnb, tw, yg, yg)


def _route_tables(ti, n_rows):
    e_flat = jnp.concatenate([ti[:, 0], ti[:, 1]])
    onehot = (e_flat[:, None] == jnp.arange(N_EXPERTS, dtype=jnp.int32)[None, :]).astype(jnp.int32)
    cum = jnp.cumsum(onehot, axis=0)
    counts = cum[-1]
    padded = (counts + MOE_TM - 1) // MOE_TM * MOE_TM
    ends = jnp.cumsum(padded)
    pos = jnp.sum(onehot * (cum - 1 + (ends - padded)[None, :]), axis=1)
    tile_start = jnp.arange(n_rows // MOE_TM, dtype=jnp.int32) * MOE_TM
    tile_expert = jnp.minimum(jnp.sum((tile_start[:, None] >= ends[None, :]).astype(jnp.int32), axis=1),
                              N_EXPERTS - 1)
    return pos.astype(jnp.int32), tile_expert, (ends[-1:] // MOE_TM).astype(jnp.int32)


def _moe(x2, scale, shift, gate, lng, lnb, router_p, w1b, w2b, layer, seq, tm, alpha):
    n, d = x2.shape
    n_rows = 2 * n + N_EXPERTS * MOE_TM
    assert n % GATHER_ROWS == 0
    h, ti, tw = _router(x2, scale, shift, router_p, seq, tm)
    pos, tile_expert, n_used = _route_tables(ti, n_rows)
    xg = _scatter_rows(pos, h, n_rows)
    act = _expert_up(tile_expert, n_used, xg, w1b, layer)
    y = _expert_down(tile_expert, n_used, act, w2b, layer)
    yg = _gather_rows(pos, y)
    return _moe_combine(x2, gate, lng, lnb, tw, yg, seq, tm, alpha)


def _rel_bucket_np(dist):
    n = np.maximum(dist, 0)
    max_exact = NUM_BUCKETS // 2
    nf = np.maximum(n, 1).astype(np.float32)
    large = max_exact + (np.log(nf / np.float32(max_exact)) / np.float32(math.log(MAX_DISTANCE / max_exact))
                         * np.float32(NUM_BUCKETS - max_exact)).astype(np.int32)
    large = np.minimum(large, NUM_BUCKETS - 1)
    return np.where(n < max_exact, n, large).astype(np.int32)


def _bias_lookup(rel_bias, dist, valid):
    onehot = _rel_bucket_np(dist)[..., None] == np.arange(NUM_BUCKETS)
    vals = jnp.einsum('...k,kh->...h', jnp.asarray(onehot).astype(F32), rel_bias, precision=HIGHEST)
    return jnp.where(jnp.asarray(valid)[..., None], vals * LOG2E, NEG_INF)


def _key_major(vals, add_masked_tile=False):
    n_tiles, tq, tk, _ = vals.shape
    if add_masked_tile:
        vals = jnp.concatenate([vals, jnp.full((1,) + vals.shape[1:], NEG_INF, vals.dtype)], axis=0)
        n_tiles += 1
    vals = vals.reshape(n_tiles, tq, tk, N_KV, Q_PER_KV).transpose(3, 0, 2, 4, 1)
    return vals.reshape(N_KV, n_tiles, tk, Q_PER_KV * tq).astype(F32)


def _band_dist(n_delta):
    i = np.arange(ATT_TQ)[None, :, None]
    k = np.arange(ATT_TQ)[None, None, :]
    return ATT_TQ * np.arange(n_delta)[:, None, None] + i - k


def _cmp_bias_table(rel_bias, n_qt, n_cmp_pad):
    per_tile = ATT_TQ // CMP_STRIDE
    shift = per_tile * (n_qt - 1)
    rel_n = np.arange(n_cmp_pad + shift) - shift
    dist = np.arange(ATT_TQ)[:, None] - (CMP_STRIDE * rel_n[None, :] + L_CMP - 1)
    base = _bias_lookup(rel_bias, dist, dist >= 0)
    tiles = [base[:, shift - per_tile * qt: shift - per_tile * qt + n_cmp_pad] for qt in range(n_qt)]
    return _key_major(jnp.stack(tiles))


def _inproj_weight(w_in):
    depth, d, _ = w_in.shape
    d_nsa = N_HEADS * HEAD_DIM
    d_kv = N_KV * HEAD_DIM
    base_ks = d_nsa + 2 * d_kv
    base_gl = d_nsa + 6 * d_kv
    base_cv = base_gl + N_HEADS * N_BRANCH
    wq = jnp.pad(w_in[:, :, :d_nsa].reshape(depth, d, N_HEADS, HEAD_DIM),
                 ((0, 0), (0, 0), (0, 0), (0, LANE - HEAD_DIM))).reshape(depth, d, Q_COLS)
    parts = [wq, w_in[:, :, d_nsa:base_ks]]
    for pair in range(2):
        kbase = base_ks + pair * 2 * d_kv
        for g in range(N_KV):
            parts.append(w_in[:, :, kbase + g * HEAD_DIM: kbase + (g + 1) * HEAD_DIM])
            parts.append(w_in[:, :, kbase + d_kv + g * HEAD_DIM: kbase + d_kv + (g + 1) * HEAD_DIM])
    parts.append(w_in[:, :, base_cv:])
    gl = w_in[:, :, base_gl:base_cv].reshape(depth, d, N_KV, Q_PER_KV, N_BRANCH).transpose(0, 2, 4, 3, 1)
    gl = jnp.pad(gl.reshape(depth, N_KV, N_BRANCH * Q_PER_KV, d),
                 ((0, 0), (0, 0), (0, LANE - N_BRANCH * Q_PER_KV), (0, 0)))
    return jnp.concatenate(parts, axis=-1).astype(BF16), gl.reshape(depth, GL_ROWS, d).astype(BF16)


def kernel(x, c, w_in, cmp_pos, cmp_w1, cmp_w2, conv_w, conv_b, conv_ln_g, conv_ln_b, w_out, rel_bias,
           ada_w, ada_b, ln_g, ln_b, ffn_w1, ffn_w2, router_w, moe_w1, moe_w2):
    bsz, seq, d = x.shape
    depth = w_in.shape[0]
    n = bsz * seq
    alpha = (2 * depth) ** 0.25
    n_blocks = seq // L_SEL
    n_sel = min(N_SEL, n_blocks)
    n_cmp = (seq - L_CMP) // CMP_STRIDE + 1
    nrow = seq // CMP_STRIDE
    assert d == 1024 and seq % 512 == 0 and nrow == LANE and n_cmp == nrow - 1
    tm = 512

    n_qt = seq // ATT_TQ
    d_win = _band_dist(WINDOW // ATT_TQ + 1)
    win_tab = _key_major(_bias_lookup(rel_bias, d_win, (d_win >= 0) & (d_win < WINDOW)), add_masked_tile=True)
    d_sel = _band_dist(3)
    assert _rel_bucket_np(d_sel[2]).min() == NUM_BUCKETS - 1
    sel_tab = _key_major(_bias_lookup(rel_bias, d_sel, d_sel >= 0), add_masked_tile=True)
    cmp_tab = _cmp_bias_table(rel_bias, n_qt, nrow)

    w_in_p, w_gl_t = _inproj_weight(w_in)
    eye2 = jnp.eye(2, dtype=F32)
    w1r = cmp_w1.reshape(depth, 2, L_CMP, HEAD_DIM, HEAD_DIM)
    w1x = jnp.einsum('zslde,st,gh->zlsgdhte', w1r, eye2, eye2).reshape(depth, L_CMP, 4 * HEAD_DIM, 4 * HEAD_DIM)
    half = L_CMP // 2
    w_lo = w1x[:, :half].astype(BF16)
    w_hi = w1x[:, half:].astype(BF16)
    w2bd = jnp.einsum('zsde,st,gh->zgsdhte', cmp_w2, eye2, eye2).reshape(depth, 4 * HEAD_DIM, 4 * HEAD_DIM).astype(BF16)
    posx = jnp.broadcast_to(cmp_pos[:, :, :, None, :], (depth, 2, L_CMP, N_KV, HEAD_DIM)).transpose(0, 2, 1, 3, 4)
    pos_lo = posx[:, :half].reshape(depth, half, 4 * HEAD_DIM)
    pos_hi = posx[:, half:].reshape(depth, half, 4 * HEAD_DIM)
    conv_wp = jnp.broadcast_to(conv_w[:, :, None, :], conv_w.shape[:2] + (SUBLANE, conv_w.shape[2]))
    w_out_b = w_out.astype(BF16)
    dff = ffn_w2.shape[1]
    ffn_wg = ffn_w1[:, :, :dff].astype(BF16)
    ffn_wu = ffn_w1[:, :, dff:].astype(BF16)
    ffn_w2b = ffn_w2.astype(BF16)
    router_p = jnp.pad(router_w, ((0, 0), (0, 0), (0, LANE - N_EXPERTS)))
    moe_w1b = moe_w1.astype(BF16)
    moe_w2b = moe_w2.astype(BF16)

    mod = _ada(c, ada_w, ada_b).reshape(depth, 6, bsz, 1, d)
    x2 = x.reshape(n, d)
    for l in range(depth):
        shift1, scale1, gate1, shift2, scale2, gate2 = [mod[l, k] for k in range(6)]
        q, kvc, kvsw, gl3, cv = _inproj(x2, scale1, shift1, w_in_p[l], w_gl_t[l], seq, tm)
        q3 = q.reshape(bsz, seq, Q_COLS)
        kvsw3 = kvsw.reshape(bsz, seq, KVSW_COLS)
        kvsw_t = kvsw3.reshape(bsz, n_qt, ATT_TQ, 2 * N_KV, LANE).transpose(0, 3, 1, 4, 2)
        kvcmp, kvcmp_t = _compress(kvc.reshape(bsz, seq, KVC_COLS), pos_lo[l], pos_hi[l], w_lo[l], w_hi[l],
                                   w2bd[l])
        o_nsa = _nsa_attn(q3, kvcmp, kvcmp_t, cmp_tab, kvsw3, kvsw_t, sel_tab, win_tab, gl3, n_blocks, n_sel)
        o_conv = _conv_branch(cv.reshape(bsz, seq, 1024), conv_wp[l], conv_b[l][None], conv_ln_g[l][None],
                              conv_ln_b[l][None])
        x2 = _outproj(x2, gate1, ln_g[l, 0][None], ln_b[l, 0][None], o_nsa.reshape(n, 512), o_conv.reshape(n, 512),
                      w_out_b[l], seq, tm, alpha)
        if l % 2 == 0:
            act = _ffn1(x2, scale2, shift2, ffn_wg[l // 2], ffn_wu[l // 2], seq, tm)
            x2 = _ffn2(x2, gate2, ln_g[l, 1][None], ln_b[l, 1][None], act, ffn_w2b[l // 2], seq, tm, alpha)
        else:
            x2 = _moe(x2, scale2, shift2, gate2, ln_g[l, 1][None], ln_b[l, 1][None], router_p[l // 2],
                      moe_w1b, moe_w2b, l // 2, seq, tm, alpha)
    return x2.reshape(bsz, seq, d)
```

```python
import functools
import math

import numpy as np
import jax
import jax.numpy as jnp
from jax import lax
from jax.experimental import pallas as pl
from jax.experimental.pallas import tpu as pltpu

F32 = jnp.float32
BF16 = jnp.bfloat16
HIGHEST = lax.Precision.HIGHEST

HEAD_DIM = 64
N_HEADS = 8
N_KV = 2
Q_PER_KV = N_HEADS // N_KV
N_BRANCH = 3
L_CMP = 32
CMP_STRIDE = 16
L_SEL = 64
N_SEL = 8
WINDOW = 512
CONV_WIDTH = 31
NUM_BUCKETS = 32
MAX_DISTANCE = 128
N_EXPERTS = 8
LN_EPS = 1e-5
NEG_INF = -1e30
LOG2E = math.log2(math.e)

LANE = 128
SUBLANE = 8
ATT_TQ = 128
ATT_ROWS = Q_PER_KV * ATT_TQ
CONV_TQ = 256
CONV_HALO = 32
CONV_RC = 64
VMEM_LIMIT_BYTES = 52 * 1024 * 1024


def _cparams(*sem):
    return pltpu.CompilerParams(dimension_semantics=sem, vmem_limit_bytes=VMEM_LIMIT_BYTES)


def _dot(a, b):
    return jnp.dot(a, b, preferred_element_type=F32)


def _dot_nt(a, b, precision=None):
    return lax.dot_general(a, b, (((1,), (1,)), ((), ())), precision=precision,
                           preferred_element_type=F32)


def _res_ln(x, gate, y, g, b, alpha):
    z = alpha * x + (1.0 + gate) * y
    mu = jnp.mean(z, axis=-1, keepdims=True)
    zc = z - mu
    var = jnp.mean(zc * zc, axis=-1, keepdims=True)
    return zc * lax.rsqrt(var + LN_EPS) * g + b


def _ada_kernel(c_ref, w_ref, b_ref, o_ref):
    sc = jax.nn.silu(c_ref[...])
    o_ref[0, 0] = jnp.dot(sc, w_ref[0], precision=HIGHEST, preferred_element_type=F32) + b_ref[0]


def _ada(c, ada_w, ada_b):
    depth, d, _ = ada_w.shape
    bsz = c.shape[0]
    return pl.pallas_call(
        _ada_kernel,
        out_shape=jax.ShapeDtypeStruct((depth, 6, bsz, d), F32),
        grid=(depth, 6),
        in_specs=[pl.BlockSpec((bsz, d), lambda l, k: (0, 0)),
                  pl.BlockSpec((1, d, d), lambda l, k: (l, 0, k)),
                  pl.BlockSpec((1, 1, d), lambda l, k: (l, 0, k))],
        out_specs=pl.BlockSpec((1, 1, bsz, d), lambda l, k: (l, k, 0, 0)),
        compiler_params=_cparams("parallel", "parallel"),
        name="ada_mod",
    )(c, ada_w, ada_b.reshape(depth, 1, 6 * d))


Q_COLS = N_HEADS * LANE
KVC_COLS = 4 * HEAD_DIM
KVSW_COLS = 8 * HEAD_DIM
GL_ROWS = N_KV * LANE


def _inproj_kernel(x_ref, sc_ref, sh_ref, w_ref, wgl_ref, q_ref, kvc_ref, kvsw_ref, glt_ref, cv_ref):
    h = (x_ref[...] * (1.0 + sc_ref[0]) + sh_ref[0]).astype(BF16)
    c0, c1, c2 = Q_COLS, Q_COLS + KVC_COLS, Q_COLS + KVC_COLS + KVSW_COLS
    q_ref[...] = (_dot(h, w_ref[:, 0:c0]) * (HEAD_DIM ** -0.5 * LOG2E)).astype(BF16)
    kvc_ref[...] = _dot(h, w_ref[:, c0:c1])
    kvsw_ref[...] = _dot(h, w_ref[:, c1:c2]).astype(BF16)
    cv_ref[...] = _dot(h, w_ref[:, c2:])
    glt_ref[...] = _dot_nt(wgl_ref[...], h)


def _inproj(x2, scale, shift, w, wgl_t, seq, tm):
    n, d = x2.shape
    per_b = seq // tm
    d_cv = w.shape[1] - (Q_COLS + KVC_COLS + KVSW_COLS)
    row = lambda i: (i, 0)
    mod = lambda i: (i // per_b, 0, 0)
    const = lambda i: (0, 0)
    return pl.pallas_call(
        _inproj_kernel,
        out_shape=(jax.ShapeDtypeStruct((n, Q_COLS), BF16), jax.ShapeDtypeStruct((n, KVC_COLS), F32),
                   jax.ShapeDtypeStruct((n, KVSW_COLS), BF16), jax.ShapeDtypeStruct((GL_ROWS, n), F32),
                   jax.ShapeDtypeStruct((n, d_cv), F32)),
        grid=(n // tm,),
        in_specs=[pl.BlockSpec((tm, d), row), pl.BlockSpec((1, 1, d), mod), pl.BlockSpec((1, 1, d), mod),
                  pl.BlockSpec(w.shape, const), pl.BlockSpec(wgl_t.shape, const)],
        out_specs=(pl.BlockSpec((tm, Q_COLS), row), pl.BlockSpec((tm, KVC_COLS), row),
                   pl.BlockSpec((tm, KVSW_COLS), row), pl.BlockSpec((GL_ROWS, tm), lambda i: (0, i)),
                   pl.BlockSpec((tm, d_cv), row)),
        compiler_params=_cparams("parallel"),
        name="in_proj",
    )(x2, scale, shift, w, wgl_t)


def _compress_kernel(ak_ref, av_ref, plo_ref, phi_ref, wlo_ref, whi_ref, w2_ref, o_ref, ot_ref):
    nrow = o_ref.shape[1]
    lo = jnp.zeros((nrow, o_ref.shape[2]), F32)
    hi = jnp.zeros((nrow, o_ref.shape[2]), F32)
    for l in range(CMP_STRIDE):
        for part, a_ref in enumerate((ak_ref, av_ref)):
            lanes = slice(part * LANE, (part + 1) * LANE)
            tok = a_ref[0, pl.ds(l, nrow, stride=CMP_STRIDE), :]
            lo = lo + _dot((tok + plo_ref[l:l + 1, lanes]).astype(BF16), wlo_ref[l, lanes, :])
            hi = hi + _dot((tok + phi_ref[l:l + 1, lanes]).astype(BF16), whi_ref[l, lanes, :])
    pre = lo + pltpu.roll(hi, nrow - 1, 0)
    act = jax.nn.gelu(pre, approximate=True)
    out = _dot(act.astype(BF16), w2_ref[...])
    o_ref[0] = out.astype(BF16)
    ot_ref[0] = out.T.astype(BF16)


def _compress(kvc3, pos_lo, pos_hi, wlo, whi, w2bd):
    bsz, seq, width = kvc3.shape
    nrow = seq // CMP_STRIDE
    const = lambda b: (0, 0)
    const3 = lambda b: (0, 0, 0)
    return pl.pallas_call(
        _compress_kernel,
        out_shape=(jax.ShapeDtypeStruct((bsz, nrow, 256), BF16), jax.ShapeDtypeStruct((bsz, 256, nrow), BF16)),
        grid=(bsz,),
        in_specs=[pl.BlockSpec((1, seq, LANE), lambda b: (b, 0, 0)), pl.BlockSpec((1, seq, LANE), lambda b: (b, 0, 1)),
                  pl.BlockSpec(pos_lo.shape, const), pl.BlockSpec(pos_hi.shape, const),
                  pl.BlockSpec(wlo.shape, const3), pl.BlockSpec(whi.shape, const3),
                  pl.BlockSpec((256, 256), const)],
        out_specs=(pl.BlockSpec((1, nrow, 256), lambda b: (b, 0, 0)),
                   pl.BlockSpec((1, 256, nrow), lambda b: (b, 0, 0))),
        compiler_params=_cparams("parallel"),
        name="compress_mlp",
    )(kvc3, kvc3, pos_lo, pos_hi, wlo, whi, w2bd)


def _stack_heads(q):
    return jnp.concatenate([q[:, r * LANE:(r + 1) * LANE] for r in range(Q_PER_KV)], axis=0)


def _branch_gates(gl_t, branch):
    return jnp.concatenate([jax.nn.sigmoid(gl_t[branch * Q_PER_KV + r:branch * Q_PER_KV + r + 1, :])
                            for r in range(Q_PER_KV)], axis=1)


def _heads_to_rows(o_t):
    tq = o_t.shape[1] // Q_PER_KV
    og = o_t.astype(BF16)
    ii = lax.broadcasted_iota(jnp.int32, (tq, tq), 0)
    kk = lax.broadcasted_iota(jnp.int32, (tq, tq), 1)
    eye = jnp.where(ii == kk, 1.0, 0.0).astype(BF16)
    halves = []
    for pair in range(Q_PER_KV // 2):
        two_heads = jnp.concatenate([og[:, (2 * pair) * tq:(2 * pair + 1) * tq],
                                     og[:, (2 * pair + 1) * tq:(2 * pair + 2) * tq]], axis=0)
        halves.append(_dot_nt(eye, two_heads))
    return jnp.concatenate(halves, axis=1).astype(BF16)


def _cmp_section(qs, qt, kv_ref, kvt_ref, tab_ref, gl_ref, sel_sc, *, n_blocks, n_sel):
    tq = ATT_TQ
    outs = []
    jj = lax.broadcasted_iota(jnp.int32, (n_blocks, LANE), 0)
    nn = lax.broadcasted_iota(jnp.int32, (n_blocks, LANE), 1)
    overlap_t = jnp.where((CMP_STRIDE * nn < L_SEL * jj + L_SEL) & (CMP_STRIDE * nn + L_CMP > L_SEL * jj),
                          1.0, 0.0).astype(F32)
    jb = lax.broadcasted_iota(jnp.int32, (n_blocks, tq), 0)
    tb = (qt * tq + lax.broadcasted_iota(jnp.int32, (n_blocks, tq), 1)) // L_SEL
    for g in range(N_KV):
        bias = tab_ref[g, 0]
        s = _dot_nt(kv_ref[0, :, g * LANE:(g + 1) * LANE], qs[g]) + bias
        valid = bias > 0.5 * NEG_INF
        m = jnp.max(s, axis=0, keepdims=True)
        p = jnp.where(valid, jnp.exp2(s - m), 0.0)
        l = jnp.sum(p, axis=0, keepdims=True)
        p = p * (1.0 / jnp.where(l > 0.0, l, 1.0))
        o_t = _dot(kvt_ref[0, g * LANE + HEAD_DIM:(g + 1) * LANE, :], p.astype(BF16))
        outs.append(o_t * _branch_gates(gl_ref[g * LANE:(g + 1) * LANE, :], 0))

        psum = p[:, 0:tq] + p[:, tq:2 * tq] + p[:, 2 * tq:3 * tq] + p[:, 3 * tq:4 * tq]
        imp_t = jnp.dot(overlap_t, psum, precision=HIGHEST, preferred_element_type=F32)
        score = jnp.where(jb > tb, -jnp.inf,
                          jnp.where((jb == 0) | (jb == tb) | (jb == tb - 1), jnp.inf, imp_t))
        rank = jnp.zeros((n_blocks, tq), F32)
        for jp in range(n_blocks):
            row = score[jp:jp + 1, :]
            earlier = jnp.where(jb > jp, 1.0, 0.0)
            rank = rank + jnp.where(row > score, 1.0, 0.0) + jnp.where(row == score, earlier, 0.0)
        madd = jnp.where(rank < float(n_sel), 0.0, NEG_INF)
        per_chunk = ATT_TQ // L_SEL
        for c in range(n_blocks // per_chunk):
            sel_sc[g, c, 0:per_chunk, :] = madd[c * per_chunk:(c + 1) * per_chunk, :]
    return outs


def _band_section(qs, qt, kv_ref, kvt_ref, tab_ref, gl_ref, sel_sc, s_sc, *, span, group, branch):
    tq = ATT_TQ
    n_delta = tab_ref.shape[1] - 1
    n_chunks = kv_ref.shape[1] // tq
    c_base = qt - span if span is not None else 0
    use_sel = sel_sc is not None

    def chunk_ids(gi, u):
        c = c_base + gi * group + u
        inside = (c >= 0) & (c <= qt)
        dd = jnp.where(inside, jnp.minimum(qt - c, n_delta - 1), n_delta)
        return jnp.clip(c, 0, n_chunks - 1), dd

    def rows(gi, u):
        return pl.ds(pl.multiple_of((gi * group + u) * tq, tq), tq)

    def scores(gi, ms):
        ms = list(ms)
        for u in range(group):
            cc, dd = chunk_ids(gi, u)
            kv = kv_ref[0, pl.ds(pl.multiple_of(cc * tq, tq), tq), :]
            for g in range(N_KV):
                s = _dot_nt(kv[:, g * LANE:(g + 1) * LANE], qs[g]) + tab_ref[g, dd]
                if use_sel:
                    blocks = sel_sc[g, cc, 0:tq // L_SEL, :]
                    madd = jnp.concatenate([jnp.broadcast_to(blocks[j:j + 1, :], (L_SEL, tq))
                                            for j in range(tq // L_SEL)], axis=0)
                    s = s + jnp.concatenate([madd] * Q_PER_KV, axis=1)
                s_sc[g, rows(gi, u), :] = s
                ms[g] = jnp.maximum(ms[g], jnp.max(s, axis=0, keepdims=True))
        return tuple(ms)

    m0 = tuple(jnp.full((1, ATT_ROWS), NEG_INF, F32) for _ in range(N_KV))
    if span is not None:
        n_groups = 1
        m = scores(0, m0)
    else:
        n_groups = qt // group + 1
        m = lax.fori_loop(0, n_groups, scores, m0)

    def values(gi, carry):
        carry = [list(c) for c in carry]
        for u in range(group):
            cc, _ = chunk_ids(gi, u)
            for g in range(N_KV):
                l, acc = carry[g]
                p = jnp.exp2(s_sc[g, rows(gi, u), :] - m[g])
                l = l + jnp.sum(p, axis=0, keepdims=True)
                acc = acc + _dot(kvt_ref[0, g, cc, HEAD_DIM:, :], p.astype(BF16))
                carry[g] = [l, acc]
        return tuple(tuple(c) for c in carry)

    init = tuple((jnp.zeros((1, ATT_ROWS), F32), jnp.zeros((HEAD_DIM, ATT_ROWS), F32)) for _ in range(N_KV))
    if span is not None:
        out = values(0, init)
    else:
        out = lax.fori_loop(0, n_groups, values, init)
    return [out[g][1] * (_branch_gates(gl_ref[g * LANE:(g + 1) * LANE, :], branch) / out[g][0]) for g in range(N_KV)]


SEL_GROUP = 4
WIN_SPAN = WINDOW // ATT_TQ


def _nsa_kernel(q_ref, kvc_ref, kvct_ref, ctab_ref, kvs_ref, kvst_ref, stab_ref, kvw_ref, kvwt_ref, wtab_ref,
                gl_ref, o_ref, sel_sc, ssel_sc, swin_sc, *, n_blocks, n_sel):
    qt = pl.program_id(1)
    hw = Q_PER_KV * LANE
    qs = [_stack_heads(q_ref[0, :, g * hw:(g + 1) * hw]) for g in range(N_KV)]
    o_cmp = _cmp_section(qs, qt, kvc_ref, kvct_ref, ctab_ref, gl_ref, sel_sc, n_blocks=n_blocks, n_sel=n_sel)
    o_sel = _band_section(qs, qt, kvs_ref, kvst_ref, stab_ref, gl_ref, sel_sc, ssel_sc,
                          span=None, group=SEL_GROUP, branch=1)
    o_win = _band_section(qs, qt, kvw_ref, kvwt_ref, wtab_ref, gl_ref, None, swin_sc,
                          span=WIN_SPAN, group=WIN_SPAN + 1, branch=2)
    for g in range(N_KV):
        o_ref[0, :, g * 256:(g + 1) * 256] = _heads_to_rows(o_cmp[g] + o_sel[g] + o_win[g])


def _nsa_attn(q3, kvcmp, kvcmp_t, cmp_tab, kvsw3, kvsw_t, sel_tab, win_tab, gl_t, n_blocks, n_sel):
    bsz, seq, _ = q3.shape
    nqt = seq // ATT_TQ
    n_cmp = kvcmp.shape[1]
    sel_slots = -(-nqt // SEL_GROUP) * SEL_GROUP
    full4 = lambda b, t: (0, 0, 0, 0)
    kv_spec = lambda pair: pl.BlockSpec((1, seq, N_KV * LANE), lambda b, t: (b, 0, pair))
    kvt_spec = lambda pair: pl.BlockSpec((1, N_KV) + kvsw_t.shape[2:], lambda b, t: (b, pair, 0, 0, 0))
    return pl.pallas_call(
        functools.partial(_nsa_kernel, n_blocks=n_blocks, n_sel=n_sel),
        out_shape=jax.ShapeDtypeStruct((bsz, seq, 512), BF16),
        grid=(bsz, nqt),
        in_specs=[pl.BlockSpec((1, ATT_TQ, Q_COLS), lambda b, t: (b, t, 0)),
                  pl.BlockSpec((1, n_cmp, N_KV * LANE), lambda b, t: (b, 0, 0)),
                  pl.BlockSpec((1, N_KV * LANE, n_cmp), lambda b, t: (b, 0, 0)),
                  pl.BlockSpec((N_KV, 1, n_cmp, ATT_ROWS), lambda b, t: (0, t, 0, 0)),
                  kv_spec(0), kvt_spec(0), pl.BlockSpec(sel_tab.shape, full4),
                  kv_spec(1), kvt_spec(1), pl.BlockSpec(win_tab.shape, full4),
                  pl.BlockSpec((GL_ROWS, ATT_TQ), lambda b, t: (0, b * nqt + t))],
        out_specs=pl.BlockSpec((1, ATT_TQ, 512), lambda b, t: (b, t, 0)),
        scratch_shapes=[pltpu.VMEM((N_KV, n_blocks * L_SEL // ATT_TQ, SUBLANE, ATT_TQ), F32),
                        pltpu.VMEM((N_KV, sel_slots * ATT_TQ, ATT_ROWS), F32),
                        pltpu.VMEM((N_KV, (WIN_SPAN + 1) * ATT_TQ, ATT_ROWS), F32)],
        compiler_params=_cparams("parallel", "parallel"),
        name="nsa_attn",
    )(q3, kvcmp, kvcmp_t, cmp_tab, kvsw3, kvsw_t, sel_tab, kvsw3, kvsw_t, win_tab, gl_t)


def _conv_kernel(cur_ref, prev_ref, w_ref, cb_ref, g_ref, b_ref, o_ref, ybuf, ysh, cbuf):
    i = pl.program_id(1)
    dc = o_ref.shape[-1]
    cur = cur_ref[0]
    prev = prev_ref[0]
    ybuf[CONV_HALO:, :] = cur[:, :dc] * jax.nn.sigmoid(cur[:, dc:])
    halo = prev[:, :dc] * jax.nn.sigmoid(prev[:, dc:])
    ybuf[0:CONV_HALO, :] = jnp.where(i > 0, halo, 0.0)
    n_sh = ysh.shape[1]
    for b in range(1, SUBLANE):
        ysh[b - 1] = ybuf[b:b + n_sh, :]
    off = CONV_HALO - (CONV_WIDTH - 1)
    sub = CONV_RC // SUBLANE
    n_rc = CONV_TQ // CONV_RC
    for lc in range(dc // LANE):
        cols = slice(lc * LANE, (lc + 1) * LANE)
        accs = [jnp.zeros((sub, SUBLANE, LANE), F32) for _ in range(n_rc)]
        for k in range(CONV_WIDTH):
            b = (off + k) % SUBLANE
            wk = w_ref[k, :, cols][None]
            for rc in range(n_rc):
                r0 = rc * CONV_RC + (off + k) - b
                src = ybuf[r0:r0 + CONV_RC, cols] if b == 0 else ysh[b - 1, r0:r0 + CONV_RC, cols]
                accs[rc] = accs[rc] + wk * src.reshape(sub, SUBLANE, LANE)
        for rc in range(n_rc):
            cbuf[rc * CONV_RC:(rc + 1) * CONV_RC, cols] = accs[rc].reshape(CONV_RC, LANE)
    z = cbuf[...] + cb_ref[...]
    mu = jnp.mean(z, axis=-1, keepdims=True)
    zc = z - mu
    var = jnp.mean(zc * zc, axis=-1, keepdims=True)
    y = zc * lax.rsqrt(var + LN_EPS) * g_ref[...] + b_ref[...]
    o_ref[0] = jax.nn.silu(y).astype(BF16)


def _conv_branch(cv3, w, cb, g, b):
    bsz, seq, two_dc = cv3.shape
    dc = two_dc // 2
    per = CONV_TQ // CONV_HALO
    const = lambda bb, i: (0, 0)
    return pl.pallas_call(
        _conv_kernel,
        out_shape=jax.ShapeDtypeStruct((bsz, seq, dc), BF16),
        grid=(bsz, seq // CONV_TQ),
        in_specs=[pl.BlockSpec((1, CONV_TQ, two_dc), lambda bb, i: (bb, i, 0)),
                  pl.BlockSpec((1, CONV_HALO, two_dc), lambda bb, i: (bb, jnp.maximum(i * per - 1, 0), 0)),
                  pl.BlockSpec(w.shape, lambda bb, i: (0, 0, 0)), pl.BlockSpec((1, dc), const),
                  pl.BlockSpec((1, dc), const), pl.BlockSpec((1, dc), const)],
        out_specs=pl.BlockSpec((1, CONV_TQ, dc), lambda bb, i: (bb, i, 0)),
        scratch_shapes=[pltpu.VMEM((CONV_HALO + CONV_TQ, dc), F32),
                        pltpu.VMEM((SUBLANE - 1, CONV_HALO + CONV_TQ - SUBLANE, dc), F32),
                        pltpu.VMEM((CONV_TQ, dc), F32)],
        compiler_params=_cparams("parallel", "parallel"),
        name="conv_branch",
    )(cv3, cv3, w, cb, g, b)


def _outproj_kernel(x_ref, gate_ref, g_ref, b_ref, nsa_ref, ocv_ref, w_ref, o_ref, *, alpha):
    half = nsa_ref.shape[1]
    y = _dot(nsa_ref[...], w_ref[0:half, :]) + _dot(ocv_ref[...], w_ref[half:, :])
    o_ref[...] = _res_ln(x_ref[...], gate_ref[0], y, g_ref[...], b_ref[...], alpha)


def _outproj(x2, gate, lng, lnb, o_nsa, o_conv, w, seq, tm, alpha):
    n, d = x2.shape
    per_b = seq // tm
    row = lambda i: (i, 0)
    const = lambda i: (0, 0)
    return pl.pallas_call(
        functools.partial(_outproj_kernel, alpha=alpha),
        out_shape=jax.ShapeDtypeStruct((n, d), F32),
        grid=(n // tm,),
        in_specs=[pl.BlockSpec((tm, d), row), pl.BlockSpec((1, 1, d), lambda i: (i // per_b, 0, 0)),
                  pl.BlockSpec((1, d), const), pl.BlockSpec((1, d), const),
                  pl.BlockSpec((tm, o_nsa.shape[1]), row), pl.BlockSpec((tm, o_conv.shape[1]), row),
                  pl.BlockSpec(w.shape, const)],
        out_specs=pl.BlockSpec((tm, d), row),
        compiler_params=_cparams("parallel"),
        name="out_proj_ln",
    )(x2, gate, lng, lnb, o_nsa, o_conv, w)


def _ffn1_kernel(x_ref, sc_ref, sh_ref, wg_ref, wu_ref, o_ref, *, tn):
    h = (x_ref[...] * (1.0 + sc_ref[0]) + sh_ref[0]).astype(BF16)
    for j in range(o_ref.shape[1] // tn):
        cols = slice(j * tn, (j + 1) * tn)
        gate = _dot(h, wg_ref[:, cols])
        up = _dot(h, wu_ref[:, cols])
        o_ref[:, cols] = (jax.nn.silu(gate) * up).astype(BF16)


def _ffn1(x2, scale, shift, wg, wu, seq, tm):
    n, d = x2.shape
    dff = wg.shape[1]
    per_b = seq // tm
    row = lambda i: (i, 0)
    mod = lambda i: (i // per_b, 0, 0)
    const = lambda i: (0, 0)
    return pl.pallas_call(
        functools.partial(_ffn1_kernel, tn=256),
        out_shape=jax.ShapeDtypeStruct((n, dff), BF16),
        grid=(n // tm,),
        in_specs=[pl.BlockSpec((tm, d), row), pl.BlockSpec((1, 1, d), mod), pl.BlockSpec((1, 1, d), mod),
                  pl.BlockSpec((d, dff), const), pl.BlockSpec((d, dff), const)],
        out_specs=pl.BlockSpec((tm, dff), row),
        compiler_params=_cparams("parallel"),
        name="ffn_up",
    )(x2, scale, shift, wg, wu)


def _ffn2_kernel(x_ref, gate_ref, g_ref, b_ref, a_ref, w_ref, o_ref, *, alpha):
    y = _dot(a_ref[...], w_ref[...])
    o_ref[...] = _res_ln(x_ref[...], gate_ref[0], y, g_ref[...], b_ref[...], alpha)


def _ffn2(x2, gate, lng, lnb, act, w, seq, tm, alpha):
    n, d = x2.shape
    per_b = seq // tm
    row = lambda i: (i, 0)
    const = lambda i: (0, 0)
    return pl.pallas_call(
        functools.partial(_ffn2_kernel, alpha=alpha),
        out_shape=jax.ShapeDtypeStruct((n, d), F32),
        grid=(n // tm,),
        in_specs=[pl.BlockSpec((tm, d), row), pl.BlockSpec((1, 1, d), lambda i: (i // per_b, 0, 0)),
                  pl.BlockSpec((1, d), const), pl.BlockSpec((1, d), const),
                  pl.BlockSpec((tm, act.shape[1]), row), pl.BlockSpec(w.shape, const)],
        out_specs=pl.BlockSpec((tm, d), row),
        compiler_params=_cparams("parallel"),
        name="ffn_down_ln",
    )(x2, gate, lng, lnb, act, w)


MOE_TM = 1024
MOE_FC = 512
GATHER_ROWS = 1024


def _router_kernel(x_ref, sc_ref, sh_ref, rw_ref, h_ref, ti_ref, tw_ref):
    hf = x_ref[...] * (1.0 + sc_ref[0]) + sh_ref[0]
    h_ref[...] = hf
    lane = lax.broadcasted_iota(jnp.int32, (hf.shape[0], LANE), 1)
    logits = jnp.dot(hf, rw_ref[...], precision=HIGHEST, preferred_element_type=F32)
    logits = jnp.where(lane < N_EXPERTS, logits, -jnp.inf)
    m1 = jnp.max(logits, axis=-1, keepdims=True)
    i1 = jnp.min(jnp.where(logits == m1, lane, LANE), axis=-1, keepdims=True)
    rest = jnp.where(lane == i1, -jnp.inf, logits)
    m2 = jnp.max(rest, axis=-1, keepdims=True)
    i2 = jnp.min(jnp.where(rest == m2, lane, LANE), axis=-1, keepdims=True)
    e2 = jnp.exp(m2 - m1)
    ti_ref[...] = jnp.where(lane == 0, i1, jnp.where(lane == 1, i2, 0))
    tw_ref[...] = jnp.where(lane == 0, 1.0 / (1.0 + e2), jnp.where(lane == 1, e2 / (1.0 + e2), 0.0))


def _router(x2, scale, shift, router_p, seq, tm):
    n, d = x2.shape
    per_b = seq // tm
    row = lambda i: (i, 0)
    mod = lambda i: (i // per_b, 0, 0)
    return pl.pallas_call(
        _router_kernel,
        out_shape=(jax.ShapeDtypeStruct((n, d), F32), jax.ShapeDtypeStruct((n, LANE), jnp.int32),
                   jax.ShapeDtypeStruct((n, LANE), F32)),
        grid=(n // tm,),
        in_specs=[pl.BlockSpec((tm, d), row), pl.BlockSpec((1, 1, d), mod), pl.BlockSpec((1, 1, d), mod),
                  pl.BlockSpec(router_p.shape, lambda i: (0, 0))],
        out_specs=(pl.BlockSpec((tm, d), row), pl.BlockSpec((tm, LANE), row), pl.BlockSpec((tm, LANE), row)),
        compiler_params=_cparams("parallel"),
        name="moe_router",
    )(x2, scale, shift, router_p)


def _scatter_rows_kernel(pos_ref, h_ref, init_ref, o_ref, sem):
    del init_ref
    rows = h_ref.shape[0]
    n = pl.num_programs(0) * rows
    base = pl.program_id(0) * rows

    def issue(r, carry):
        row = h_ref.at[pl.ds(r, 1)]
        pltpu.make_async_copy(row, o_ref.at[pl.ds(pos_ref[base + r], 1)], sem).start()
        pltpu.make_async_copy(row, o_ref.at[pl.ds(pos_ref[n + base + r], 1)], sem).start()
        return carry

    lax.fori_loop(0, rows, issue, 0, unroll=8)
    for _ in range(2):
        pltpu.make_async_copy(h_ref, o_ref.at[pl.ds(0, rows)], sem).wait()


def _scatter_rows(pos, h, n_rows):
    n, d = h.shape
    return pl.pallas_call(
        _scatter_rows_kernel,
        out_shape=jax.ShapeDtypeStruct((n_rows, d), h.dtype),
        grid_spec=pltpu.PrefetchScalarGridSpec(
            num_scalar_prefetch=1,
            grid=(n // GATHER_ROWS,),
            in_specs=[pl.BlockSpec((GATHER_ROWS, d), lambda i, pos_ref: (i, 0)),
                      pl.BlockSpec(memory_space=pl.ANY)],
            out_specs=pl.BlockSpec(memory_space=pl.ANY),
            scratch_shapes=[pltpu.SemaphoreType.DMA]),
        input_output_aliases={2: 0},
        compiler_params=_cparams("arbitrary"),
        name="row_scatter",
    )(pos, h, jnp.zeros((n_rows, d), h.dtype))


def _expert_up_kernel(te_ref, nu_ref, x_ref, wg_ref, wu_ref, o_ref, xb_sc):
    i = pl.program_id(0)
    f = pl.program_id(1)

    @pl.when(i < nu_ref[0])
    def _compute():
        @pl.when(f == 0)
        def _cast():
            xb_sc[...] = x_ref[...].astype(BF16)
        h = xb_sc[...]
        gate = _dot(h, wg_ref[0, 0])
        up = _dot(h, wu_ref[0, 0])
        o_ref[...] = (jax.nn.silu(gate) * up).astype(BF16)

    @pl.when(i >= nu_ref[0])
    def _unused_tile():
        o_ref[...] = jnp.zeros_like(o_ref)


def _expert_up(tile_expert, n_used, xg, w1b, layer):
    p_rows, d = xg.shape
    nf = w1b.shape[3] // (2 * MOE_FC)
    return pl.pallas_call(
        _expert_up_kernel,
        out_shape=jax.ShapeDtypeStruct((p_rows, nf * MOE_FC), BF16),
        grid_spec=pltpu.PrefetchScalarGridSpec(
            num_scalar_prefetch=2,
            grid=(p_rows // MOE_TM, nf),
            in_specs=[pl.BlockSpec((MOE_TM, d), lambda i, f, te, nu: (i, 0)),
                      pl.BlockSpec((1, 1, d, MOE_FC), lambda i, f, te, nu: (layer, te[i], 0, f)),
                      pl.BlockSpec((1, 1, d, MOE_FC), lambda i, f, te, nu: (layer, te[i], 0, nf + f))],
            out_specs=pl.BlockSpec((MOE_TM, MOE_FC), lambda i, f, te, nu: (i, f)),
            scratch_shapes=[pltpu.VMEM((MOE_TM, d), BF16)]),
        compiler_params=_cparams("parallel", "arbitrary"),
        name="moe_up",
    )(tile_expert, n_used, xg, w1b, w1b)


def _expert_down_kernel(te_ref, nu_ref, a_ref, w_ref, o_ref):
    i = pl.program_id(0)

    @pl.when(i < nu_ref[0])
    def _compute():
        o_ref[...] = _dot(a_ref[...], w_ref[0, 0])

    @pl.when(i >= nu_ref[0])
    def _unused_tile():
        o_ref[...] = jnp.zeros_like(o_ref)


def _expert_down(tile_expert, n_used, act, w2b, layer):
    p_rows, dffe = act.shape
    d = w2b.shape[3]
    return pl.pallas_call(
        _expert_down_kernel,
        out_shape=jax.ShapeDtypeStruct((p_rows, d), F32),
        grid_spec=pltpu.PrefetchScalarGridSpec(
            num_scalar_prefetch=2,
            grid=(p_rows // MOE_TM,),
            in_specs=[pl.BlockSpec((MOE_TM, dffe), lambda i, te, nu: (i, 0)),
                      pl.BlockSpec((1, 1, dffe, d), lambda i, te, nu: (layer, te[i], 0, 0))],
            out_specs=pl.BlockSpec((MOE_TM, d), lambda i, te, nu: (i, 0))),
        compiler_params=_cparams("parallel"),
        name="moe_down",
    )(tile_expert, n_used, act, w2b)


def _moe_combine_kernel(pos_ref, x_ref, gate_ref, g_ref, b_ref, tw_ref, y_ref, o_ref, ybuf, sems, *, alpha):
    i = pl.program_id(0)
    nt = pl.num_programs(0)
    tm = x_ref.shape[0]
    n = nt * tm

    def fetch(step, slot):
        def issue(r, carry):
            t = step * tm + r
            pltpu.make_async_copy(y_ref.at[pl.ds(pos_ref[t], 1)], ybuf.at[slot, 0, pl.ds(r, 1)],
                                  sems.at[slot]).start()
            pltpu.make_async_copy(y_ref.at[pl.ds(pos_ref[n + t], 1)], ybuf.at[slot, 1, pl.ds(r, 1)],
                                  sems.at[slot]).start()
            return carry
        lax.fori_loop(0, tm, issue, 0, unroll=8)

    @pl.when(i == 0)
    def _first():
        fetch(0, 0)

    @pl.when(i + 1 < nt)
    def _next():
        fetch(i + 1, (i + 1) % 2)

    slot = i % 2
    for choice in range(2):
        pltpu.make_async_copy(y_ref.at[pl.ds(0, tm)], ybuf.at[slot, choice], sems.at[slot]).wait()
    tw = tw_ref[...]
    y = tw[:, 0:1] * ybuf[slot, 0] + tw[:, 1:2] * ybuf[slot, 1]
    o_ref[...] = _res_ln(x_ref[...], gate_ref[0], y, g_ref[...], b_ref[...], alpha)


def _moe_combine(pos, x2, gate, lng, lnb, tw, y, seq, tm, alpha):
    n, d = x2.shape
    per_b = seq // tm
    row = lambda i, p: (i, 0)
    const = lambda i, p: (0, 0)
    return pl.pallas_call(
        functools.partial(_moe_combine_kernel, alpha=alpha),
        out_shape=jax.ShapeDtypeStruct((n, d), F32),
        grid_spec=pltpu.PrefetchScalarGridSpec(
            num_scalar_prefetch=1,
            grid=(n // tm,),
            in_specs=[pl.BlockSpec((tm, d), row), pl.BlockSpec((1, 1, d), lambda i, p: (i // per_b, 0, 0)),
                      pl.BlockSpec((1, d), const), pl.BlockSpec((1, d), const), pl.BlockSpec((tm, LANE), row),
                      pl.BlockSpec(memory_space=pl.ANY)],
            out_specs=pl.BlockSpec((tm, d), row),
            scratch_shapes=[pltpu.VMEM((2, 2, tm, d), F32), pltpu.SemaphoreType.DMA((2,))]),
        compiler_params=_cparams("arbitrary"),
        name="moe_combine_ln",
    )(pos, x2, gate, lng, lnb, tw, y)


def _route_tables(ti, n_rows):
    e_flat = jnp.concatenate([ti[:, 0], ti[:, 1]])
    onehot = (e_flat[:, None] == jnp.arange(N_EXPERTS, dtype=jnp.int32)[None, :]).astype(jnp.int32)
    cum = jnp.cumsum(onehot, axis=0)
    counts = cum[-1]
    padded = (counts + MOE_TM - 1) // MOE_TM * MOE_TM
    ends = jnp.cumsum(padded)
    pos = jnp.sum(onehot * (cum - 1 + (ends - padded)[None, :]), axis=1)
    tile_start = jnp.arange(n_rows // MOE_TM, dtype=jnp.int32) * MOE_TM
    tile_expert = jnp.minimum(jnp.sum((tile_start[:, None] >= ends[None, :]).astype(jnp.int32), axis=1),
                              N_EXPERTS - 1)
    return pos.astype(jnp.int32), tile_expert, (ends[-1:] // MOE_TM).astype(jnp.int32)


def _moe(x2, scale, shift, gate, lng, lnb, router_p, w1b, w2b, layer, seq, tm, alpha):
    n, d = x2.shape
    n_rows = 2 * n + N_EXPERTS * MOE_TM
    assert n % GATHER_ROWS == 0
    h, ti, tw = _router(x2, scale, shift, router_p, seq, tm)
    pos, tile_expert, n_used = _route_tables(ti, n_rows)
    xg = _scatter_rows(pos, h, n_rows)
    act = _expert_up(tile_expert, n_used, xg, w1b, layer)
    y = _expert_down(tile_expert, n_used, act, w2b, layer)
    return _moe_combine(pos, x2, gate, lng, lnb, tw, y, seq, tm, alpha)


def _rel_bucket_np(dist):
    n = np.maximum(dist, 0)
    max_exact = NUM_BUCKETS // 2
    nf = np.maximum(n, 1).astype(np.float32)
    large = max_exact + (np.log(nf / np.float32(max_exact)) / np.float32(math.log(MAX_DISTANCE / max_exact))
                         * np.float32(NUM_BUCKETS - max_exact)).astype(np.int32)
    large = np.minimum(large, NUM_BUCKETS - 1)
    return np.where(n < max_exact, n, large).astype(np.int32)


def _bias_lookup(rel_bias, dist, valid):
    onehot = _rel_bucket_np(dist)[..., None] == np.arange(NUM_BUCKETS)
    vals = jnp.einsum('...k,kh->...h', jnp.asarray(onehot).astype(F32), rel_bias, precision=HIGHEST)
    return jnp.where(jnp.asarray(valid)[..., None], vals * LOG2E, NEG_INF)


def _key_major(vals, add_masked_tile=False):
    n_tiles, tq, tk, _ = vals.shape
    if add_masked_tile:
        vals = jnp.concatenate([vals, jnp.full((1,) + vals.shape[1:], NEG_INF, vals.dtype)], axis=0)
        n_tiles += 1
    vals = vals.reshape(n_tiles, tq, tk, N_KV, Q_PER_KV).transpose(3, 0, 2, 4, 1)
    return vals.reshape(N_KV, n_tiles, tk, Q_PER_KV * tq).astype(F32)


def _band_dist(n_delta):
    i = np.arange(ATT_TQ)[None, :, None]
    k = np.arange(ATT_TQ)[None, None, :]
    return ATT_TQ * np.arange(n_delta)[:, None, None] + i - k


def _cmp_bias_table(rel_bias, n_qt, n_cmp_pad):
    per_tile = ATT_TQ // CMP_STRIDE
    shift = per_tile * (n_qt - 1)
    rel_n = np.arange(n_cmp_pad + shift) - shift
    dist = np.arange(ATT_TQ)[:, None] - (CMP_STRIDE * rel_n[None, :] + L_CMP - 1)
    base = _bias_lookup(rel_bias, dist, dist >= 0)
    tiles = [base[:, shift - per_tile * qt: shift - per_tile * qt + n_cmp_pad] for qt in range(n_qt)]
    return _key_major(jnp.stack(tiles))


def _inproj_weight(w_in):
    depth, d, _ = w_in.shape
    d_nsa = N_HEADS * HEAD_DIM
    d_kv = N_KV * HEAD_DIM
    base_ks = d_nsa + 2 * d_kv
    base_gl = d_nsa + 6 * d_kv
    base_cv = base_gl + N_HEADS * N_BRANCH
    wq = jnp.pad(w_in[:, :, :d_nsa].reshape(depth, d, N_HEADS, HEAD_DIM),
                 ((0, 0), (0, 0), (0, 0), (0, LANE - HEAD_DIM))).reshape(depth, d, Q_COLS)
    parts = [wq, w_in[:, :, d_nsa:base_ks]]
    for pair in range(2):
        kbase = base_ks + pair * 2 * d_kv
        for g in range(N_KV):
            parts.append(w_in[:, :, kbase + g * HEAD_DIM: kbase + (g + 1) * HEAD_DIM])
            parts.append(w_in[:, :, kbase + d_kv + g * HEAD_DIM: kbase + d_kv + (g + 1) * HEAD_DIM])
    parts.append(w_in[:, :, base_cv:])
    gl = w_in[:, :, base_gl:base_cv].reshape(depth, d, N_KV, Q_PER_KV, N_BRANCH).transpose(0, 2, 4, 3, 1)
    gl = jnp.pad(gl.reshape(depth, N_KV, N_BRANCH * Q_PER_KV, d),
                 ((0, 0), (0, 0), (0, LANE - N_BRANCH * Q_PER_KV), (0, 0)))
    return jnp.concatenate(parts, axis=-1).astype(BF16), gl.reshape(depth, GL_ROWS, d).astype(BF16)


def kernel(x, c, w_in, cmp_pos, cmp_w1, cmp_w2, conv_w, conv_b, conv_ln_g, conv_ln_b, w_out, rel_bias,
           ada_w, ada_b, ln_g, ln_b, ffn_w1, ffn_w2, router_w, moe_w1, moe_w2):
    bsz, seq, d = x.shape
    depth = w_in.shape[0]
    n = bsz * seq
    alpha = (2 * depth) ** 0.25
    n_blocks = seq // L_SEL
    n_sel = min(N_SEL, n_blocks)
    n_cmp = (seq - L_CMP) // CMP_STRIDE + 1
    nrow = seq // CMP_STRIDE
    assert d == 1024 and seq % 512 == 0 and nrow == LANE and n_cmp == nrow - 1
    tm = 512

    n_qt = seq // ATT_TQ
    d_win = _band_dist(WINDOW // ATT_TQ + 1)
    win_tab = _key_major(_bias_lookup(rel_bias, d_win, (d_win >= 0) & (d_win < WINDOW)), add_masked_tile=True)
    d_sel = _band_dist(3)
    assert _rel_bucket_np(d_sel[2]).min() == NUM_BUCKETS - 1
    sel_tab = _key_major(_bias_lookup(rel_bias, d_sel, d_sel >= 0), add_masked_tile=True)
    cmp_tab = _cmp_bias_table(rel_bias, n_qt, nrow)

    w_in_p, w_gl_t = _inproj_weight(w_in)
    eye2 = jnp.eye(2, dtype=F32)
    w1r = cmp_w1.reshape(depth, 2, L_CMP, HEAD_DIM, HEAD_DIM)
    w1x = jnp.zeros((depth, L_CMP, 4 * HEAD_DIM, 4 * HEAD_DIM), F32)
    for s in range(2):
        for g in range(N_KV):
            r0 = (s * N_KV + g) * HEAD_DIM
            c0 = (g * 2 + s) * HEAD_DIM
            w1x = w1x.at[:, :, r0:r0 + HEAD_DIM, c0:c0 + HEAD_DIM].set(w1r[:, s])
    half = L_CMP // 2
    w_lo = w1x[:, :half].astype(BF16)
    w_hi = w1x[:, half:].astype(BF16)
    w2bd = jnp.einsum('zsde,st,gh->zgsdhte', cmp_w2, eye2, eye2).reshape(depth, 4 * HEAD_DIM, 4 * HEAD_DIM).astype(BF16)
    posx = jnp.broadcast_to(cmp_pos[:, :, :, None, :], (depth, 2, L_CMP, N_KV, HEAD_DIM)).transpose(0, 2, 1, 3, 4)
    pos_lo = posx[:, :half].reshape(depth, half, 4 * HEAD_DIM)
    pos_hi = posx[:, half:].reshape(depth, half, 4 * HEAD_DIM)
    conv_wp = jnp.broadcast_to(conv_w[:, :, None, :], conv_w.shape[:2] + (SUBLANE, conv_w.shape[2]))
    w_out_b = w_out.astype(BF16)
    dff = ffn_w2.shape[1]
    ffn_wg = ffn_w1[:, :, :dff].astype(BF16)
    ffn_wu = ffn_w1[:, :, dff:].astype(BF16)
    ffn_w2b = ffn_w2.astype(BF16)
    router_p = jnp.pad(router_w, ((0, 0), (0, 0), (0, LANE - N_EXPERTS)))
    moe_w1b = moe_w1.astype(BF16)
    moe_w2b = moe_w2.astype(BF16)

    mod = _ada(c, ada_w, ada_b).reshape(depth, 6, bsz, 1, d)
    x2 = x.reshape(n, d)
    for l in range(depth):
        shift1, scale1, gate1, shift2, scale2, gate2 = [mod[l, k] for k in range(6)]
        q, kvc, kvsw, gl3, cv = _inproj(x2, scale1, shift1, w_in_p[l], w_gl_t[l], seq, tm)
        q3 = q.reshape(bsz, seq, Q_COLS)
        kvsw3 = kvsw.reshape(bsz, seq, KVSW_COLS)
        kvsw_t = kvsw3.reshape(bsz, n_qt, ATT_TQ, 2 * N_KV, LANE).transpose(0, 3, 1, 4, 2)
        kvcmp, kvcmp_t = _compress(kvc.reshape(bsz, seq, KVC_COLS), pos_lo[l], pos_hi[l], w_lo[l], w_hi[l],
                                   w2bd[l])
        o_nsa = _nsa_attn(q3, kvcmp, kvcmp_t, cmp_tab, kvsw3, kvsw_t, sel_tab, win_tab, gl3, n_blocks, n_sel)
        o_conv = _conv_branch(cv.reshape(bsz, seq, 1024), conv_wp[l], conv_b[l][None], conv_ln_g[l][None],
                              conv_ln_b[l][None])
        x2 = _outproj(x2, gate1, ln_g[l, 0][None], ln_b[l, 0][None], o_nsa.reshape(n, 512), o_conv.reshape(n, 512),
                      w_out_b[l], seq, tm, alpha)
        if l % 2 == 0:
            act = _ffn1(x2, scale2, shift2, ffn_wg[l // 2], ffn_wu[l // 2], seq, tm)
            x2 = _ffn2(x2, gate2, ln_g[l, 1][None], ln_b[l, 1][None], act, ffn_w2b[l // 2], seq, tm, alpha)
        else:
            x2 = _moe(x2, scale2, shift2, gate2, ln_g[l, 1][None], ln_b[l, 1][None], router_p[l // 2],
                      moe_w1b, moe_w2b, l // 2, seq, tm, alpha)
    return x2.reshape(bsz, seq, d)
```

```python
import functools
import math

import numpy as np
import jax
import jax.numpy as jnp
from jax import lax
from jax.experimental import pallas as pl
from jax.experimental.pallas import tpu as pltpu

F32 = jnp.float32
BF16 = jnp.bfloat16
HIGHEST = lax.Precision.HIGHEST

HEAD_DIM = 64
N_HEADS = 8
N_KV = 2
Q_PER_KV = N_HEADS // N_KV
N_BRANCH = 3
L_CMP = 32
CMP_STRIDE = 16
L_SEL = 64
N_SEL = 8
WINDOW = 512
CONV_WIDTH = 31
NUM_BUCKETS = 32
MAX_DISTANCE = 128
N_EXPERTS = 8
LN_EPS = 1e-5
NEG_INF = -1e30
LOG2E = math.log2(math.e)

LANE = 128
SUBLANE = 8
ATT_TQ = 128
ATT_ROWS = Q_PER_KV * ATT_TQ
CONV_TQ = 256
CONV_HALO = 32
CONV_RC = 64
VMEM_LIMIT_BYTES = 52 * 1024 * 1024


def _cparams(*sem):
    return pltpu.CompilerParams(dimension_semantics=sem, vmem_limit_bytes=VMEM_LIMIT_BYTES)


def _dot(a, b):
    return jnp.dot(a, b, preferred_element_type=F32)


def _dot_nt(a, b, precision=None):
    return lax.dot_general(a, b, (((1,), (1,)), ((), ())), precision=precision,
                           preferred_element_type=F32)


def _res_ln(x, gate, y, g, b, alpha):
    z = alpha * x + (1.0 + gate) * y
    mu = jnp.mean(z, axis=-1, keepdims=True)
    zc = z - mu
    var = jnp.mean(zc * zc, axis=-1, keepdims=True)
    return zc * lax.rsqrt(var + LN_EPS) * g + b


def _ada_kernel(c_ref, w_ref, b_ref, o_ref):
    sc = jax.nn.silu(c_ref[...])
    o_ref[0, 0] = jnp.dot(sc, w_ref[0], precision=HIGHEST, preferred_element_type=F32) + b_ref[0]


def _ada(c, ada_w, ada_b):
    depth, d, _ = ada_w.shape
    bsz = c.shape[0]
    return pl.pallas_call(
        _ada_kernel,
        out_shape=jax.ShapeDtypeStruct((depth, 6, bsz, d), F32),
        grid=(depth, 6),
        in_specs=[pl.BlockSpec((bsz, d), lambda l, k: (0, 0)),
                  pl.BlockSpec((1, d, d), lambda l, k: (l, 0, k)),
                  pl.BlockSpec((1, 1, d), lambda l, k: (l, 0, k))],
        out_specs=pl.BlockSpec((1, 1, bsz, d), lambda l, k: (l, k, 0, 0)),
        compiler_params=_cparams("parallel", "parallel"),
        name="ada_mod",
    )(c, ada_w, ada_b.reshape(depth, 1, 6 * d))


Q_COLS = N_HEADS * LANE
KVC_COLS = 4 * HEAD_DIM
KVSW_COLS = 8 * HEAD_DIM
GL_ROWS = N_KV * LANE


def _inproj_kernel(x_ref, sc_ref, sh_ref, w_ref, wgl_ref, q_ref, kvc_ref, kvsw_ref, glt_ref, cv_ref):
    h = (x_ref[...] * (1.0 + sc_ref[0]) + sh_ref[0]).astype(BF16)
    c0, c1, c2 = Q_COLS, Q_COLS + KVC_COLS, Q_COLS + KVC_COLS + KVSW_COLS
    q_ref[...] = (_dot(h, w_ref[:, 0:c0]) * (HEAD_DIM ** -0.5 * LOG2E)).astype(BF16)
    kvc_ref[...] = _dot(h, w_ref[:, c0:c1])
    kvsw_ref[...] = _dot(h, w_ref[:, c1:c2]).astype(BF16)
    cv_ref[...] = _dot(h, w_ref[:, c2:])
    glt_ref[...] = _dot_nt(wgl_ref[...], h)


def _inproj(x2, scale, shift, w, wgl_t, seq, tm):
    n, d = x2.shape
    per_b = seq // tm
    d_cv = w.shape[1] - (Q_COLS + KVC_COLS + KVSW_COLS)
    row = lambda i: (i, 0)
    mod = lambda i: (i // per_b, 0, 0)
    const = lambda i: (0, 0)
    return pl.pallas_call(
        _inproj_kernel,
        out_shape=(jax.ShapeDtypeStruct((n, Q_COLS), BF16), jax.ShapeDtypeStruct((n, KVC_COLS), F32),
                   jax.ShapeDtypeStruct((n, KVSW_COLS), BF16), jax.ShapeDtypeStruct((GL_ROWS, n), F32),
                   jax.ShapeDtypeStruct((n, d_cv), F32)),
        grid=(n // tm,),
        in_specs=[pl.BlockSpec((tm, d), row), pl.BlockSpec((1, 1, d), mod), pl.BlockSpec((1, 1, d), mod),
                  pl.BlockSpec(w.shape, const), pl.BlockSpec(wgl_t.shape, const)],
        out_specs=(pl.BlockSpec((tm, Q_COLS), row), pl.BlockSpec((tm, KVC_COLS), row),
                   pl.BlockSpec((tm, KVSW_COLS), row), pl.BlockSpec((GL_ROWS, tm), lambda i: (0, i)),
                   pl.BlockSpec((tm, d_cv), row)),
        compiler_params=_cparams("parallel"),
        name="in_proj",
    )(x2, scale, shift, w, wgl_t)


def _compress_kernel(ak_ref, av_ref, plo_ref, phi_ref, wlo_ref, whi_ref, w2_ref, o_ref, ot_ref):
    nrow = o_ref.shape[1]
    lo = jnp.zeros((nrow, o_ref.shape[2]), F32)
    hi = jnp.zeros((nrow, o_ref.shape[2]), F32)
    for l in range(CMP_STRIDE):
        for part, a_ref in enumerate((ak_ref, av_ref)):
            lanes = slice(part * LANE, (part + 1) * LANE)
            tok = a_ref[0, pl.ds(l, nrow, stride=CMP_STRIDE), :]
            lo = lo + _dot((tok + plo_ref[l:l + 1, lanes]).astype(BF16), wlo_ref[l, lanes, :])
            hi = hi + _dot((tok + phi_ref[l:l + 1, lanes]).astype(BF16), whi_ref[l, lanes, :])
    pre = lo + pltpu.roll(hi, nrow - 1, 0)
    act = jax.nn.gelu(pre, approximate=True)
    out = _dot(act.astype(BF16), w2_ref[...])
    o_ref[0] = out.astype(BF16)
    ot_ref[0] = out.T.astype(BF16)


def _compress(kvc3, pos_lo, pos_hi, wlo, whi, w2bd):
    bsz, seq, width = kvc3.shape
    nrow = seq // CMP_STRIDE
    const = lambda b: (0, 0)
    const3 = lambda b: (0, 0, 0)
    return pl.pallas_call(
        _compress_kernel,
        out_shape=(jax.ShapeDtypeStruct((bsz, nrow, 256), BF16), jax.ShapeDtypeStruct((bsz, 256, nrow), BF16)),
        grid=(bsz,),
        in_specs=[pl.BlockSpec((1, seq, LANE), lambda b: (b, 0, 0)), pl.BlockSpec((1, seq, LANE), lambda b: (b, 0, 1)),
                  pl.BlockSpec(pos_lo.shape, const), pl.BlockSpec(pos_hi.shape, const),
                  pl.BlockSpec(wlo.shape, const3), pl.BlockSpec(whi.shape, const3),
                  pl.BlockSpec((256, 256), const)],
        out_specs=(pl.BlockSpec((1, nrow, 256), lambda b: (b, 0, 0)),
                   pl.BlockSpec((1, 256, nrow), lambda b: (b, 0, 0))),
        compiler_params=_cparams("parallel"),
        name="compress_mlp",
    )(kvc3, kvc3, pos_lo, pos_hi, wlo, whi, w2bd)


def _stack_heads(q):
    return jnp.concatenate([q[:, r * LANE:(r + 1) * LANE] for r in range(Q_PER_KV)], axis=0)


def _branch_gates(gl_t, branch):
    return jnp.concatenate([jax.nn.sigmoid(gl_t[branch * Q_PER_KV + r:branch * Q_PER_KV + r + 1, :])
                            for r in range(Q_PER_KV)], axis=1)


def _heads_to_rows(o_t):
    tq = o_t.shape[1] // Q_PER_KV
    og = o_t.astype(BF16)
    ii = lax.broadcasted_iota(jnp.int32, (tq, tq), 0)
    kk = lax.broadcasted_iota(jnp.int32, (tq, tq), 1)
    eye = jnp.where(ii == kk, 1.0, 0.0).astype(BF16)
    halves = []
    for pair in range(Q_PER_KV // 2):
        two_heads = jnp.concatenate([og[:, (2 * pair) * tq:(2 * pair + 1) * tq],
                                     og[:, (2 * pair + 1) * tq:(2 * pair + 2) * tq]], axis=0)
        halves.append(_dot_nt(eye, two_heads))
    return jnp.concatenate(halves, axis=1).astype(BF16)


def _cmp_section(qs, qt, kv_ref, kvt_ref, tab_ref, gl_ref, sel_sc, *, n_blocks, n_sel):
    tq = ATT_TQ
    outs = []
    jj = lax.broadcasted_iota(jnp.int32, (n_blocks, LANE), 0)
    nn = lax.broadcasted_iota(jnp.int32, (n_blocks, LANE), 1)
    overlap_t = jnp.where((CMP_STRIDE * nn < L_SEL * jj + L_SEL) & (CMP_STRIDE * nn + L_CMP > L_SEL * jj),
                          1.0, 0.0).astype(F32)
    jb = lax.broadcasted_iota(jnp.int32, (n_blocks, tq), 0)
    tb = (qt * tq + lax.broadcasted_iota(jnp.int32, (n_blocks, tq), 1)) // L_SEL
    for g in range(N_KV):
        bias = tab_ref[g, 0]
        s = _dot_nt(kv_ref[0, :, g * LANE:(g + 1) * LANE], qs[g]) + bias
        valid = bias > 0.5 * NEG_INF
        m = jnp.max(s, axis=0, keepdims=True)
        p = jnp.where(valid, jnp.exp2(s - m), 0.0)
        l = jnp.sum(p, axis=0, keepdims=True)
        p = p * (1.0 / jnp.where(l > 0.0, l, 1.0))
        o_t = _dot(kvt_ref[0, g * LANE + HEAD_DIM:(g + 1) * LANE, :], p.astype(BF16))
        outs.append(o_t * _branch_gates(gl_ref[g * LANE:(g + 1) * LANE, :], 0))

        psum = p[:, 0:tq] + p[:, tq:2 * tq] + p[:, 2 * tq:3 * tq] + p[:, 3 * tq:4 * tq]
        imp_t = jnp.dot(overlap_t, psum, precision=HIGHEST, preferred_element_type=F32)
        score = jnp.where(jb > tb, -jnp.inf,
                          jnp.where((jb == 0) | (jb == tb) | (jb == tb - 1), jnp.inf, imp_t))
        rank = jnp.zeros((n_blocks, tq), F32)
        for jp in range(n_blocks):
            row = score[jp:jp + 1, :]
            earlier = jnp.where(jb > jp, 1.0, 0.0)
            rank = rank + jnp.where(row > score, 1.0, 0.0) + jnp.where(row == score, earlier, 0.0)
        madd = jnp.where(rank < float(n_sel), 0.0, NEG_INF)
        per_chunk = ATT_TQ // L_SEL
        for c in range(n_blocks // per_chunk):
            sel_sc[g, c, 0:per_chunk, :] = madd[c * per_chunk:(c + 1) * per_chunk, :]
    return outs


def _band_section(qs, qt, kv_ref, kvt_ref, tab_ref, gl_ref, sel_sc, s_sc, *, span, group, branch):
    tq = ATT_TQ
    n_delta = tab_ref.shape[1] - 1
    n_chunks = kv_ref.shape[1] // tq
    c_base = qt - span if span is not None else 0
    use_sel = sel_sc is not None

    def chunk_ids(gi, u):
        c = c_base + gi * group + u
        inside = (c >= 0) & (c <= qt)
        dd = jnp.where(inside, jnp.minimum(qt - c, n_delta - 1), n_delta)
        return jnp.clip(c, 0, n_chunks - 1), dd

    def rows(gi, u):
        return pl.ds(pl.multiple_of((gi * group + u) * tq, tq), tq)

    def scores(gi, ms):
        ms = list(ms)
        for u in range(group):
            cc, dd = chunk_ids(gi, u)
            kv = kv_ref[0, pl.ds(pl.multiple_of(cc * tq, tq), tq), :]
            for g in range(N_KV):
                s = _dot_nt(kv[:, g * LANE:(g + 1) * LANE], qs[g]) + tab_ref[g, dd]
                if use_sel:
                    blocks = sel_sc[g, cc, 0:tq // L_SEL, :]
                    madd = jnp.concatenate([jnp.broadcast_to(blocks[j:j + 1, :], (L_SEL, tq))
                                            for j in range(tq // L_SEL)], axis=0)
                    s = s + jnp.concatenate([madd] * Q_PER_KV, axis=1)
                s_sc[g, rows(gi, u), :] = s
                ms[g] = jnp.maximum(ms[g], jnp.max(s, axis=0, keepdims=True))
        return tuple(ms)

    m0 = tuple(jnp.full((1, ATT_ROWS), NEG_INF, F32) for _ in range(N_KV))
    if span is not None:
        n_groups = 1
        m = scores(0, m0)
    else:
        n_groups = qt // group + 1
        m = lax.fori_loop(0, n_groups, scores, m0)

    def values(gi, carry):
        carry = [list(c) for c in carry]
        for u in range(group):
            cc, _ = chunk_ids(gi, u)
            for g in range(N_KV):
                l, acc = carry[g]
                p = jnp.exp2(s_sc[g, rows(gi, u), :] - m[g])
                l = l + jnp.sum(p, axis=0, keepdims=True)
                acc = acc + _dot(kvt_ref[0, g, cc, HEAD_DIM:, :], p.astype(BF16))
                carry[g] = [l, acc]
        return tuple(tuple(c) for c in carry)

    init = tuple((jnp.zeros((1, ATT_ROWS), F32), jnp.zeros((HEAD_DIM, ATT_ROWS), F32)) for _ in range(N_KV))
    if span is not None:
        out = values(0, init)
    else:
        out = lax.fori_loop(0, n_groups, values, init)
    return [out[g][1] * (_branch_gates(gl_ref[g * LANE:(g + 1) * LANE, :], branch) / out[g][0]) for g in range(N_KV)]


SEL_GROUP = 4
WIN_SPAN = WINDOW // ATT_TQ


def _nsa_kernel(q_ref, kvc_ref, kvct_ref, ctab_ref, kvs_ref, kvst_ref, stab_ref, kvw_ref, kvwt_ref, wtab_ref,
                gl_ref, o_ref, sel_sc, ssel_sc, swin_sc, *, n_blocks, n_sel):
    qt = pl.program_id(1)
    hw = Q_PER_KV * LANE
    qs = [_stack_heads(q_ref[0, :, g * hw:(g + 1) * hw]) for g in range(N_KV)]
    o_cmp = _cmp_section(qs, qt, kvc_ref, kvct_ref, ctab_ref, gl_ref, sel_sc, n_blocks=n_blocks, n_sel=n_sel)
    o_sel = _band_section(qs, qt, kvs_ref, kvst_ref, stab_ref, gl_ref, sel_sc, ssel_sc,
                          span=None, group=SEL_GROUP, branch=1)
    o_win = _band_section(qs, qt, kvw_ref, kvwt_ref, wtab_ref, gl_ref, None, swin_sc,
                          span=WIN_SPAN, group=WIN_SPAN + 1, branch=2)
    for g in range(N_KV):
        o_ref[0, :, g * 256:(g + 1) * 256] = _heads_to_rows(o_cmp[g] + o_sel[g] + o_win[g])


def _nsa_attn(q3, kvcmp, kvcmp_t, cmp_tab, kvsw3, kvsw_t, sel_tab, win_tab, gl_t, n_blocks, n_sel):
    bsz, seq, _ = q3.shape
    nqt = seq // ATT_TQ
    n_cmp = kvcmp.shape[1]
    sel_slots = -(-nqt // SEL_GROUP) * SEL_GROUP
    full4 = lambda b, t: (0, 0, 0, 0)
    kv_spec = lambda pair: pl.BlockSpec((1, seq, N_KV * LANE), lambda b, t: (b, 0, pair))
    kvt_spec = lambda pair: pl.BlockSpec((1, N_KV) + kvsw_t.shape[2:], lambda b, t: (b, pair, 0, 0, 0))
    return pl.pallas_call(
        functools.partial(_nsa_kernel, n_blocks=n_blocks, n_sel=n_sel),
        out_shape=jax.ShapeDtypeStruct((bsz, seq, 512), BF16),
        grid=(bsz, nqt),
        in_specs=[pl.BlockSpec((1, ATT_TQ, Q_COLS), lambda b, t: (b, t, 0)),
                  pl.BlockSpec((1, n_cmp, N_KV * LANE), lambda b, t: (b, 0, 0)),
                  pl.BlockSpec((1, N_KV * LANE, n_cmp), lambda b, t: (b, 0, 0)),
                  pl.BlockSpec((N_KV, 1, n_cmp, ATT_ROWS), lambda b, t: (0, t, 0, 0)),
                  kv_spec(0), kvt_spec(0), pl.BlockSpec(sel_tab.shape, full4),
                  kv_spec(1), kvt_spec(1), pl.BlockSpec(win_tab.shape, full4),
                  pl.BlockSpec((GL_ROWS, ATT_TQ), lambda b, t: (0, b * nqt + t))],
        out_specs=pl.BlockSpec((1, ATT_TQ, 512), lambda b, t: (b, t, 0)),
        scratch_shapes=[pltpu.VMEM((N_KV, n_blocks * L_SEL // ATT_TQ, SUBLANE, ATT_TQ), F32),
                        pltpu.VMEM((N_KV, sel_slots * ATT_TQ, ATT_ROWS), F32),
                        pltpu.VMEM((N_KV, (WIN_SPAN + 1) * ATT_TQ, ATT_ROWS), F32)],
        compiler_params=_cparams("parallel", "parallel"),
        name="nsa_attn",
    )(q3, kvcmp, kvcmp_t, cmp_tab, kvsw3, kvsw_t, sel_tab, kvsw3, kvsw_t, win_tab, gl_t)


def _conv_kernel(cur_ref, prev_ref, w_ref, cb_ref, g_ref, b_ref, o_ref, ybuf, ysh, cbuf):
    i = pl.program_id(1)
    dc = o_ref.shape[-1]
    cur = cur_ref[0]
    prev = prev_ref[0]
    ybuf[CONV_HALO:, :] = cur[:, :dc] * jax.nn.sigmoid(cur[:, dc:])
    halo = prev[:, :dc] * jax.nn.sigmoid(prev[:, dc:])
    ybuf[0:CONV_HALO, :] = jnp.where(i > 0, halo, 0.0)
    n_sh = ysh.shape[1]
    for b in range(1, SUBLANE):
        ysh[b - 1] = ybuf[b:b + n_sh, :]
    off = CONV_HALO - (CONV_WIDTH - 1)
    sub = CONV_RC // SUBLANE
    n_rc = CONV_TQ // CONV_RC
    for lc in range(dc // LANE):
        cols = slice(lc * LANE, (lc + 1) * LANE)
        accs = [jnp.zeros((sub, SUBLANE, LANE), F32) for _ in range(n_rc)]
        for k in range(CONV_WIDTH):
            b = (off + k) % SUBLANE
            wk = w_ref[k, :, cols][None]
            for rc in range(n_rc):
                r0 = rc * CONV_RC + (off + k) - b
                src = ybuf[r0:r0 + CONV_RC, cols] if b == 0 else ysh[b - 1, r0:r0 + CONV_RC, cols]
                accs[rc] = accs[rc] + wk * src.reshape(sub, SUBLANE, LANE)
        for rc in range(n_rc):
            cbuf[rc * CONV_RC:(rc + 1) * CONV_RC, cols] = accs[rc].reshape(CONV_RC, LANE)
    z = cbuf[...] + cb_ref[...]
    mu = jnp.mean(z, axis=-1, keepdims=True)
    zc = z - mu
    var = jnp.mean(zc * zc, axis=-1, keepdims=True)
    y = zc * lax.rsqrt(var + LN_EPS) * g_ref[...] + b_ref[...]
    o_ref[0] = jax.nn.silu(y).astype(BF16)


def _conv_branch(cv3, w, cb, g, b):
    bsz, seq, two_dc = cv3.shape
    dc = two_dc // 2
    per = CONV_TQ // CONV_HALO
    const = lambda bb, i: (0, 0)
    return pl.pallas_call(
        _conv_kernel,
        out_shape=jax.ShapeDtypeStruct((bsz, seq, dc), BF16),
        grid=(bsz, seq // CONV_TQ),
        in_specs=[pl.BlockSpec((1, CONV_TQ, two_dc), lambda bb, i: (bb, i, 0)),
                  pl.BlockSpec((1, CONV_HALO, two_dc), lambda bb, i: (bb, jnp.maximum(i * per - 1, 0), 0)),
                  pl.BlockSpec(w.shape, lambda bb, i: (0, 0, 0)), pl.BlockSpec((1, dc), const),
                  pl.BlockSpec((1, dc), const), pl.BlockSpec((1, dc), const)],
        out_specs=pl.BlockSpec((1, CONV_TQ, dc), lambda bb, i: (bb, i, 0)),
        scratch_shapes=[pltpu.VMEM((CONV_HALO + CONV_TQ, dc), F32),
                        pltpu.VMEM((SUBLANE - 1, CONV_HALO + CONV_TQ - SUBLANE, dc), F32),
                        pltpu.VMEM((CONV_TQ, dc), F32)],
        compiler_params=_cparams("parallel", "parallel"),
        name="conv_branch",
    )(cv3, cv3, w, cb, g, b)


def _outproj_kernel(x_ref, gate_ref, g_ref, b_ref, nsa_ref, ocv_ref, w_ref, o_ref, *, alpha):
    half = nsa_ref.shape[1]
    y = _dot(nsa_ref[...], w_ref[0:half, :]) + _dot(ocv_ref[...], w_ref[half:, :])
    o_ref[...] = _res_ln(x_ref[...], gate_ref[0], y, g_ref[...], b_ref[...], alpha)


def _outproj(x2, gate, lng, lnb, o_nsa, o_conv, w, seq, tm, alpha):
    n, d = x2.shape
    per_b = seq // tm
    row = lambda i: (i, 0)
    const = lambda i: (0, 0)
    return pl.pallas_call(
        functools.partial(_outproj_kernel, alpha=alpha),
        out_shape=jax.ShapeDtypeStruct((n, d), F32),
        grid=(n // tm,),
        in_specs=[pl.BlockSpec((tm, d), row), pl.BlockSpec((1, 1, d), lambda i: (i // per_b, 0, 0)),
                  pl.BlockSpec((1, d), const), pl.BlockSpec((1, d), const),
                  pl.BlockSpec((tm, o_nsa.shape[1]), row), pl.BlockSpec((tm, o_conv.shape[1]), row),
                  pl.BlockSpec(w.shape, const)],
        out_specs=pl.BlockSpec((tm, d), row),
        compiler_params=_cparams("parallel"),
        name="out_proj_ln",
    )(x2, gate, lng, lnb, o_nsa, o_conv, w)


def _ffn1_kernel(x_ref, sc_ref, sh_ref, wg_ref, wu_ref, o_ref, *, tn):
    h = (x_ref[...] * (1.0 + sc_ref[0]) + sh_ref[0]).astype(BF16)
    for j in range(o_ref.shape[1] // tn):
        cols = slice(j * tn, (j + 1) * tn)
        gate = _dot(h, wg_ref[:, cols])
        up = _dot(h, wu_ref[:, cols])
        o_ref[:, cols] = (jax.nn.silu(gate) * up).astype(BF16)


def _ffn1(x2, scale, shift, wg, wu, seq, tm):
    n, d = x2.shape
    dff = wg.shape[1]
    per_b = seq // tm
    row = lambda i: (i, 0)
    mod = lambda i: (i // per_b, 0, 0)
    const = lambda i: (0, 0)
    return pl.pallas_call(
        functools.partial(_ffn1_kernel, tn=256),
        out_shape=jax.ShapeDtypeStruct((n, dff), BF16),
        grid=(n // tm,),
        in_specs=[pl.BlockSpec((tm, d), row), pl.BlockSpec((1, 1, d), mod), pl.BlockSpec((1, 1, d), mod),
                  pl.BlockSpec((d, dff), const), pl.BlockSpec((d, dff), const)],
        out_specs=pl.BlockSpec((tm, dff), row),
        compiler_params=_cparams("parallel"),
        name="ffn_up",
    )(x2, scale, shift, wg, wu)


def _ffn2_kernel(x_ref, gate_ref, g_ref, b_ref, a_ref, w_ref, o_ref, *, alpha):
    y = _dot(a_ref[...], w_ref[...])
    o_ref[...] = _res_ln(x_ref[...], gate_ref[0], y, g_ref[...], b_ref[...], alpha)


def _ffn2(x2, gate, lng, lnb, act, w, seq, tm, alpha):
    n, d = x2.shape
    per_b = seq // tm
    row = lambda i: (i, 0)
    const = lambda i: (0, 0)
    return pl.pallas_call(
        functools.partial(_ffn2_kernel, alpha=alpha),
        out_shape=jax.ShapeDtypeStruct((n, d), F32),
        grid=(n // tm,),
        in_specs=[pl.BlockSpec((tm, d), row), pl.BlockSpec((1, 1, d), lambda i: (i // per_b, 0, 0)),
                  pl.BlockSpec((1, d), const), pl.BlockSpec((1, d), const),
                  pl.BlockSpec((tm, act.shape[1]), row), pl.BlockSpec(w.shape, const)],
        out_specs=pl.BlockSpec((tm, d), row),
        compiler_params=_cparams("parallel"),
        name="ffn_down_ln",
    )(x2, gate, lng, lnb, act, w)


MOE_TM = 1024
MOE_FC = 512
GATHER_ROWS = 1024


def _router_kernel(x_ref, sc_ref, sh_ref, rw_ref, h_ref, ti_ref, tw_ref):
    hf = x_ref[...] * (1.0 + sc_ref[0]) + sh_ref[0]
    h_ref[...] = hf
    lane = lax.broadcasted_iota(jnp.int32, (hf.shape[0], LANE), 1)
    logits = jnp.dot(hf, rw_ref[...], precision=HIGHEST, preferred_element_type=F32)
    logits = jnp.where(lane < N_EXPERTS, logits, -jnp.inf)
    m1 = jnp.max(logits, axis=-1, keepdims=True)
    i1 = jnp.min(jnp.where(logits == m1, lane, LANE), axis=-1, keepdims=True)
    rest = jnp.where(lane == i1, -jnp.inf, logits)
    m2 = jnp.max(rest, axis=-1, keepdims=True)
    i2 = jnp.min(jnp.where(rest == m2, lane, LANE), axis=-1, keepdims=True)
    e2 = jnp.exp(m2 - m1)
    ti_ref[...] = jnp.where(lane == 0, i1, jnp.where(lane == 1, i2, 0))
    tw_ref[...] = jnp.where(lane == 0, 1.0 / (1.0 + e2), jnp.where(lane == 1, e2 / (1.0 + e2), 0.0))


def _router(x2, scale, shift, router_p, seq, tm):
    n, d = x2.shape
    per_b = seq // tm
    row = lambda i: (i, 0)
    mod = lambda i: (i // per_b, 0, 0)
    return pl.pallas_call(
        _router_kernel,
        out_shape=(jax.ShapeDtypeStruct((n, d), F32), jax.ShapeDtypeStruct((n, LANE), jnp.int32),
                   jax.ShapeDtypeStruct((n, LANE), F32)),
        grid=(n // tm,),
        in_specs=[pl.BlockSpec((tm, d), row), pl.BlockSpec((1, 1, d), mod), pl.BlockSpec((1, 1, d), mod),
                  pl.BlockSpec(router_p.shape, lambda i: (0, 0))],
        out_specs=(pl.BlockSpec((tm, d), row), pl.BlockSpec((tm, LANE), row), pl.BlockSpec((tm, LANE), row)),
        compiler_params=_cparams("parallel"),
        name="moe_router",
    )(x2, scale, shift, router_p)


def _scatter_rows_kernel(pos_ref, h_ref, init_ref, o_ref, sem):
    del init_ref
    rows = h_ref.shape[0]
    n = pl.num_programs(0) * rows
    base = pl.program_id(0) * rows

    def issue(r, carry):
        row = h_ref.at[pl.ds(r, 1)]
        pltpu.make_async_copy(row, o_ref.at[pl.ds(pos_ref[base + r], 1)], sem).start()
        pltpu.make_async_copy(row, o_ref.at[pl.ds(pos_ref[n + base + r], 1)], sem).start()
        return carry

    lax.fori_loop(0, rows, issue, 0, unroll=8)
    for _ in range(2):
        pltpu.make_async_copy(h_ref, o_ref.at[pl.ds(0, rows)], sem).wait()


def _scatter_rows(pos, h, init):
    n, d = h.shape
    return pl.pallas_call(
        _scatter_rows_kernel,
        out_shape=jax.ShapeDtypeStruct(init.shape, h.dtype),
        grid_spec=pltpu.PrefetchScalarGridSpec(
            num_scalar_prefetch=1,
            grid=(n // GATHER_ROWS,),
            in_specs=[pl.BlockSpec((GATHER_ROWS, d), lambda i, pos_ref: (i, 0)),
                      pl.BlockSpec(memory_space=pl.ANY)],
            out_specs=pl.BlockSpec(memory_space=pl.ANY),
            scratch_shapes=[pltpu.SemaphoreType.DMA]),
        input_output_aliases={2: 0},
        compiler_params=_cparams("arbitrary"),
        name="row_scatter",
    )(pos, h, init)


def _expert_up_kernel(te_ref, nu_ref, x_ref, wg_ref, wu_ref, o_ref, xb_sc):
    i = pl.program_id(0)
    f = pl.program_id(1)

    @pl.when(i < nu_ref[0])
    def _compute():
        @pl.when(f == 0)
        def _cast():
            xb_sc[...] = x_ref[...].astype(BF16)
        h = xb_sc[...]
        gate = _dot(h, wg_ref[0, 0])
        up = _dot(h, wu_ref[0, 0])
        o_ref[...] = (jax.nn.silu(gate) * up).astype(BF16)

    @pl.when(i >= nu_ref[0])
    def _unused_tile():
        o_ref[...] = jnp.zeros_like(o_ref)


def _expert_up(tile_expert, n_used, xg, w1b, layer):
    p_rows, d = xg.shape
    nf = w1b.shape[3] // (2 * MOE_FC)
    return pl.pallas_call(
        _expert_up_kernel,
        out_shape=jax.ShapeDtypeStruct((p_rows, nf * MOE_FC), BF16),
        grid_spec=pltpu.PrefetchScalarGridSpec(
            num_scalar_prefetch=2,
            grid=(p_rows // MOE_TM, nf),
            in_specs=[pl.BlockSpec((MOE_TM, d), lambda i, f, te, nu: (i, 0)),
                      pl.BlockSpec((1, 1, d, MOE_FC), lambda i, f, te, nu: (layer, te[i], 0, f)),
                      pl.BlockSpec((1, 1, d, MOE_FC), lambda i, f, te, nu: (layer, te[i], 0, nf + f))],
            out_specs=pl.BlockSpec((MOE_TM, MOE_FC), lambda i, f, te, nu: (i, f)),
            scratch_shapes=[pltpu.VMEM((MOE_TM, d), BF16)]),
        compiler_params=_cparams("parallel", "arbitrary"),
        name="moe_up",
    )(tile_expert, n_used, xg, w1b, w1b)


def _expert_down_kernel(te_ref, nu_ref, a_ref, w_ref, o_ref):
    i = pl.program_id(0)

    @pl.when(i < nu_ref[0])
    def _compute():
        o_ref[...] = _dot(a_ref[...], w_ref[0, 0])

    @pl.when(i >= nu_ref[0])
    def _unused_tile():
        o_ref[...] = jnp.zeros_like(o_ref)


def _expert_down(tile_expert, n_used, act, w2b, layer):
    p_rows, dffe = act.shape
    d = w2b.shape[3]
    return pl.pallas_call(
        _expert_down_kernel,
        out_shape=jax.ShapeDtypeStruct((p_rows, d), F32),
        grid_spec=pltpu.PrefetchScalarGridSpec(
            num_scalar_prefetch=2,
            grid=(p_rows // MOE_TM,),
            in_specs=[pl.BlockSpec((MOE_TM, dffe), lambda i, te, nu: (i, 0)),
                      pl.BlockSpec((1, 1, dffe, d), lambda i, te, nu: (layer, te[i], 0, 0))],
            out_specs=pl.BlockSpec((MOE_TM, d), lambda i, te, nu: (i, 0))),
        compiler_params=_cparams("parallel"),
        name="moe_down",
    )(tile_expert, n_used, act, w2b)


def _moe_combine_kernel(pos_ref, x_ref, gate_ref, g_ref, b_ref, tw_ref, y_ref, o_ref, ybuf, sems, *, alpha):
    i = pl.program_id(0)
    nt = pl.num_programs(0)
    tm = x_ref.shape[0]
    n = nt * tm

    def fetch(step, slot):
        def issue(r, carry):
            t = step * tm + r
            pltpu.make_async_copy(y_ref.at[pl.ds(pos_ref[t], 1)], ybuf.at[slot, 0, pl.ds(r, 1)],
                                  sems.at[slot]).start()
            pltpu.make_async_copy(y_ref.at[pl.ds(pos_ref[n + t], 1)], ybuf.at[slot, 1, pl.ds(r, 1)],
                                  sems.at[slot]).start()
            return carry
        lax.fori_loop(0, tm, issue, 0, unroll=8)

    @pl.when(i == 0)
    def _first():
        fetch(0, 0)

    @pl.when(i + 1 < nt)
    def _next():
        fetch(i + 1, (i + 1) % 2)

    slot = i % 2
    for choice in range(2):
        pltpu.make_async_copy(y_ref.at[pl.ds(0, tm)], ybuf.at[slot, choice], sems.at[slot]).wait()
    tw = tw_ref[...]
    y = tw[:, 0:1] * ybuf[slot, 0] + tw[:, 1:2] * ybuf[slot, 1]
    o_ref[...] = _res_ln(x_ref[...], gate_ref[0], y, g_ref[...], b_ref[...], alpha)


def _moe_combine(pos, x2, gate, lng, lnb, tw, y, seq, tm, alpha):
    n, d = x2.shape
    per_b = seq // tm
    row = lambda i, p: (i, 0)
    const = lambda i, p: (0, 0)
    return pl.pallas_call(
        functools.partial(_moe_combine_kernel, alpha=alpha),
        out_shape=jax.ShapeDtypeStruct((n, d), F32),
        grid_spec=pltpu.PrefetchScalarGridSpec(
            num_scalar_prefetch=1,
            grid=(n // tm,),
            in_specs=[pl.BlockSpec((tm, d), row), pl.BlockSpec((1, 1, d), lambda i, p: (i // per_b, 0, 0)),
                      pl.BlockSpec((1, d), const), pl.BlockSpec((1, d), const), pl.BlockSpec((tm, LANE), row),
                      pl.BlockSpec(memory_space=pl.ANY)],
            out_specs=pl.BlockSpec((tm, d), row),
            scratch_shapes=[pltpu.VMEM((2, 2, tm, d), F32), pltpu.SemaphoreType.DMA((2,))]),
        compiler_params=_cparams("arbitrary"),
        name="moe_combine_ln",
    )(pos, x2, gate, lng, lnb, tw, y)


def _route_tables(ti, n_rows):
    e_flat = jnp.concatenate([ti[:, 0], ti[:, 1]])
    onehot = (e_flat[:, None] == jnp.arange(N_EXPERTS, dtype=jnp.int32)[None, :]).astype(jnp.int32)
    cum = jnp.cumsum(onehot, axis=0)
    counts = cum[-1]
    padded = (counts + MOE_TM - 1) // MOE_TM * MOE_TM
    ends = jnp.cumsum(padded)
    pos = jnp.sum(onehot * (cum - 1 + (ends - padded)[None, :]), axis=1)
    tile_start = jnp.arange(n_rows // MOE_TM, dtype=jnp.int32) * MOE_TM
    tile_expert = jnp.minimum(jnp.sum((tile_start[:, None] >= ends[None, :]).astype(jnp.int32), axis=1),
                              N_EXPERTS - 1)
    return pos.astype(jnp.int32), tile_expert, (ends[-1:] // MOE_TM).astype(jnp.int32)


def _moe(x2, scale, shift, gate, lng, lnb, router_p, w1b, w2b, layer, seq, tm, alpha, grouped_buf):
    n, d = x2.shape
    n_rows = 2 * n + N_EXPERTS * MOE_TM
    assert n % GATHER_ROWS == 0
    if grouped_buf is None:
        grouped_buf = jnp.zeros((n_rows, d), F32)
    h, ti, tw = _router(x2, scale, shift, router_p, seq, tm)
    pos, tile_expert, n_used = _route_tables(ti, n_rows)
    xg = _scatter_rows(pos, h, grouped_buf)
    act = _expert_up(tile_expert, n_used, xg, w1b, layer)
    y = _expert_down(tile_expert, n_used, act, w2b, layer)
    return _moe_combine(pos, x2, gate, lng, lnb, tw, y, seq, tm, alpha), xg


def _rel_bucket_np(dist):
    n = np.maximum(dist, 0)
    max_exact = NUM_BUCKETS // 2
    nf = np.maximum(n, 1).astype(np.float32)
    large = max_exact + (np.log(nf / np.float32(max_exact)) / np.float32(math.log(MAX_DISTANCE / max_exact))
                         * np.float32(NUM_BUCKETS - max_exact)).astype(np.int32)
    large = np.minimum(large, NUM_BUCKETS - 1)
    return np.where(n < max_exact, n, large).astype(np.int32)


def _bias_lookup(rel_bias, dist, valid):
    onehot = _rel_bucket_np(dist)[..., None] == np.arange(NUM_BUCKETS)
    vals = jnp.einsum('...k,kh->...h', jnp.asarray(onehot).astype(F32), rel_bias, precision=HIGHEST)
    return jnp.where(jnp.asarray(valid)[..., None], vals * LOG2E, NEG_INF)


def _key_major(vals, add_masked_tile=False):
    n_tiles, tq, tk, _ = vals.shape
    if add_masked_tile:
        vals = jnp.concatenate([vals, jnp.full((1,) + vals.shape[1:], NEG_INF, vals.dtype)], axis=0)
        n_tiles += 1
    vals = vals.reshape(n_tiles, tq, tk, N_KV, Q_PER_KV).transpose(3, 0, 2, 4, 1)
    return vals.reshape(N_KV, n_tiles, tk, Q_PER_KV * tq).astype(F32)


def _band_dist(n_delta):
    i = np.arange(ATT_TQ)[None, :, None]
    k = np.arange(ATT_TQ)[None, None, :]
    return ATT_TQ * np.arange(n_delta)[:, None, None] + i - k


def _cmp_bias_table(rel_bias, n_qt, n_cmp_pad):
    per_tile = ATT_TQ // CMP_STRIDE
    shift = per_tile * (n_qt - 1)
    rel_n = np.arange(n_cmp_pad + shift) - shift
    dist = np.arange(ATT_TQ)[:, None] - (CMP_STRIDE * rel_n[None, :] + L_CMP - 1)
    base = _bias_lookup(rel_bias, dist, dist >= 0)
    tiles = [base[:, shift - per_tile * qt: shift - per_tile * qt + n_cmp_pad] for qt in range(n_qt)]
    return _key_major(jnp.stack(tiles))


def _inproj_weight(w_in):
    depth, d, _ = w_in.shape
    d_nsa = N_HEADS * HEAD_DIM
    d_kv = N_KV * HEAD_DIM
    base_ks = d_nsa + 2 * d_kv
    base_gl = d_nsa + 6 * d_kv
    base_cv = base_gl + N_HEADS * N_BRANCH
    wq = jnp.pad(w_in[:, :, :d_nsa].reshape(depth, d, N_HEADS, HEAD_DIM),
                 ((0, 0), (0, 0), (0, 0), (0, LANE - HEAD_DIM))).reshape(depth, d, Q_COLS)
    parts = [wq, w_in[:, :, d_nsa:base_ks]]
    for pair in range(2):
        kbase = base_ks + pair * 2 * d_kv
        for g in range(N_KV):
            parts.append(w_in[:, :, kbase + g * HEAD_DIM: kbase + (g + 1) * HEAD_DIM])
            parts.append(w_in[:, :, kbase + d_kv + g * HEAD_DIM: kbase + d_kv + (g + 1) * HEAD_DIM])
    parts.append(w_in[:, :, base_cv:])
    gl = w_in[:, :, base_gl:base_cv].reshape(depth, d, N_KV, Q_PER_KV, N_BRANCH).transpose(0, 2, 4, 3, 1)
    gl = jnp.pad(gl.reshape(depth, N_KV, N_BRANCH * Q_PER_KV, d),
                 ((0, 0), (0, 0), (0, LANE - N_BRANCH * Q_PER_KV), (0, 0)))
    return jnp.concatenate(parts, axis=-1).astype(BF16), gl.reshape(depth, GL_ROWS, d).astype(BF16)


def kernel(x, c, w_in, cmp_pos, cmp_w1, cmp_w2, conv_w, conv_b, conv_ln_g, conv_ln_b, w_out, rel_bias,
           ada_w, ada_b, ln_g, ln_b, ffn_w1, ffn_w2, router_w, moe_w1, moe_w2):
    bsz, seq, d = x.shape
    depth = w_in.shape[0]
    n = bsz * seq
    alpha = (2 * depth) ** 0.25
    n_blocks = seq // L_SEL
    n_sel = min(N_SEL, n_blocks)
    n_cmp = (seq - L_CMP) // CMP_STRIDE + 1
    nrow = seq // CMP_STRIDE
    assert d == 1024 and seq % 512 == 0 and nrow == LANE and n_cmp == nrow - 1
    tm = 512

    n_qt = seq // ATT_TQ
    d_win = _band_dist(WINDOW // ATT_TQ + 1)
    win_tab = _key_major(_bias_lookup(rel_bias, d_win, (d_win >= 0) & (d_win < WINDOW)), add_masked_tile=True)
    d_sel = _band_dist(3)
    assert _rel_bucket_np(d_sel[2]).min() == NUM_BUCKETS - 1
    sel_tab = _key_major(_bias_lookup(rel_bias, d_sel, d_sel >= 0), add_masked_tile=True)
    cmp_tab = _cmp_bias_table(rel_bias, n_qt, nrow)

    w_in_p, w_gl_t = _inproj_weight(w_in)
    eye2 = jnp.eye(2, dtype=F32)
    w1r = cmp_w1.reshape(depth, 2, L_CMP, HEAD_DIM, HEAD_DIM).astype(BF16)
    zero_blk = jnp.zeros((depth, L_CMP, HEAD_DIM, HEAD_DIM), BF16)
    row_blocks = []
    for s in range(2):
        for g in range(N_KV):
            col_blocks = [zero_blk] * (2 * N_KV)
            col_blocks[g * 2 + s] = w1r[:, s]
            row_blocks.append(jnp.concatenate(col_blocks, axis=-1))
    w1x = jnp.concatenate(row_blocks, axis=2)
    half = L_CMP // 2
    w_lo = w1x[:, :half]
    w_hi = w1x[:, half:]
    w2bd = jnp.einsum('zsde,st,gh->zgsdhte', cmp_w2, eye2, eye2).reshape(depth, 4 * HEAD_DIM, 4 * HEAD_DIM).astype(BF16)
    posx = jnp.broadcast_to(cmp_pos[:, :, :, None, :], (depth, 2, L_CMP, N_KV, HEAD_DIM)).transpose(0, 2, 1, 3, 4)
    pos_lo = posx[:, :half].reshape(depth, half, 4 * HEAD_DIM)
    pos_hi = posx[:, half:].reshape(depth, half, 4 * HEAD_DIM)
    conv_wp = jnp.broadcast_to(conv_w[:, :, None, :], conv_w.shape[:2] + (SUBLANE, conv_w.shape[2]))
    w_out_b = w_out.astype(BF16)
    dff = ffn_w2.shape[1]
    ffn_wg = ffn_w1[:, :, :dff].astype(BF16)
    ffn_wu = ffn_w1[:, :, dff:].astype(BF16)
    ffn_w2b = ffn_w2.astype(BF16)
    router_p = jnp.pad(router_w, ((0, 0), (0, 0), (0, LANE - N_EXPERTS)))
    moe_w1b = moe_w1.astype(BF16)
    moe_w2b = moe_w2.astype(BF16)

    mod = _ada(c, ada_w, ada_b).reshape(depth, 6, bsz, 1, d)
    x2 = x.reshape(n, d)
    grouped_buf = None
    for l in range(depth):
        shift1, scale1, gate1, shift2, scale2, gate2 = [mod[l, k] for k in range(6)]
        q, kvc, kvsw, gl3, cv = _inproj(x2, scale1, shift1, w_in_p[l], w_gl_t[l], seq, tm)
        q3 = q.reshape(bsz, seq, Q_COLS)
        kvsw3 = kvsw.reshape(bsz, seq, KVSW_COLS)
        kvsw_t = kvsw3.reshape(bsz, n_qt, ATT_TQ, 2 * N_KV, LANE).transpose(0, 3, 1, 4, 2)
        kvcmp, kvcmp_t = _compress(kvc.reshape(bsz, seq, KVC_COLS), pos_lo[l], pos_hi[l], w_lo[l], w_hi[l],
                                   w2bd[l])
        o_nsa = _nsa_attn(q3, kvcmp, kvcmp_t, cmp_tab, kvsw3, kvsw_t, sel_tab, win_tab, gl3, n_blocks, n_sel)
        o_conv = _conv_branch(cv.reshape(bsz, seq, 1024), conv_wp[l], conv_b[l][None], conv_ln_g[l][None],
                              conv_ln_b[l][None])
        x2 = _outproj(x2, gate1, ln_g[l, 0][None], ln_b[l, 0][None], o_nsa.reshape(n, 512), o_conv.reshape(n, 512),
                      w_out_b[l], seq, tm, alpha)
        if l % 2 == 0:
            act = _ffn1(x2, scale2, shift2, ffn_wg[l // 2], ffn_wu[l // 2], seq, tm)
            x2 = _ffn2(x2, gate2, ln_g[l, 1][None], ln_b[l, 1][None], act, ffn_w2b[l // 2], seq, tm, alpha)
        else:
            x2, grouped_buf = _moe(x2, scale2, shift2, gate2, ln_g[l, 1][None], ln_b[l, 1][None], router_p[l // 2],
                                   moe_w1b, moe_w2b, l // 2, seq, tm, alpha, grouped_buf)
    return x2.reshape(bsz, seq, d)
```

```python
import functools
import math

import numpy as np
import jax
import jax.numpy as jnp
from jax import lax
from jax.experimental import pallas as pl
from jax.experimental.pallas import tpu as pltpu

F32 = jnp.float32
BF16 = jnp.bfloat16
HIGHEST = lax.Precision.HIGHEST

HEAD_DIM = 64
N_HEADS = 8
N_KV = 2
Q_PER_KV = N_HEADS // N_KV
N_BRANCH = 3
L_CMP = 32
CMP_STRIDE = 16
L_SEL = 64
N_SEL = 8
WINDOW = 512
CONV_WIDTH = 31
NUM_BUCKETS = 32
MAX_DISTANCE = 128
N_EXPERTS = 8
LN_EPS = 1e-5
NEG_INF = -1e30
LOG2E = math.log2(math.e)

LANE = 128
SUBLANE = 8
ATT_TQ = 128
ATT_ROWS = Q_PER_KV * ATT_TQ
CONV_TQ = 256
CONV_HALO = 32
CONV_RC = 64
VMEM_LIMIT_BYTES = 52 * 1024 * 1024


def _cparams(*sem):
    return pltpu.CompilerParams(dimension_semantics=sem, vmem_limit_bytes=VMEM_LIMIT_BYTES)


def _dot(a, b):
    return jnp.dot(a, b, preferred_element_type=F32)


def _dot_nt(a, b):
    return lax.dot_general(a, b, (((1,), (1,)), ((), ())), preferred_element_type=F32)


def _res_ln(x, gate, y, g, b, alpha):
    z = alpha * x + (1.0 + gate) * y
    mu = jnp.mean(z, axis=-1, keepdims=True)
    zc = z - mu
    var = jnp.mean(zc * zc, axis=-1, keepdims=True)
    return zc * lax.rsqrt(var + LN_EPS) * g + b


def _ada_kernel(c_ref, w_ref, b_ref, o_ref):
    sc = jax.nn.silu(c_ref[...])
    o_ref[0, 0] = jnp.dot(sc, w_ref[0], precision=HIGHEST, preferred_element_type=F32) + b_ref[0]


def _ada(c, ada_w, ada_b):
    depth, d, _ = ada_w.shape
    bsz = c.shape[0]
    return pl.pallas_call(
        _ada_kernel,
        out_shape=jax.ShapeDtypeStruct((depth, 6, bsz, d), F32),
        grid=(depth, 6),
        in_specs=[pl.BlockSpec((bsz, d), lambda l, k: (0, 0)),
                  pl.BlockSpec((1, d, d), lambda l, k: (l, 0, k)),
                  pl.BlockSpec((1, 1, d), lambda l, k: (l, 0, k))],
        out_specs=pl.BlockSpec((1, 1, bsz, d), lambda l, k: (l, k, 0, 0)),
        compiler_params=_cparams("parallel", "parallel"),
        name="ada_mod",
    )(c, ada_w, ada_b.reshape(depth, 1, 6 * d))


Q_COLS = N_HEADS * LANE
KVC_COLS = 4 * HEAD_DIM
KVSW_COLS = 8 * HEAD_DIM
GL_ROWS = N_KV * LANE


def _inproj_kernel(x_ref, sc_ref, sh_ref, w_ref, wgl_ref, q_ref, kvc_ref, kvsw_ref, glt_ref, cv_ref):
    h = (x_ref[...] * (1.0 + sc_ref[0]) + sh_ref[0]).astype(BF16)
    c0, c1, c2 = Q_COLS, Q_COLS + KVC_COLS, Q_COLS + KVC_COLS + KVSW_COLS
    q_ref[...] = (_dot(h, w_ref[:, 0:c0]) * (HEAD_DIM ** -0.5 * LOG2E)).astype(BF16)
    kvc_ref[...] = _dot(h, w_ref[:, c0:c1])
    kvsw_ref[...] = _dot(h, w_ref[:, c1:c2]).astype(BF16)
    cv_ref[...] = _dot(h, w_ref[:, c2:])
    glt_ref[...] = _dot_nt(wgl_ref[...], h)


def _inproj(x2, scale, shift, w, wgl_t, seq, tm):
    n, d = x2.shape
    per_b = seq // tm
    d_cv = w.shape[1] - (Q_COLS + KVC_COLS + KVSW_COLS)
    row = lambda i: (i, 0)
    mod = lambda i: (i // per_b, 0, 0)
    const = lambda i: (0, 0)
    return pl.pallas_call(
        _inproj_kernel,
        out_shape=(jax.ShapeDtypeStruct((n, Q_COLS), BF16), jax.ShapeDtypeStruct((n, KVC_COLS), F32),
                   jax.ShapeDtypeStruct((n, KVSW_COLS), BF16), jax.ShapeDtypeStruct((GL_ROWS, n), F32),
                   jax.ShapeDtypeStruct((n, d_cv), F32)),
        grid=(n // tm,),
        in_specs=[pl.BlockSpec((tm, d), row), pl.BlockSpec((1, 1, d), mod), pl.BlockSpec((1, 1, d), mod),
                  pl.BlockSpec(w.shape, const), pl.BlockSpec(wgl_t.shape, const)],
        out_specs=(pl.BlockSpec((tm, Q_COLS), row), pl.BlockSpec((tm, KVC_COLS), row),
                   pl.BlockSpec((tm, KVSW_COLS), row), pl.BlockSpec((GL_ROWS, tm), lambda i: (0, i)),
                   pl.BlockSpec((tm, d_cv), row)),
        compiler_params=_cparams("parallel"),
        name="in_proj",
    )(x2, scale, shift, w, wgl_t)


def _compress_kernel(ak_ref, av_ref, plo_ref, phi_ref, wlo_ref, whi_ref, w2_ref, o_ref, ot_ref):
    nrow = o_ref.shape[1]
    lo = jnp.zeros((nrow, o_ref.shape[2]), F32)
    hi = jnp.zeros((nrow, o_ref.shape[2]), F32)
    for l in range(CMP_STRIDE):
        for part, a_ref in enumerate((ak_ref, av_ref)):
            lanes = slice(part * LANE, (part + 1) * LANE)
            tok = a_ref[0, pl.ds(l, nrow, stride=CMP_STRIDE), :]
            lo = lo + _dot((tok + plo_ref[l:l + 1, lanes]).astype(BF16), wlo_ref[l, lanes, :])
            hi = hi + _dot((tok + phi_ref[l:l + 1, lanes]).astype(BF16), whi_ref[l, lanes, :])
    pre = lo + pltpu.roll(hi, nrow - 1, 0)
    act = jax.nn.gelu(pre, approximate=True)
    out = _dot(act.astype(BF16), w2_ref[...])
    o_ref[0] = out.astype(BF16)
    ot_ref[0] = out.T.astype(BF16)


def _compress(kvc3, pos_lo, pos_hi, wlo, whi, w2bd):
    bsz, seq, width = kvc3.shape
    nrow = seq // CMP_STRIDE
    const = lambda b: (0, 0)
    const3 = lambda b: (0, 0, 0)
    return pl.pallas_call(
        _compress_kernel,
        out_shape=(jax.ShapeDtypeStruct((bsz, nrow, 256), BF16), jax.ShapeDtypeStruct((bsz, 256, nrow), BF16)),
        grid=(bsz,),
        in_specs=[pl.BlockSpec((1, seq, LANE), lambda b: (b, 0, 0)), pl.BlockSpec((1, seq, LANE), lambda b: (b, 0, 1)),
                  pl.BlockSpec(pos_lo.shape, const), pl.BlockSpec(pos_hi.shape, const),
                  pl.BlockSpec(wlo.shape, const3), pl.BlockSpec(whi.shape, const3),
                  pl.BlockSpec((256, 256), const)],
        out_specs=(pl.BlockSpec((1, nrow, 256), lambda b: (b, 0, 0)),
                   pl.BlockSpec((1, 256, nrow), lambda b: (b, 0, 0))),
        compiler_params=_cparams("parallel"),
        name="compress_mlp",
    )(kvc3, kvc3, pos_lo, pos_hi, wlo, whi, w2bd)


def _stack_heads(q):
    return jnp.concatenate([q[:, r * LANE:(r + 1) * LANE] for r in range(Q_PER_KV)], axis=0)


def _branch_gates(gl_t, branch):
    return jnp.concatenate([jax.nn.sigmoid(gl_t[branch * Q_PER_KV + r:branch * Q_PER_KV + r + 1, :])
                            for r in range(Q_PER_KV)], axis=1)


def _heads_to_rows(o_t):
    tq = o_t.shape[1] // Q_PER_KV
    og = o_t.astype(BF16)
    ii = lax.broadcasted_iota(jnp.int32, (tq, tq), 0)
    kk = lax.broadcasted_iota(jnp.int32, (tq, tq), 1)
    eye = jnp.where(ii == kk, 1.0, 0.0).astype(BF16)
    halves = []
    for pair in range(Q_PER_KV // 2):
        two_heads = jnp.concatenate([og[:, (2 * pair) * tq:(2 * pair + 1) * tq],
                                     og[:, (2 * pair + 1) * tq:(2 * pair + 2) * tq]], axis=0)
        halves.append(_dot_nt(eye, two_heads))
    return jnp.concatenate(halves, axis=1).astype(BF16)


def _cmp_section(qs, qt, kv_ref, kvt_ref, tab_ref, gl_ref, sel_sc, *, n_blocks, n_sel):
    tq = ATT_TQ
    outs = []
    jj = lax.broadcasted_iota(jnp.int32, (n_blocks, LANE), 0)
    nn = lax.broadcasted_iota(jnp.int32, (n_blocks, LANE), 1)
    overlap_t = jnp.where((CMP_STRIDE * nn < L_SEL * jj + L_SEL) & (CMP_STRIDE * nn + L_CMP > L_SEL * jj),
                          1.0, 0.0).astype(F32)
    jb = lax.broadcasted_iota(jnp.int32, (n_blocks, tq), 0)
    tb = (qt * tq + lax.broadcasted_iota(jnp.int32, (n_blocks, tq), 1)) // L_SEL
    for g in range(N_KV):
        bias = tab_ref[g, 0]
        s = _dot_nt(kv_ref[0, :, g * LANE:(g + 1) * LANE], qs[g]) + bias
        valid = bias > 0.5 * NEG_INF
        m = jnp.max(s, axis=0, keepdims=True)
        p = jnp.where(valid, jnp.exp2(s - m), 0.0)
        l = jnp.sum(p, axis=0, keepdims=True)
        p = p * (1.0 / jnp.where(l > 0.0, l, 1.0))
        o_t = _dot(kvt_ref[0, g * LANE + HEAD_DIM:(g + 1) * LANE, :], p.astype(BF16))
        outs.append(o_t * _branch_gates(gl_ref[g * LANE:(g + 1) * LANE, :], 0))

        psum = p[:, 0:tq] + p[:, tq:2 * tq] + p[:, 2 * tq:3 * tq] + p[:, 3 * tq:4 * tq]
        imp_t = jnp.dot(overlap_t, psum, precision=HIGHEST, preferred_element_type=F32)
        score = jnp.where(jb > tb, -jnp.inf,
                          jnp.where((jb == 0) | (jb == tb) | (jb == tb - 1), jnp.inf, imp_t))
        rank = jnp.zeros((n_blocks, tq), F32)
        for jp in range(n_blocks):
            row = score[jp:jp + 1, :]
            earlier = jnp.where(jb > jp, 1.0, 0.0)
            rank = rank + jnp.where(row > score, 1.0, 0.0) + jnp.where(row == score, earlier, 0.0)
        madd = jnp.where(rank < float(n_sel), 0.0, NEG_INF)
        per_chunk = ATT_TQ // L_SEL
        for c in range(n_blocks // per_chunk):
            sel_sc[g, c, 0:per_chunk, :] = madd[c * per_chunk:(c + 1) * per_chunk, :]
    return outs


def _band_section(qs, qt, kv_ref, kvt_ref, tab_ref, gl_ref, sel_sc, s_sc, *, span, group, branch):
    tq = ATT_TQ
    n_delta = tab_ref.shape[1] - 1
    n_chunks = kv_ref.shape[1] // tq
    c_base = qt - span if span is not None else 0
    use_sel = sel_sc is not None

    def chunk_ids(gi, u):
        c = c_base + gi * group + u
        inside = (c >= 0) & (c <= qt)
        dd = jnp.where(inside, jnp.minimum(qt - c, n_delta - 1), n_delta)
        return jnp.clip(c, 0, n_chunks - 1), dd

    def rows(gi, u):
        return pl.ds(pl.multiple_of((gi * group + u) * tq, tq), tq)

    def scores(gi, ms):
        ms = list(ms)
        for u in range(group):
            cc, dd = chunk_ids(gi, u)
            kv = kv_ref[0, pl.ds(pl.multiple_of(cc * tq, tq), tq), :]
            for g in range(N_KV):
                s = _dot_nt(kv[:, g * LANE:(g + 1) * LANE], qs[g]) + tab_ref[g, dd]
                if use_sel:
                    blocks = sel_sc[g, cc, 0:tq // L_SEL, :]
                    madd = jnp.concatenate([jnp.broadcast_to(blocks[j:j + 1, :], (L_SEL, tq))
                                            for j in range(tq // L_SEL)], axis=0)
                    s = s + jnp.concatenate([madd] * Q_PER_KV, axis=1)
                s_sc[g, rows(gi, u), :] = s
                ms[g] = jnp.maximum(ms[g], jnp.max(s, axis=0, keepdims=True))
        return tuple(ms)

    m0 = tuple(jnp.full((1, ATT_ROWS), NEG_INF, F32) for _ in range(N_KV))
    if span is not None:
        n_groups = 1
        m = scores(0, m0)
    else:
        n_groups = qt // group + 1
        m = lax.fori_loop(0, n_groups, scores, m0)

    def values(gi, carry):
        carry = [list(c) for c in carry]
        for u in range(group):
            cc, _ = chunk_ids(gi, u)
            for g in range(N_KV):
                l, acc = carry[g]
                p = jnp.exp2(s_sc[g, rows(gi, u), :] - m[g])
                l = l + jnp.sum(p, axis=0, keepdims=True)
                acc = acc + _dot(kvt_ref[0, g, cc, HEAD_DIM:, :], p.astype(BF16))
                carry[g] = [l, acc]
        return tuple(tuple(c) for c in carry)

    init = tuple((jnp.zeros((1, ATT_ROWS), F32), jnp.zeros((HEAD_DIM, ATT_ROWS), F32)) for _ in range(N_KV))
    if span is not None:
        out = values(0, init)
    else:
        out = lax.fori_loop(0, n_groups, values, init)
    return [out[g][1] * (_branch_gates(gl_ref[g * LANE:(g + 1) * LANE, :], branch) / out[g][0]) for g in range(N_KV)]


SEL_GROUP = 4
WIN_SPAN = WINDOW // ATT_TQ


def _nsa_kernel(q_ref, kvc_ref, kvct_ref, ctab_ref, kvs_ref, kvst_ref, stab_ref, kvw_ref, kvwt_ref, wtab_ref,
                gl_ref, o_ref, sel_sc, ssel_sc, swin_sc, *, n_blocks, n_sel):
    qt = pl.program_id(1)
    hw = Q_PER_KV * LANE
    qs = [_stack_heads(q_ref[0, :, g * hw:(g + 1) * hw]) for g in range(N_KV)]
    o_cmp = _cmp_section(qs, qt, kvc_ref, kvct_ref, ctab_ref, gl_ref, sel_sc, n_blocks=n_blocks, n_sel=n_sel)
    o_sel = _band_section(qs, qt, kvs_ref, kvst_ref, stab_ref, gl_ref, sel_sc, ssel_sc,
                          span=None, group=SEL_GROUP, branch=1)
    o_win = _band_section(qs, qt, kvw_ref, kvwt_ref, wtab_ref, gl_ref, None, swin_sc,
                          span=WIN_SPAN, group=WIN_SPAN + 1, branch=2)
    for g in range(N_KV):
        o_ref[0, :, g * 256:(g + 1) * 256] = _heads_to_rows(o_cmp[g] + o_sel[g] + o_win[g])


def _nsa_attn(q3, kvcmp, kvcmp_t, cmp_tab, kvsw3, kvsw_t, sel_tab, win_tab, gl_t, n_blocks, n_sel):
    bsz, seq, _ = q3.shape
    nqt = seq // ATT_TQ
    n_cmp = kvcmp.shape[1]
    sel_slots = -(-nqt // SEL_GROUP) * SEL_GROUP
    full4 = lambda b, t: (0, 0, 0, 0)
    kv_spec = lambda pair: pl.BlockSpec((1, seq, N_KV * LANE), lambda b, t: (b, 0, pair))
    kvt_spec = lambda pair: pl.BlockSpec((1, N_KV) + kvsw_t.shape[2:], lambda b, t: (b, pair, 0, 0, 0))
    return pl.pallas_call(
        functools.partial(_nsa_kernel, n_blocks=n_blocks, n_sel=n_sel),
        out_shape=jax.ShapeDtypeStruct((bsz, seq, 512), BF16),
        grid=(bsz, nqt),
        in_specs=[pl.BlockSpec((1, ATT_TQ, Q_COLS), lambda b, t: (b, t, 0)),
                  pl.BlockSpec((1, n_cmp, N_KV * LANE), lambda b, t: (b, 0, 0)),
                  pl.BlockSpec((1, N_KV * LANE, n_cmp), lambda b, t: (b, 0, 0)),
                  pl.BlockSpec((N_KV, 1, n_cmp, ATT_ROWS), lambda b, t: (0, t, 0, 0)),
                  kv_spec(0), kvt_spec(0), pl.BlockSpec(sel_tab.shape, full4),
                  kv_spec(1), kvt_spec(1), pl.BlockSpec(win_tab.shape, full4),
                  pl.BlockSpec((GL_ROWS, ATT_TQ), lambda b, t: (0, b * nqt + t))],
        out_specs=pl.BlockSpec((1, ATT_TQ, 512), lambda b, t: (b, t, 0)),
        scratch_shapes=[pltpu.VMEM((N_KV, n_blocks * L_SEL // ATT_TQ, SUBLANE, ATT_TQ), F32),
                        pltpu.VMEM((N_KV, sel_slots * ATT_TQ, ATT_ROWS), F32),
                        pltpu.VMEM((N_KV, (WIN_SPAN + 1) * ATT_TQ, ATT_ROWS), F32)],
        compiler_params=_cparams("parallel", "parallel"),
        name="nsa_attn",
    )(q3, kvcmp, kvcmp_t, cmp_tab, kvsw3, kvsw_t, sel_tab, kvsw3, kvsw_t, win_tab, gl_t)


def _conv_kernel(cur_ref, prev_ref, w_ref, cb_ref, g_ref, b_ref, o_ref, ybuf, ysh, cbuf):
    i = pl.program_id(1)
    dc = o_ref.shape[-1]
    cur = cur_ref[0]
    prev = prev_ref[0]
    ybuf[CONV_HALO:, :] = cur[:, :dc] * jax.nn.sigmoid(cur[:, dc:])
    halo = prev[:, :dc] * jax.nn.sigmoid(prev[:, dc:])
    ybuf[0:CONV_HALO, :] = jnp.where(i > 0, halo, 0.0)
    n_sh = ysh.shape[1]
    for b in range(1, SUBLANE):
        ysh[b - 1] = ybuf[b:b + n_sh, :]
    off = CONV_HALO - (CONV_WIDTH - 1)
    sub = CONV_RC // SUBLANE
    n_rc = CONV_TQ // CONV_RC
    for lc in range(dc // LANE):
        cols = slice(lc * LANE, (lc + 1) * LANE)
        accs = [jnp.zeros((sub, SUBLANE, LANE), F32) for _ in range(n_rc)]
        for k in range(CONV_WIDTH):
            b = (off + k) % SUBLANE
            wk = w_ref[k, :, cols][None]
            for rc in range(n_rc):
                r0 = rc * CONV_RC + (off + k) - b
                src = ybuf[r0:r0 + CONV_RC, cols] if b == 0 else ysh[b - 1, r0:r0 + CONV_RC, cols]
                accs[rc] = accs[rc] + wk * src.reshape(sub, SUBLANE, LANE)
        for rc in range(n_rc):
            cbuf[rc * CONV_RC:(rc + 1) * CONV_RC, cols] = accs[rc].reshape(CONV_RC, LANE)
    z = cbuf[...] + cb_ref[...]
    mu = jnp.mean(z, axis=-1, keepdims=True)
    zc = z - mu
    var = jnp.mean(zc * zc, axis=-1, keepdims=True)
    y = zc * lax.rsqrt(var + LN_EPS) * g_ref[...] + b_ref[...]
    o_ref[0] = jax.nn.silu(y).astype(BF16)


def _conv_branch(cv3, w, cb, g, b):
    bsz, seq, two_dc = cv3.shape
    dc = two_dc // 2
    per = CONV_TQ // CONV_HALO
    const = lambda bb, i: (0, 0)
    return pl.pallas_call(
        _conv_kernel,
        out_shape=jax.ShapeDtypeStruct((bsz, seq, dc), BF16),
        grid=(bsz, seq // CONV_TQ),
        in_specs=[pl.BlockSpec((1, CONV_TQ, two_dc), lambda bb, i: (bb, i, 0)),
                  pl.BlockSpec((1, CONV_HALO, two_dc), lambda bb, i: (bb, jnp.maximum(i * per - 1, 0), 0)),
                  pl.BlockSpec(w.shape, lambda bb, i: (0, 0, 0)), pl.BlockSpec((1, dc), const),
                  pl.BlockSpec((1, dc), const), pl.BlockSpec((1, dc), const)],
        out_specs=pl.BlockSpec((1, CONV_TQ, dc), lambda bb, i: (bb, i, 0)),
        scratch_shapes=[pltpu.VMEM((CONV_HALO + CONV_TQ, dc), F32),
                        pltpu.VMEM((SUBLANE - 1, CONV_HALO + CONV_TQ - SUBLANE, dc), F32),
                        pltpu.VMEM((CONV_TQ, dc), F32)],
        compiler_params=_cparams("parallel", "parallel"),
        name="conv_branch",
    )(cv3, cv3, w, cb, g, b)


def _outproj_kernel(x_ref, gate_ref, g_ref, b_ref, nsa_ref, ocv_ref, w_ref, o_ref, *, alpha):
    half = nsa_ref.shape[1]
    y = _dot(nsa_ref[...], w_ref[0:half, :]) + _dot(ocv_ref[...], w_ref[half:, :])
    o_ref[...] = _res_ln(x_ref[...], gate_ref[0], y, g_ref[...], b_ref[...], alpha)


def _outproj(x2, gate, lng, lnb, o_nsa, o_conv, w, seq, tm, alpha):
    n, d = x2.shape
    per_b = seq // tm
    row = lambda i: (i, 0)
    const = lambda i: (0, 0)
    return pl.pallas_call(
        functools.partial(_outproj_kernel, alpha=alpha),
        out_shape=jax.ShapeDtypeStruct((n, d), F32),
        grid=(n // tm,),
        in_specs=[pl.BlockSpec((tm, d), row), pl.BlockSpec((1, 1, d), lambda i: (i // per_b, 0, 0)),
                  pl.BlockSpec((1, d), const), pl.BlockSpec((1, d), const),
                  pl.BlockSpec((tm, o_nsa.shape[1]), row), pl.BlockSpec((tm, o_conv.shape[1]), row),
                  pl.BlockSpec(w.shape, const)],
        out_specs=pl.BlockSpec((tm, d), row),
        compiler_params=_cparams("parallel"),
        name="out_proj_ln",
    )(x2, gate, lng, lnb, o_nsa, o_conv, w)


def _ffn1_kernel(x_ref, sc_ref, sh_ref, wg_ref, wu_ref, o_ref, *, tn):
    h = (x_ref[...] * (1.0 + sc_ref[0]) + sh_ref[0]).astype(BF16)
    for j in range(o_ref.shape[1] // tn):
        cols = slice(j * tn, (j + 1) * tn)
        gate = _dot(h, wg_ref[:, cols])
        up = _dot(h, wu_ref[:, cols])
        o_ref[:, cols] = (jax.nn.silu(gate) * up).astype(BF16)


def _ffn1(x2, scale, shift, wg, wu, seq, tm):
    n, d = x2.shape
    dff = wg.shape[1]
    per_b = seq // tm
    row = lambda i: (i, 0)
    mod = lambda i: (i // per_b, 0, 0)
    const = lambda i: (0, 0)
    return pl.pallas_call(
        functools.partial(_ffn1_kernel, tn=256),
        out_shape=jax.ShapeDtypeStruct((n, dff), BF16),
        grid=(n // tm,),
        in_specs=[pl.BlockSpec((tm, d), row), pl.BlockSpec((1, 1, d), mod), pl.BlockSpec((1, 1, d), mod),
                  pl.BlockSpec((d, dff), const), pl.BlockSpec((d, dff), const)],
        out_specs=pl.BlockSpec((tm, dff), row),
        compiler_params=_cparams("parallel"),
        name="ffn_up",
    )(x2, scale, shift, wg, wu)


def _ffn2_kernel(x_ref, gate_ref, g_ref, b_ref, a_ref, w_ref, o_ref, *, alpha):
    y = _dot(a_ref[...], w_ref[...])
    o_ref[...] = _res_ln(x_ref[...], gate_ref[0], y, g_ref[...], b_ref[...], alpha)


def _ffn2(x2, gate, lng, lnb, act, w, seq, tm, alpha):
    n, d = x2.shape
    per_b = seq // tm
    row = lambda i: (i, 0)
    const = lambda i: (0, 0)
    return pl.pallas_call(
        functools.partial(_ffn2_kernel, alpha=alpha),
        out_shape=jax.ShapeDtypeStruct((n, d), F32),
        grid=(n // tm,),
        in_specs=[pl.BlockSpec((tm, d), row), pl.BlockSpec((1, 1, d), lambda i: (i // per_b, 0, 0)),
                  pl.BlockSpec((1, d), const), pl.BlockSpec((1, d), const),
                  pl.BlockSpec((tm, act.shape[1]), row), pl.BlockSpec(w.shape, const)],
        out_specs=pl.BlockSpec((tm, d), row),
        compiler_params=_cparams("parallel"),
        name="ffn_down_ln",
    )(x2, gate, lng, lnb, act, w)


MOE_TM = 1024
MOE_FC = 512
GATHER_ROWS = 1024


def _router_kernel(x_ref, sc_ref, sh_ref, rw_ref, h_ref, ti_ref, tw_ref):
    hf = x_ref[...] * (1.0 + sc_ref[0]) + sh_ref[0]
    h_ref[...] = hf
    lane = lax.broadcasted_iota(jnp.int32, (hf.shape[0], LANE), 1)
    logits = jnp.dot(hf, rw_ref[...], precision=HIGHEST, preferred_element_type=F32)
    logits = jnp.where(lane < N_EXPERTS, logits, -jnp.inf)
    m1 = jnp.max(logits, axis=-1, keepdims=True)
    i1 = jnp.min(jnp.where(logits == m1, lane, LANE), axis=-1, keepdims=True)
    rest = jnp.where(lane == i1, -jnp.inf, logits)
    m2 = jnp.max(rest, axis=-1, keepdims=True)
    i2 = jnp.min(jnp.where(rest == m2, lane, LANE), axis=-1, keepdims=True)
    e2 = jnp.exp(m2 - m1)
    ti_ref[...] = jnp.where(lane == 0, i1, jnp.where(lane == 1, i2, 0))
    tw_ref[...] = jnp.where(lane == 0, 1.0 / (1.0 + e2), jnp.where(lane == 1, e2 / (1.0 + e2), 0.0))


def _router(x2, scale, shift, router_p, seq, tm):
    n, d = x2.shape
    per_b = seq // tm
    row = lambda i: (i, 0)
    mod = lambda i: (i // per_b, 0, 0)
    return pl.pallas_call(
        _router_kernel,
        out_shape=(jax.ShapeDtypeStruct((n, d), F32), jax.ShapeDtypeStruct((n, LANE), jnp.int32),
                   jax.ShapeDtypeStruct((n, LANE), F32)),
        grid=(n // tm,),
        in_specs=[pl.BlockSpec((tm, d), row), pl.BlockSpec((1, 1, d), mod), pl.BlockSpec((1, 1, d), mod),
                  pl.BlockSpec(router_p.shape, lambda i: (0, 0))],
        out_specs=(pl.BlockSpec((tm, d), row), pl.BlockSpec((tm, LANE), row), pl.BlockSpec((tm, LANE), row)),
        compiler_params=_cparams("parallel"),
        name="moe_router",
    )(x2, scale, shift, router_p)


def _scatter_rows_kernel(pos_ref, h_ref, init_ref, o_ref, sem):
    del init_ref
    rows = h_ref.shape[0]
    n = pl.num_programs(0) * rows
    base = pl.program_id(0) * rows

    def issue(r, carry):
        row = h_ref.at[pl.ds(r, 1)]
        pltpu.make_async_copy(row, o_ref.at[pl.ds(pos_ref[base + r], 1)], sem).start()
        pltpu.make_async_copy(row, o_ref.at[pl.ds(pos_ref[n + base + r], 1)], sem).start()
        return carry

    lax.fori_loop(0, rows, issue, 0, unroll=8)
    for _ in range(2):
        pltpu.make_async_copy(h_ref, o_ref.at[pl.ds(0, rows)], sem).wait()


def _scatter_rows(pos, h, init):
    n, d = h.shape
    return pl.pallas_call(
        _scatter_rows_kernel,
        out_shape=jax.ShapeDtypeStruct(init.shape, h.dtype),
        grid_spec=pltpu.PrefetchScalarGridSpec(
            num_scalar_prefetch=1,
            grid=(n // GATHER_ROWS,),
            in_specs=[pl.BlockSpec((GATHER_ROWS, d), lambda i, pos_ref: (i, 0)),
                      pl.BlockSpec(memory_space=pl.ANY)],
            out_specs=pl.BlockSpec(memory_space=pl.ANY),
            scratch_shapes=[pltpu.SemaphoreType.DMA]),
        input_output_aliases={2: 0},
        compiler_params=_cparams("arbitrary"),
        name="row_scatter",
    )(pos, h, init)


def _expert_up_kernel(te_ref, nu_ref, x_ref, wg_ref, wu_ref, o_ref, xb_sc):
    i = pl.program_id(0)
    f = pl.program_id(1)

    @pl.when(i < nu_ref[0])
    def _compute():
        @pl.when(f == 0)
        def _cast():
            xb_sc[...] = x_ref[...].astype(BF16)
        h = xb_sc[...]
        gate = _dot(h, wg_ref[0, 0])
        up = _dot(h, wu_ref[0, 0])
        o_ref[...] = (jax.nn.silu(gate) * up).astype(BF16)

    @pl.when(i >= nu_ref[0])
    def _unused_tile():
        o_ref[...] = jnp.zeros_like(o_ref)


def _expert_up(tile_expert, n_used, xg, w1b, layer):
    p_rows, d = xg.shape
    nf = w1b.shape[3] // (2 * MOE_FC)
    return pl.pallas_call(
        _expert_up_kernel,
        out_shape=jax.ShapeDtypeStruct((p_rows, nf * MOE_FC), BF16),
        grid_spec=pltpu.PrefetchScalarGridSpec(
            num_scalar_prefetch=2,
            grid=(p_rows // MOE_TM, nf),
            in_specs=[pl.BlockSpec((MOE_TM, d), lambda i, f, te, nu: (i, 0)),
                      pl.BlockSpec((1, 1, d, MOE_FC), lambda i, f, te, nu: (layer, te[i], 0, f)),
                      pl.BlockSpec((1, 1, d, MOE_FC), lambda i, f, te, nu: (layer, te[i], 0, nf + f))],
            out_specs=pl.BlockSpec((MOE_TM, MOE_FC), lambda i, f, te, nu: (i, f)),
            scratch_shapes=[pltpu.VMEM((MOE_TM, d), BF16)]),
        compiler_params=_cparams("parallel", "arbitrary"),
        name="moe_up",
    )(tile_expert, n_used, xg, w1b, w1b)


def _expert_down_kernel(te_ref, nu_ref, a_ref, w_ref, o_ref):
    i = pl.program_id(0)

    @pl.when(i < nu_ref[0])
    def _compute():
        o_ref[...] = _dot(a_ref[...], w_ref[0, 0])

    @pl.when(i >= nu_ref[0])
    def _unused_tile():
        o_ref[...] = jnp.zeros_like(o_ref)


def _expert_down(tile_expert, n_used, act, w2b, layer):
    p_rows, dffe = act.shape
    d = w2b.shape[3]
    return pl.pallas_call(
        _expert_down_kernel,
        out_shape=jax.ShapeDtypeStruct((p_rows, d), F32),
        grid_spec=pltpu.PrefetchScalarGridSpec(
            num_scalar_prefetch=2,
            grid=(p_rows // MOE_TM,),
            in_specs=[pl.BlockSpec((MOE_TM, dffe), lambda i, te, nu: (i, 0)),
                      pl.BlockSpec((1, 1, dffe, d), lambda i, te, nu: (layer, te[i], 0, 0))],
            out_specs=pl.BlockSpec((MOE_TM, d), lambda i, te, nu: (i, 0))),
        compiler_params=_cparams("parallel"),
        name="moe_down",
    )(tile_expert, n_used, act, w2b)


def _moe_combine_kernel(pos_ref, x_ref, gate_ref, g_ref, b_ref, tw_ref, y_ref, o_ref, ybuf, sems, *, alpha):
    i = pl.program_id(0)
    nt = pl.num_programs(0)
    tm = x_ref.shape[0]
    n = nt * tm

    def fetch(step, slot):
        def issue(r, carry):
            t = step * tm + r
            pltpu.make_async_copy(y_ref.at[pl.ds(pos_ref[t], 1)], ybuf.at[slot, 0, pl.ds(r, 1)],
                                  sems.at[slot]).start()
            pltpu.make_async_copy(y_ref.at[pl.ds(pos_ref[n + t], 1)], ybuf.at[slot, 1, pl.ds(r, 1)],
                                  sems.at[slot]).start()
            return carry
        lax.fori_loop(0, tm, issue, 0, unroll=8)

    @pl.when(i == 0)
    def _first():
        fetch(0, 0)

    @pl.when(i + 1 < nt)
    def _next():
        fetch(i + 1, (i + 1) % 2)

    slot = i % 2
    for choice in range(2):
        pltpu.make_async_copy(y_ref.at[pl.ds(0, tm)], ybuf.at[slot, choice], sems.at[slot]).wait()
    tw = tw_ref[...]
    y = tw[:, 0:1] * ybuf[slot, 0] + tw[:, 1:2] * ybuf[slot, 1]
    o_ref[...] = _res_ln(x_ref[...], gate_ref[0], y, g_ref[...], b_ref[...], alpha)


def _moe_combine(pos, x2, gate, lng, lnb, tw, y, seq, tm, alpha):
    n, d = x2.shape
    per_b = seq // tm
    row = lambda i, p: (i, 0)
    const = lambda i, p: (0, 0)
    return pl.pallas_call(
        functools.partial(_moe_combine_kernel, alpha=alpha),
        out_shape=jax.ShapeDtypeStruct((n, d), F32),
        grid_spec=pltpu.PrefetchScalarGridSpec(
            num_scalar_prefetch=1,
            grid=(n // tm,),
            in_specs=[pl.BlockSpec((tm, d), row), pl.BlockSpec((1, 1, d), lambda i, p: (i // per_b, 0, 0)),
                      pl.BlockSpec((1, d), const), pl.BlockSpec((1, d), const), pl.BlockSpec((tm, LANE), row),
                      pl.BlockSpec(memory_space=pl.ANY)],
            out_specs=pl.BlockSpec((tm, d), row),
            scratch_shapes=[pltpu.VMEM((2, 2, tm, d), F32), pltpu.SemaphoreType.DMA((2,))]),
        compiler_params=_cparams("arbitrary"),
        name="moe_combine_ln",
    )(pos, x2, gate, lng, lnb, tw, y)


def _route_tables(ti, n_rows):
    e_flat = jnp.concatenate([ti[:, 0], ti[:, 1]])
    onehot = (e_flat[:, None] == jnp.arange(N_EXPERTS, dtype=jnp.int32)[None, :]).astype(jnp.int32)
    cum = jnp.cumsum(onehot, axis=0)
    counts = cum[-1]
    padded = (counts + MOE_TM - 1) // MOE_TM * MOE_TM
    ends = jnp.cumsum(padded)
    pos = jnp.sum(onehot * (cum - 1 + (ends - padded)[None, :]), axis=1)
    tile_start = jnp.arange(n_rows // MOE_TM, dtype=jnp.int32) * MOE_TM
    tile_expert = jnp.minimum(jnp.sum((tile_start[:, None] >= ends[None, :]).astype(jnp.int32), axis=1),
                              N_EXPERTS - 1)
    return pos.astype(jnp.int32), tile_expert, (ends[-1:] // MOE_TM).astype(jnp.int32)


def _moe(x2, scale, shift, gate, lng, lnb, router_p, w1b, w2b, layer, seq, tm, alpha, grouped_buf):
    n, d = x2.shape
    n_rows = 2 * n + N_EXPERTS * MOE_TM
    assert n % GATHER_ROWS == 0
    if grouped_buf is None:
        grouped_buf = jnp.zeros((n_rows, d), F32)
    h, ti, tw = _router(x2, scale, shift, router_p, seq, tm)
    pos, tile_expert, n_used = _route_tables(ti, n_rows)
    xg = _scatter_rows(pos, h, grouped_buf)
    act = _expert_up(tile_expert, n_used, xg, w1b, layer)
    y = _expert_down(tile_expert, n_used, act, w2b, layer)
    return _moe_combine(pos, x2, gate, lng, lnb, tw, y, seq, tm, alpha), xg


def _rel_bucket_np(dist):
    n = np.maximum(dist, 0)
    max_exact = NUM_BUCKETS // 2
    nf = np.maximum(n, 1).astype(np.float32)
    large = max_exact + (np.log(nf / np.float32(max_exact)) / np.float32(math.log(MAX_DISTANCE / max_exact))
                         * np.float32(NUM_BUCKETS - max_exact)).astype(np.int32)
    large = np.minimum(large, NUM_BUCKETS - 1)
    return np.where(n < max_exact, n, large).astype(np.int32)


def _bias_lookup(rel_bias, dist, valid):
    onehot = _rel_bucket_np(dist)[..., None] == np.arange(NUM_BUCKETS)
    vals = jnp.einsum('...k,kh->...h', jnp.asarray(onehot).astype(F32), rel_bias, precision=HIGHEST)
    return jnp.where(jnp.asarray(valid)[..., None], vals * LOG2E, NEG_INF)


def _key_major(vals, add_masked_tile=False):
    n_tiles, tq, tk, _ = vals.shape
    if add_masked_tile:
        vals = jnp.concatenate([vals, jnp.full((1,) + vals.shape[1:], NEG_INF, vals.dtype)], axis=0)
        n_tiles += 1
    vals = vals.reshape(n_tiles, tq, tk, N_KV, Q_PER_KV).transpose(3, 0, 2, 4, 1)
    return vals.reshape(N_KV, n_tiles, tk, Q_PER_KV * tq).astype(F32)


def _band_dist(n_delta):
    i = np.arange(ATT_TQ)[None, :, None]
    k = np.arange(ATT_TQ)[None, None, :]
    return ATT_TQ * np.arange(n_delta)[:, None, None] + i - k


def _cmp_bias_table(rel_bias, n_qt, n_cmp_pad):
    per_tile = ATT_TQ // CMP_STRIDE
    shift = per_tile * (n_qt - 1)
    rel_n = np.arange(n_cmp_pad + shift) - shift
    dist = np.arange(ATT_TQ)[:, None] - (CMP_STRIDE * rel_n[None, :] + L_CMP - 1)
    base = _bias_lookup(rel_bias, dist, dist >= 0)
    tiles = [base[:, shift - per_tile * qt: shift - per_tile * qt + n_cmp_pad] for qt in range(n_qt)]
    return _key_major(jnp.stack(tiles))


def _inproj_weight(w_in):
    depth, d, _ = w_in.shape
    d_nsa = N_HEADS * HEAD_DIM
    d_kv = N_KV * HEAD_DIM
    base_ks = d_nsa + 2 * d_kv
    base_gl = d_nsa + 6 * d_kv
    base_cv = base_gl + N_HEADS * N_BRANCH
    wq = jnp.pad(w_in[:, :, :d_nsa].reshape(depth, d, N_HEADS, HEAD_DIM),
                 ((0, 0), (0, 0), (0, 0), (0, LANE - HEAD_DIM))).reshape(depth, d, Q_COLS)
    parts = [wq, w_in[:, :, d_nsa:base_ks]]
    for pair in range(2):
        kbase = base_ks + pair * 2 * d_kv
        for g in range(N_KV):
            parts.append(w_in[:, :, kbase + g * HEAD_DIM: kbase + (g + 1) * HEAD_DIM])
            parts.append(w_in[:, :, kbase + d_kv + g * HEAD_DIM: kbase + d_kv + (g + 1) * HEAD_DIM])
    parts.append(w_in[:, :, base_cv:])
    gl = w_in[:, :, base_gl:base_cv].reshape(depth, d, N_KV, Q_PER_KV, N_BRANCH).transpose(0, 2, 4, 3, 1)
    gl = jnp.pad(gl.reshape(depth, N_KV, N_BRANCH * Q_PER_KV, d),
                 ((0, 0), (0, 0), (0, LANE - N_BRANCH * Q_PER_KV), (0, 0)))
    return jnp.concatenate(parts, axis=-1).astype(BF16), gl.reshape(depth, GL_ROWS, d).astype(BF16)


def kernel(x, c, w_in, cmp_pos, cmp_w1, cmp_w2, conv_w, conv_b, conv_ln_g, conv_ln_b, w_out, rel_bias,
           ada_w, ada_b, ln_g, ln_b, ffn_w1, ffn_w2, router_w, moe_w1, moe_w2):
    bsz, seq, d = x.shape
    depth = w_in.shape[0]
    n = bsz * seq
    alpha = (2 * depth) ** 0.25
    n_blocks = seq // L_SEL
    n_sel = min(N_SEL, n_blocks)
    n_cmp = (seq - L_CMP) // CMP_STRIDE + 1
    nrow = seq // CMP_STRIDE
    assert d == 1024 and seq % 512 == 0 and nrow == LANE and n_cmp == nrow - 1
    tm = 512

    n_qt = seq // ATT_TQ
    d_win = _band_dist(WINDOW // ATT_TQ + 1)
    win_tab = _key_major(_bias_lookup(rel_bias, d_win, (d_win >= 0) & (d_win < WINDOW)), add_masked_tile=True)
    d_sel = _band_dist(3)
    assert _rel_bucket_np(d_sel[2]).min() == NUM_BUCKETS - 1
    sel_tab = _key_major(_bias_lookup(rel_bias, d_sel, d_sel >= 0), add_masked_tile=True)
    cmp_tab = _cmp_bias_table(rel_bias, n_qt, nrow)

    w_in_p, w_gl_t = _inproj_weight(w_in)
    eye2 = jnp.eye(2, dtype=F32)
    w1r = cmp_w1.reshape(depth, 2, L_CMP, HEAD_DIM, HEAD_DIM).astype(BF16)
    zero_blk = jnp.zeros((depth, L_CMP, HEAD_DIM, HEAD_DIM), BF16)
    row_blocks = []
    for s in range(2):
        for g in range(N_KV):
            col_blocks = [zero_blk] * (2 * N_KV)
            col_blocks[g * 2 + s] = w1r[:, s]
            row_blocks.append(jnp.concatenate(col_blocks, axis=-1))
    w1x = jnp.concatenate(row_blocks, axis=2)
    half = L_CMP // 2
    w_lo = w1x[:, :half]
    w_hi = w1x[:, half:]
    w2bd = jnp.einsum('zsde,st,gh->zgsdhte', cmp_w2, eye2, eye2).reshape(depth, 4 * HEAD_DIM, 4 * HEAD_DIM).astype(BF16)
    posx = jnp.broadcast_to(cmp_pos[:, :, :, None, :], (depth, 2, L_CMP, N_KV, HEAD_DIM)).transpose(0, 2, 1, 3, 4)
    pos_lo = posx[:, :half].reshape(depth, half, 4 * HEAD_DIM)
    pos_hi = posx[:, half:].reshape(depth, half, 4 * HEAD_DIM)
    conv_wp = jnp.broadcast_to(conv_w[:, :, None, :], conv_w.shape[:2] + (SUBLANE, conv_w.shape[2]))
    w_out_b = w_out.astype(BF16)
    dff = ffn_w2.shape[1]
    ffn_wg = ffn_w1[:, :, :dff].astype(BF16)
    ffn_wu = ffn_w1[:, :, dff:].astype(BF16)
    ffn_w2b = ffn_w2.astype(BF16)
    router_p = jnp.pad(router_w, ((0, 0), (0, 0), (0, LANE - N_EXPERTS)))
    moe_w1b = moe_w1.astype(BF16)
    moe_w2b = moe_w2.astype(BF16)

    mod = _ada(c, ada_w, ada_b).reshape(depth, 6, bsz, 1, d)
    x2 = x.reshape(n, d)
    grouped_buf = None
    for l in range(depth):
        shift1, scale1, gate1, shift2, scale2, gate2 = [mod[l, k] for k in range(6)]
        q, kvc, kvsw, gl3, cv = _inproj(x2, scale1, shift1, w_in_p[l], w_gl_t[l], seq, tm)
        q3 = q.reshape(bsz, seq, Q_COLS)
        kvsw3 = kvsw.reshape(bsz, seq, KVSW_COLS)
        kvsw_t = kvsw3.reshape(bsz, n_qt, ATT_TQ, 2 * N_KV, LANE).transpose(0, 3, 1, 4, 2)
        kvcmp, kvcmp_t = _compress(kvc.reshape(bsz, seq, KVC_COLS), pos_lo[l], pos_hi[l], w_lo[l], w_hi[l],
                                   w2bd[l])
        o_nsa = _nsa_attn(q3, kvcmp, kvcmp_t, cmp_tab, kvsw3, kvsw_t, sel_tab, win_tab, gl3, n_blocks, n_sel)
        o_conv = _conv_branch(cv.reshape(bsz, seq, 1024), conv_wp[l], conv_b[l][None], conv_ln_g[l][None],
                              conv_ln_b[l][None])
        x2 = _outproj(x2, gate1, ln_g[l, 0][None], ln_b[l, 0][None], o_nsa.reshape(n, 512), o_conv.reshape(n, 512),
                      w_out_b[l], seq, tm, alpha)
        if l % 2 == 0:
            act = _ffn1(x2, scale2, shift2, ffn_wg[l // 2], ffn_wu[l // 2], seq, tm)
            x2 = _ffn2(x2, gate2, ln_g[l, 1][None], ln_b[l, 1][None], act, ffn_w2b[l // 2], seq, tm, alpha)
        else:
            x2, grouped_buf = _moe(x2, scale2, shift2, gate2, ln_g[l, 1][None], ln_b[l, 1][None], router_p[l // 2],
                                   moe_w1b, moe_w2b, l // 2, seq, tm, alpha, grouped_buf)
    return x2.reshape(bsz, seq, d)
```

```python
import functools
import math

import numpy as np
import jax
import jax.numpy as jnp
from jax import lax
from jax.experimental import pallas as pl
from jax.experimental.pallas import tpu as pltpu

F32 = jnp.float32
BF16 = jnp.bfloat16
HIGHEST = lax.Precision.HIGHEST

HEAD_DIM = 64
N_HEADS = 8
N_KV = 2
Q_PER_KV = N_HEADS // N_KV
N_BRANCH = 3
L_CMP = 32
CMP_STRIDE = 16
L_SEL = 64
N_SEL = 8
WINDOW = 512
CONV_WIDTH = 31
NUM_BUCKETS = 32
MAX_DISTANCE = 128
N_EXPERTS = 8
LN_EPS = 1e-5
NEG_INF = -1e30
LOG2E = math.log2(math.e)

LANE = 128
SUBLANE = 8
ATT_TQ = 128
ATT_ROWS = Q_PER_KV * ATT_TQ
CONV_TQ = 256
CONV_HALO = 32
CONV_RC = 64
VMEM_LIMIT_BYTES = 52 * 1024 * 1024


def _cparams(*sem):
    return pltpu.CompilerParams(dimension_semantics=sem, vmem_limit_bytes=VMEM_LIMIT_BYTES)


def _dot(a, b):
    return jnp.dot(a, b, preferred_element_type=F32)


def _dot_nt(a, b):
    return lax.dot_general(a, b, (((1,), (1,)), ((), ())), preferred_element_type=F32)


def _res_ln(x, gate, y, g, b, alpha):
    z = alpha * x + (1.0 + gate) * y
    mu = jnp.mean(z, axis=-1, keepdims=True)
    zc = z - mu
    var = jnp.mean(zc * zc, axis=-1, keepdims=True)
    return zc * lax.rsqrt(var + LN_EPS) * g + b


def _ada_kernel(c_ref, w_ref, b_ref, o_ref):
    sc = jax.nn.silu(c_ref[...])
    o_ref[0, 0] = jnp.dot(sc, w_ref[0], precision=HIGHEST, preferred_element_type=F32) + b_ref[0]


def _ada(c, ada_w, ada_b):
    depth, d, _ = ada_w.shape
    bsz = c.shape[0]
    return pl.pallas_call(
        _ada_kernel,
        out_shape=jax.ShapeDtypeStruct((depth, 6, bsz, d), F32),
        grid=(depth, 6),
        in_specs=[pl.BlockSpec((bsz, d), lambda l, k: (0, 0)),
                  pl.BlockSpec((1, d, d), lambda l, k: (l, 0, k)),
                  pl.BlockSpec((1, 1, d), lambda l, k: (l, 0, k))],
        out_specs=pl.BlockSpec((1, 1, bsz, d), lambda l, k: (l, k, 0, 0)),
        compiler_params=_cparams("parallel", "parallel"),
        name="ada_mod",
    )(c, ada_w, ada_b.reshape(depth, 1, 6 * d))


Q_COLS = N_HEADS * LANE
KVC_COLS = 4 * HEAD_DIM
KVSW_COLS = 8 * HEAD_DIM
GL_ROWS = N_KV * LANE


def _inproj_kernel(x_ref, sc_ref, sh_ref, w_ref, wgl_ref, q_ref, kvc_ref, kvsw_ref, glt_ref, cv_ref):
    h = (x_ref[...] * (1.0 + sc_ref[0]) + sh_ref[0]).astype(BF16)
    c0, c1, c2 = Q_COLS, Q_COLS + KVC_COLS, Q_COLS + KVC_COLS + KVSW_COLS
    q_ref[...] = (_dot(h, w_ref[:, 0:c0]) * (HEAD_DIM ** -0.5 * LOG2E)).astype(BF16)
    kvc_ref[...] = _dot(h, w_ref[:, c0:c1])
    kvsw_ref[...] = _dot(h, w_ref[:, c1:c2]).astype(BF16)
    cv_ref[...] = _dot(h, w_ref[:, c2:])
    glt_ref[...] = _dot_nt(wgl_ref[...], h)


def _inproj(x2, scale, shift, w, wgl_t, seq, tm):
    n, d = x2.shape
    per_b = seq // tm
    d_cv = w.shape[1] - (Q_COLS + KVC_COLS + KVSW_COLS)
    row = lambda i: (i, 0)
    mod = lambda i: (i // per_b, 0, 0)
    const = lambda i: (0, 0)
    return pl.pallas_call(
        _inproj_kernel,
        out_shape=(jax.ShapeDtypeStruct((n, Q_COLS), BF16), jax.ShapeDtypeStruct((n, KVC_COLS), F32),
                   jax.ShapeDtypeStruct((n, KVSW_COLS), BF16), jax.ShapeDtypeStruct((GL_ROWS, n), F32),
                   jax.ShapeDtypeStruct((n, d_cv), F32)),
        grid=(n // tm,),
        in_specs=[pl.BlockSpec((tm, d), row), pl.BlockSpec((1, 1, d), mod), pl.BlockSpec((1, 1, d), mod),
                  pl.BlockSpec(w.shape, const), pl.BlockSpec(wgl_t.shape, const)],
        out_specs=(pl.BlockSpec((tm, Q_COLS), row), pl.BlockSpec((tm, KVC_COLS), row),
                   pl.BlockSpec((tm, KVSW_COLS), row), pl.BlockSpec((GL_ROWS, tm), lambda i: (0, i)),
                   pl.BlockSpec((tm, d_cv), row)),
        compiler_params=_cparams("parallel"),
        name="in_proj",
    )(x2, scale, shift, w, wgl_t)


def _compress_kernel(ak_ref, av_ref, plo_ref, phi_ref, wlo_ref, whi_ref, w2_ref, o_ref, ot_ref):
    nrow = o_ref.shape[1]
    lo = jnp.zeros((nrow, o_ref.shape[2]), F32)
    hi = jnp.zeros((nrow, o_ref.shape[2]), F32)
    for l in range(CMP_STRIDE):
        for part, a_ref in enumerate((ak_ref, av_ref)):
            lanes = slice(part * LANE, (part + 1) * LANE)
            tok = a_ref[0, pl.ds(l, nrow, stride=CMP_STRIDE), :]
            lo = lo + _dot((tok + plo_ref[l:l + 1, lanes]).astype(BF16), wlo_ref[l, lanes, :])
            hi = hi + _dot((tok + phi_ref[l:l + 1, lanes]).astype(BF16), whi_ref[l, lanes, :])
    pre = lo + pltpu.roll(hi, nrow - 1, 0)
    act = jax.nn.gelu(pre, approximate=True)
    out = _dot(act.astype(BF16), w2_ref[...])
    o_ref[0] = out.astype(BF16)
    ot_ref[0] = out.T.astype(BF16)


def _compress(kvc3, pos_lo, pos_hi, wlo, whi, w2bd):
    bsz, seq, width = kvc3.shape
    nrow = seq // CMP_STRIDE
    const = lambda b: (0, 0)
    const3 = lambda b: (0, 0, 0)
    return pl.pallas_call(
        _compress_kernel,
        out_shape=(jax.ShapeDtypeStruct((bsz, nrow, 256), BF16), jax.ShapeDtypeStruct((bsz, 256, nrow), BF16)),
        grid=(bsz,),
        in_specs=[pl.BlockSpec((1, seq, LANE), lambda b: (b, 0, 0)), pl.BlockSpec((1, seq, LANE), lambda b: (b, 0, 1)),
                  pl.BlockSpec(pos_lo.shape, const), pl.BlockSpec(pos_hi.shape, const),
                  pl.BlockSpec(wlo.shape, const3), pl.BlockSpec(whi.shape, const3),
                  pl.BlockSpec((256, 256), const)],
        out_specs=(pl.BlockSpec((1, nrow, 256), lambda b: (b, 0, 0)),
                   pl.BlockSpec((1, 256, nrow), lambda b: (b, 0, 0))),
        compiler_params=_cparams("parallel"),
        name="compress_mlp",
    )(kvc3, kvc3, pos_lo, pos_hi, wlo, whi, w2bd)


def _stack_heads(q):
    return jnp.concatenate([q[:, r * LANE:(r + 1) * LANE] for r in range(Q_PER_KV)], axis=0)


def _branch_gates(gl_t, branch):
    return jnp.concatenate([jax.nn.sigmoid(gl_t[branch * Q_PER_KV + r:branch * Q_PER_KV + r + 1, :])
                            for r in range(Q_PER_KV)], axis=1)


def _heads_to_rows(o_t):
    tq = o_t.shape[1] // Q_PER_KV
    og = o_t.astype(BF16)
    ii = lax.broadcasted_iota(jnp.int32, (tq, tq), 0)
    kk = lax.broadcasted_iota(jnp.int32, (tq, tq), 1)
    eye = jnp.where(ii == kk, 1.0, 0.0).astype(BF16)
    halves = []
    for pair in range(Q_PER_KV // 2):
        two_heads = jnp.concatenate([og[:, (2 * pair) * tq:(2 * pair + 1) * tq],
                                     og[:, (2 * pair + 1) * tq:(2 * pair + 2) * tq]], axis=0)
        halves.append(_dot_nt(eye, two_heads))
    return jnp.concatenate(halves, axis=1).astype(BF16)


def _cmp_section(qs, qt, kv_ref, kvt_ref, tab_ref, gl_ref, sel_sc, *, n_blocks, n_sel):
    tq = ATT_TQ
    outs = []
    jj = lax.broadcasted_iota(jnp.int32, (n_blocks, LANE), 0)
    nn = lax.broadcasted_iota(jnp.int32, (n_blocks, LANE), 1)
    overlap_t = jnp.where((CMP_STRIDE * nn < L_SEL * jj + L_SEL) & (CMP_STRIDE * nn + L_CMP > L_SEL * jj),
                          1.0, 0.0).astype(F32)
    jb = lax.broadcasted_iota(jnp.int32, (n_blocks, tq), 0)
    tb = (qt * tq + lax.broadcasted_iota(jnp.int32, (n_blocks, tq), 1)) // L_SEL
    for g in range(N_KV):
        bias = tab_ref[g, 0]
        s = _dot_nt(kv_ref[0, :, g * LANE:(g + 1) * LANE], qs[g]) + bias
        valid = bias > 0.5 * NEG_INF
        m = jnp.max(s, axis=0, keepdims=True)
        p = jnp.where(valid, jnp.exp2(s - m), 0.0)
        l = jnp.sum(p, axis=0, keepdims=True)
        p = p * (1.0 / jnp.where(l > 0.0, l, 1.0))
        o_t = _dot(kvt_ref[0, g * LANE + HEAD_DIM:(g + 1) * LANE, :], p.astype(BF16))
        outs.append(o_t * _branch_gates(gl_ref[g * LANE:(g + 1) * LANE, :], 0))

        psum = p[:, 0:tq] + p[:, tq:2 * tq] + p[:, 2 * tq:3 * tq] + p[:, 3 * tq:4 * tq]
        imp_t = jnp.dot(overlap_t, psum, precision=HIGHEST, preferred_element_type=F32)
        score = jnp.where(jb > tb, -jnp.inf,
                          jnp.where((jb == 0) | (jb == tb) | (jb == tb - 1), jnp.inf, imp_t))
        rank = jnp.zeros((n_blocks, tq), F32)
        for jp in range(n_blocks):
            row = score[jp:jp + 1, :]
            earlier = jnp.where(jb > jp, 1.0, 0.0)
            rank = rank + jnp.where(row > score, 1.0, 0.0) + jnp.where(row == score, earlier, 0.0)
        madd = jnp.where(rank < float(n_sel), 0.0, NEG_INF)
        per_chunk = ATT_TQ // L_SEL
        for c in range(n_blocks // per_chunk):
            sel_sc[g, c, 0:per_chunk, :] = madd[c * per_chunk:(c + 1) * per_chunk, :]
    return outs


def _band_section(qs, qt, kv_ref, kvt_ref, tab_ref, gl_ref, sel_sc, s_sc, *, span, group, branch):
    tq = ATT_TQ
    n_delta = tab_ref.shape[1] - 1
    n_chunks = kv_ref.shape[1] // tq
    c_base = qt - span if span is not None else 0
    use_sel = sel_sc is not None

    def chunk_ids(gi, u):
        c = c_base + gi * group + u
        inside = (c >= 0) & (c <= qt)
        dd = jnp.where(inside, jnp.minimum(qt - c, n_delta - 1), n_delta)
        return jnp.clip(c, 0, n_chunks - 1), dd

    def rows(gi, u):
        return pl.ds(pl.multiple_of((gi * group + u) * tq, tq), tq)

    def scores(gi, ms):
        ms = list(ms)
        for u in range(group):
            cc, dd = chunk_ids(gi, u)
            kv = kv_ref[0, pl.ds(pl.multiple_of(cc * tq, tq), tq), :]
            for g in range(N_KV):
                s = _dot_nt(kv[:, g * LANE:(g + 1) * LANE], qs[g]) + tab_ref[g, dd]
                if use_sel:
                    blocks = sel_sc[g, cc, 0:tq // L_SEL, :]
                    madd = jnp.concatenate([jnp.broadcast_to(blocks[j:j + 1, :], (L_SEL, tq))
                                            for j in range(tq // L_SEL)], axis=0)
                    s = s + jnp.concatenate([madd] * Q_PER_KV, axis=1)
                s_sc[g, rows(gi, u), :] = s
                ms[g] = jnp.maximum(ms[g], jnp.max(s, axis=0, keepdims=True))
        return tuple(ms)

    m0 = tuple(jnp.full((1, ATT_ROWS), NEG_INF, F32) for _ in range(N_KV))
    if span is not None:
        n_groups = 1
        m = scores(0, m0)
    else:
        n_groups = qt // group + 1
        m = lax.fori_loop(0, n_groups, scores, m0)

    def values(gi, carry):
        carry = [list(c) for c in carry]
        for u in range(group):
            cc, _ = chunk_ids(gi, u)
            for g in range(N_KV):
                l, acc = carry[g]
                p = jnp.exp2(s_sc[g, rows(gi, u), :] - m[g])
                l = l + jnp.sum(p, axis=0, keepdims=True)
                acc = acc + _dot(kvt_ref[0, g, cc, HEAD_DIM:, :], p.astype(BF16))
                carry[g] = [l, acc]
        return tuple(tuple(c) for c in carry)

    init = tuple((jnp.zeros((1, ATT_ROWS), F32), jnp.zeros((HEAD_DIM, ATT_ROWS), F32)) for _ in range(N_KV))
    if span is not None:
        out = values(0, init)
    else:
        out = lax.fori_loop(0, n_groups, values, init)
    return [out[g][1] * (_branch_gates(gl_ref[g * LANE:(g + 1) * LANE, :], branch) / out[g][0]) for g in range(N_KV)]


SEL_GROUP = 4
WIN_SPAN = WINDOW // ATT_TQ


def _nsa_kernel(q_ref, kvc_ref, kvct_ref, ctab_ref, kvs_ref, kvst_ref, stab_ref, kvw_ref, kvwt_ref, wtab_ref,
                gl_ref, o_ref, sel_sc, ssel_sc, swin_sc, *, n_blocks, n_sel):
    qt = pl.program_id(1)
    hw = Q_PER_KV * LANE
    qs = [_stack_heads(q_ref[0, :, g * hw:(g + 1) * hw]) for g in range(N_KV)]
    o_cmp = _cmp_section(qs, qt, kvc_ref, kvct_ref, ctab_ref, gl_ref, sel_sc, n_blocks=n_blocks, n_sel=n_sel)
    o_sel = _band_section(qs, qt, kvs_ref, kvst_ref, stab_ref, gl_ref, sel_sc, ssel_sc,
                          span=None, group=SEL_GROUP, branch=1)
    o_win = _band_section(qs, qt, kvw_ref, kvwt_ref, wtab_ref, gl_ref, None, swin_sc,
                          span=WIN_SPAN, group=WIN_SPAN + 1, branch=2)
    for g in range(N_KV):
        o_ref[0, :, g * 256:(g + 1) * 256] = _heads_to_rows(o_cmp[g] + o_sel[g] + o_win[g])


def _nsa_attn(q3, kvcmp, kvcmp_t, cmp_tab, kvsw3, kvsw_t, sel_tab, win_tab, gl_t, n_blocks, n_sel):
    bsz, seq, _ = q3.shape
    nqt = seq // ATT_TQ
    n_cmp = kvcmp.shape[1]
    sel_slots = -(-nqt // SEL_GROUP) * SEL_GROUP
    full4 = lambda b, t: (0, 0, 0, 0)
    kv_spec = lambda pair: pl.BlockSpec((1, seq, N_KV * LANE), lambda b, t: (b, 0, pair))
    kvt_spec = lambda pair: pl.BlockSpec((1, N_KV) + kvsw_t.shape[2:], lambda b, t: (b, pair, 0, 0, 0))
    return pl.pallas_call(
        functools.partial(_nsa_kernel, n_blocks=n_blocks, n_sel=n_sel),
        out_shape=jax.ShapeDtypeStruct((bsz, seq, 512), BF16),
        grid=(bsz, nqt),
        in_specs=[pl.BlockSpec((1, ATT_TQ, Q_COLS), lambda b, t: (b, t, 0)),
                  pl.BlockSpec((1, n_cmp, N_KV * LANE), lambda b, t: (b, 0, 0)),
                  pl.BlockSpec((1, N_KV * LANE, n_cmp), lambda b, t: (b, 0, 0)),
                  pl.BlockSpec((N_KV, 1, n_cmp, ATT_ROWS), lambda b, t: (0, t, 0, 0)),
                  kv_spec(0), kvt_spec(0), pl.BlockSpec(sel_tab.shape, full4),
                  kv_spec(1), kvt_spec(1), pl.BlockSpec(win_tab.shape, full4),
                  pl.BlockSpec((GL_ROWS, ATT_TQ), lambda b, t: (0, b * nqt + t))],
        out_specs=pl.BlockSpec((1, ATT_TQ, 512), lambda b, t: (b, t, 0)),
        scratch_shapes=[pltpu.VMEM((N_KV, n_blocks * L_SEL // ATT_TQ, SUBLANE, ATT_TQ), F32),
                        pltpu.VMEM((N_KV, sel_slots * ATT_TQ, ATT_ROWS), F32),
                        pltpu.VMEM((N_KV, (WIN_SPAN + 1) * ATT_TQ, ATT_ROWS), F32)],
        compiler_params=_cparams("parallel", "parallel"),
        name="nsa_attn",
    )(q3, kvcmp, kvcmp_t, cmp_tab, kvsw3, kvsw_t, sel_tab, kvsw3, kvsw_t, win_tab, gl_t)


def _conv_kernel(cur_ref, prev_ref, w_ref, cb_ref, g_ref, b_ref, o_ref, ybuf, ysh, cbuf):
    i = pl.program_id(1)
    dc = o_ref.shape[-1]
    cur = cur_ref[0]
    prev = prev_ref[0]
    ybuf[CONV_HALO:, :] = cur[:, :dc] * jax.nn.sigmoid(cur[:, dc:])
    halo = prev[:, :dc] * jax.nn.sigmoid(prev[:, dc:])
    ybuf[0:CONV_HALO, :] = jnp.where(i > 0, halo, 0.0)
    n_sh = ysh.shape[1]
    for b in range(1, SUBLANE):
        ysh[b - 1] = ybuf[b:b + n_sh, :]
    off = CONV_HALO - (CONV_WIDTH - 1)
    sub = CONV_RC // SUBLANE
    n_rc = CONV_TQ // CONV_RC
    for lc in range(dc // LANE):
        cols = slice(lc * LANE, (lc + 1) * LANE)
        accs = [jnp.zeros((sub, SUBLANE, LANE), F32) for _ in range(n_rc)]
        for k in range(CONV_WIDTH):
            b = (off + k) % SUBLANE
            wk = w_ref[k, :, cols][None]
            for rc in range(n_rc):
                r0 = rc * CONV_RC + (off + k) - b
                src = ybuf[r0:r0 + CONV_RC, cols] if b == 0 else ysh[b - 1, r0:r0 + CONV_RC, cols]
                accs[rc] = accs[rc] + wk * src.reshape(sub, SUBLANE, LANE)
        for rc in range(n_rc):
            cbuf[rc * CONV_RC:(rc + 1) * CONV_RC, cols] = accs[rc].reshape(CONV_RC, LANE)
    z = cbuf[...] + cb_ref[...]
    mu = jnp.mean(z, axis=-1, keepdims=True)
    zc = z - mu
    var = jnp.mean(zc * zc, axis=-1, keepdims=True)
    y = zc * lax.rsqrt(var + LN_EPS) * g_ref[...] + b_ref[...]
    o_ref[0] = jax.nn.silu(y).astype(BF16)


def _conv_branch(cv3, w, cb, g, b):
    bsz, seq, two_dc = cv3.shape
    dc = two_dc // 2
    per = CONV_TQ // CONV_HALO
    const = lambda bb, i: (0, 0)
    return pl.pallas_call(
        _conv_kernel,
        out_shape=jax.ShapeDtypeStruct((bsz, seq, dc), BF16),
        grid=(bsz, seq // CONV_TQ),
        in_specs=[pl.BlockSpec((1, CONV_TQ, two_dc), lambda bb, i: (bb, i, 0)),
                  pl.BlockSpec((1, CONV_HALO, two_dc), lambda bb, i: (bb, jnp.maximum(i * per - 1, 0), 0)),
                  pl.BlockSpec(w.shape, lambda bb, i: (0, 0, 0)), pl.BlockSpec((1, dc), const),
                  pl.BlockSpec((1, dc), const), pl.BlockSpec((1, dc), const)],
        out_specs=pl.BlockSpec((1, CONV_TQ, dc), lambda bb, i: (bb, i, 0)),
        scratch_shapes=[pltpu.VMEM((CONV_HALO + CONV_TQ, dc), F32),
                        pltpu.VMEM((SUBLANE - 1, CONV_HALO + CONV_TQ - SUBLANE, dc), F32),
                        pltpu.VMEM((CONV_TQ, dc), F32)],
        compiler_params=_cparams("parallel", "parallel"),
        name="conv_branch",
    )(cv3, cv3, w, cb, g, b)


def _outproj_kernel(x_ref, gate_ref, g_ref, b_ref, nsa_ref, ocv_ref, w_ref, o_ref, *, alpha):
    half = nsa_ref.shape[1]
    y = _dot(nsa_ref[...], w_ref[0:half, :]) + _dot(ocv_ref[...], w_ref[half:, :])
    o_ref[...] = _res_ln(x_ref[...], gate_ref[0], y, g_ref[...], b_ref[...], alpha)


def _outproj(x2, gate, lng, lnb, o_nsa, o_conv, w, seq, tm, alpha):
    n, d = x2.shape
    per_b = seq // tm
    row = lambda i: (i, 0)
    const = lambda i: (0, 0)
    return pl.pallas_call(
        functools.partial(_outproj_kernel, alpha=alpha),
        out_shape=jax.ShapeDtypeStruct((n, d), F32),
        grid=(n // tm,),
        in_specs=[pl.BlockSpec((tm, d), row), pl.BlockSpec((1, 1, d), lambda i: (i // per_b, 0, 0)),
                  pl.BlockSpec((1, d), const), pl.BlockSpec((1, d), const),
                  pl.BlockSpec((tm, o_nsa.shape[1]), row), pl.BlockSpec((tm, o_conv.shape[1]), row),
                  pl.BlockSpec(w.shape, const)],
        out_specs=pl.BlockSpec((tm, d), row),
        compiler_params=_cparams("parallel"),
        name="out_proj_ln",
    )(x2, gate, lng, lnb, o_nsa, o_conv, w)


def _ffn1_kernel(x_ref, sc_ref, sh_ref, wg_ref, wu_ref, o_ref, *, tn):
    h = (x_ref[...] * (1.0 + sc_ref[0]) + sh_ref[0]).astype(BF16)
    for j in range(o_ref.shape[1] // tn):
        cols = slice(j * tn, (j + 1) * tn)
        gate = _dot(h, wg_ref[:, cols])
        up = _dot(h, wu_ref[:, cols])
        o_ref[:, cols] = (jax.nn.silu(gate) * up).astype(BF16)


def _ffn1(x2, scale, shift, wg, wu, seq, tm):
    n, d = x2.shape
    dff = wg.shape[1]
    per_b = seq // tm
    row = lambda i: (i, 0)
    mod = lambda i: (i // per_b, 0, 0)
    const = lambda i: (0, 0)
    return pl.pallas_call(
        functools.partial(_ffn1_kernel, tn=256),
        out_shape=jax.ShapeDtypeStruct((n, dff), BF16),
        grid=(n // tm,),
        in_specs=[pl.BlockSpec((tm, d), row), pl.BlockSpec((1, 1, d), mod), pl.BlockSpec((1, 1, d), mod),
                  pl.BlockSpec((d, dff), const), pl.BlockSpec((d, dff), const)],
        out_specs=pl.BlockSpec((tm, dff), row),
        compiler_params=_cparams("parallel"),
        name="ffn_up",
    )(x2, scale, shift, wg, wu)


def _ffn2_kernel(x_ref, gate_ref, g_ref, b_ref, a_ref, w_ref, o_ref, *, alpha):
    y = _dot(a_ref[...], w_ref[...])
    o_ref[...] = _res_ln(x_ref[...], gate_ref[0], y, g_ref[...], b_ref[...], alpha)


def _ffn2(x2, gate, lng, lnb, act, w, seq, tm, alpha):
    n, d = x2.shape
    per_b = seq // tm
    row = lambda i: (i, 0)
    const = lambda i: (0, 0)
    return pl.pallas_call(
        functools.partial(_ffn2_kernel, alpha=alpha),
        out_shape=jax.ShapeDtypeStruct((n, d), F32),
        grid=(n // tm,),
        in_specs=[pl.BlockSpec((tm, d), row), pl.BlockSpec((1, 1, d), lambda i: (i // per_b, 0, 0)),
                  pl.BlockSpec((1, d), const), pl.BlockSpec((1, d), const),
                  pl.BlockSpec((tm, act.shape[1]), row), pl.BlockSpec(w.shape, const)],
        out_specs=pl.BlockSpec((tm, d), row),
        compiler_params=_cparams("parallel"),
        name="ffn_down_ln",
    )(x2, gate, lng, lnb, act, w)


MOE_TM = 1024
MOE_FC = 512
GATHER_ROWS = 1024


def _router_kernel(x_ref, sc_ref, sh_ref, rw_ref, h_ref, ti_ref, tw_ref):
    hf = x_ref[...] * (1.0 + sc_ref[0]) + sh_ref[0]
    h_ref[...] = hf
    lane = lax.broadcasted_iota(jnp.int32, (hf.shape[0], LANE), 1)
    logits = jnp.dot(hf, rw_ref[...], precision=HIGHEST, preferred_element_type=F32)
    logits = jnp.where(lane < N_EXPERTS, logits, -jnp.inf)
    m1 = jnp.max(logits, axis=-1, keepdims=True)
    i1 = jnp.min(jnp.where(logits == m1, lane, LANE), axis=-1, keepdims=True)
    rest = jnp.where(lane == i1, -jnp.inf, logits)
    m2 = jnp.max(rest, axis=-1, keepdims=True)
    i2 = jnp.min(jnp.where(rest == m2, lane, LANE), axis=-1, keepdims=True)
    e2 = jnp.exp(m2 - m1)
    ti_ref[...] = jnp.where(lane == 0, i1, jnp.where(lane == 1, i2, 0))
    tw_ref[...] = jnp.where(lane == 0, 1.0 / (1.0 + e2), jnp.where(lane == 1, e2 / (1.0 + e2), 0.0))


def _router(x2, scale, shift, router_p, seq, tm):
    n, d = x2.shape
    per_b = seq // tm
    row = lambda i: (i, 0)
    mod = lambda i: (i // per_b, 0, 0)
    return pl.pallas_call(
        _router_kernel,
        out_shape=(jax.ShapeDtypeStruct((n, d), F32), jax.ShapeDtypeStruct((n, LANE), jnp.int32),
                   jax.ShapeDtypeStruct((n, LANE), F32)),
        grid=(n // tm,),
        in_specs=[pl.BlockSpec((tm, d), row), pl.BlockSpec((1, 1, d), mod), pl.BlockSpec((1, 1, d), mod),
                  pl.BlockSpec(router_p.shape, lambda i: (0, 0))],
        out_specs=(pl.BlockSpec((tm, d), row), pl.BlockSpec((tm, LANE), row), pl.BlockSpec((tm, LANE), row)),
        compiler_params=_cparams("parallel"),
        name="moe_router",
    )(x2, scale, shift, router_p)


def _scatter_rows_kernel(pos_ref, h_ref, init_ref, o_ref, sem):
    del init_ref
    rows = h_ref.shape[0]
    n = pl.num_programs(0) * rows
    base = pl.program_id(0) * rows

    def issue(r, carry):
        row = h_ref.at[pl.ds(r, 1)]
        pltpu.make_async_copy(row, o_ref.at[pl.ds(pos_ref[base + r], 1)], sem).start()
        pltpu.make_async_copy(row, o_ref.at[pl.ds(pos_ref[n + base + r], 1)], sem).start()
        return carry

    lax.fori_loop(0, rows, issue, 0, unroll=8)
    for _ in range(2):
        pltpu.make_async_copy(h_ref, o_ref.at[pl.ds(0, rows)], sem).wait()


def _scatter_rows(pos, h, init):
    n, d = h.shape
    return pl.pallas_call(
        _scatter_rows_kernel,
        out_shape=jax.ShapeDtypeStruct(init.shape, h.dtype),
        grid_spec=pltpu.PrefetchScalarGridSpec(
            num_scalar_prefetch=1,
            grid=(n // GATHER_ROWS,),
            in_specs=[pl.BlockSpec((GATHER_ROWS, d), lambda i, pos_ref: (i, 0)),
                      pl.BlockSpec(memory_space=pl.ANY)],
            out_specs=pl.BlockSpec(memory_space=pl.ANY),
            scratch_shapes=[pltpu.SemaphoreType.DMA]),
        input_output_aliases={2: 0},
        compiler_params=_cparams("arbitrary"),
        name="row_scatter",
    )(pos, h, init)


def _expert_up_kernel(te_ref, nu_ref, x_ref, wg_ref, wu_ref, o_ref, xb_sc):
    i = pl.program_id(0)
    f = pl.program_id(1)

    @pl.when(i < nu_ref[0])
    def _compute():
        @pl.when(f == 0)
        def _cast():
            xb_sc[...] = x_ref[...].astype(BF16)
        h = xb_sc[...]
        gate = _dot(h, wg_ref[0, 0])
        up = _dot(h, wu_ref[0, 0])
        o_ref[...] = (jax.nn.silu(gate) * up).astype(BF16)

    @pl.when(i >= nu_ref[0])
    def _unused_tile():
        o_ref[...] = jnp.zeros_like(o_ref)


def _expert_up(tile_expert, n_used, xg, w1b, layer):
    p_rows, d = xg.shape
    nf = w1b.shape[3] // (2 * MOE_FC)
    return pl.pallas_call(
        _expert_up_kernel,
        out_shape=jax.ShapeDtypeStruct((p_rows, nf * MOE_FC), BF16),
        grid_spec=pltpu.PrefetchScalarGridSpec(
            num_scalar_prefetch=2,
            grid=(p_rows // MOE_TM, nf),
            in_specs=[pl.BlockSpec((MOE_TM, d), lambda i, f, te, nu: (i, 0)),
                      pl.BlockSpec((1, 1, d, MOE_FC), lambda i, f, te, nu: (layer, te[i], 0, f)),
                      pl.BlockSpec((1, 1, d, MOE_FC), lambda i, f, te, nu: (layer, te[i], 0, nf + f))],
            out_specs=pl.BlockSpec((MOE_TM, MOE_FC), lambda i, f, te, nu: (i, f)),
            scratch_shapes=[pltpu.VMEM((MOE_TM, d), BF16)]),
        compiler_params=_cparams("parallel", "arbitrary"),
        name="moe_up",
    )(tile_expert, n_used, xg, w1b, w1b)


def _expert_down_kernel(te_ref, nu_ref, a_ref, w_ref, o_ref):
    i = pl.program_id(0)

    @pl.when(i < nu_ref[0])
    def _compute():
        o_ref[...] = _dot(a_ref[...], w_ref[0, 0])

    @pl.when(i >= nu_ref[0])
    def _unused_tile():
        o_ref[...] = jnp.zeros_like(o_ref)


def _expert_down(tile_expert, n_used, act, w2b, layer):
    p_rows, dffe = act.shape
    d = w2b.shape[3]
    return pl.pallas_call(
        _expert_down_kernel,
        out_shape=jax.ShapeDtypeStruct((p_rows, d), F32),
        grid_spec=pltpu.PrefetchScalarGridSpec(
            num_scalar_prefetch=2,
            grid=(p_rows // MOE_TM,),
            in_specs=[pl.BlockSpec((MOE_TM, dffe), lambda i, te, nu: (i, 0)),
                      pl.BlockSpec((1, 1, dffe, d), lambda i, te, nu: (layer, te[i], 0, 0))],
            out_specs=pl.BlockSpec((MOE_TM, d), lambda i, te, nu: (i, 0))),
        compiler_params=_cparams("parallel"),
        name="moe_down",
    )(tile_expert, n_used, act, w2b)


def _moe_combine_kernel(pos_ref, x_ref, gate_ref, g_ref, b_ref, tw_ref, y_ref, o_ref, ybuf, sems, *, alpha):
    i = pl.program_id(0)
    nt = pl.num_programs(0)
    tm = x_ref.shape[0]
    n = nt * tm

    def fetch(step, slot):
        def issue(r, carry):
            t = step * tm + r
            pltpu.make_async_copy(y_ref.at[pl.ds(pos_ref[t], 1)], ybuf.at[slot, 0, pl.ds(r, 1)],
                                  sems.at[slot]).start()
            pltpu.make_async_copy(y_ref.at[pl.ds(pos_ref[n + t], 1)], ybuf.at[slot, 1, pl.ds(r, 1)],
                                  sems.at[slot]).start()
            return carry
        lax.fori_loop(0, tm, issue, 0, unroll=8)

    @pl.when(i == 0)
    def _first():
        fetch(0, 0)

    @pl.when(i + 1 < nt)
    def _next():
        fetch(i + 1, (i + 1) % 2)

    slot = i % 2
    for choice in range(2):
        pltpu.make_async_copy(y_ref.at[pl.ds(0, tm)], ybuf.at[slot, choice], sems.at[slot]).wait()
    tw = tw_ref[...]
    y = tw[:, 0:1] * ybuf[slot, 0] + tw[:, 1:2] * ybuf[slot, 1]
    o_ref[...] = _res_ln(x_ref[...], gate_ref[0], y, g_ref[...], b_ref[...], alpha)


def _moe_combine(pos, x2, gate, lng, lnb, tw, y, seq, tm, alpha):
    n, d = x2.shape
    per_b = seq // tm
    row = lambda i, p: (i, 0)
    const = lambda i, p: (0, 0)
    return pl.pallas_call(
        functools.partial(_moe_combine_kernel, alpha=alpha),
        out_shape=jax.ShapeDtypeStruct((n, d), F32),
        grid_spec=pltpu.PrefetchScalarGridSpec(
            num_scalar_prefetch=1,
            grid=(n // tm,),
            in_specs=[pl.BlockSpec((tm, d), row), pl.BlockSpec((1, 1, d), lambda i, p: (i // per_b, 0, 0)),
                      pl.BlockSpec((1, d), const), pl.BlockSpec((1, d), const), pl.BlockSpec((tm, LANE), row),
                      pl.BlockSpec(memory_space=pl.ANY)],
            out_specs=pl.BlockSpec((tm, d), row),
            scratch_shapes=[pltpu.VMEM((2, 2, tm, d), F32), pltpu.SemaphoreType.DMA((2,))]),
        compiler_params=_cparams("arbitrary"),
        name="moe_combine_ln",
    )(pos, x2, gate, lng, lnb, tw, y)


def _route_tables(ti, n_rows):
    e_flat = jnp.concatenate([ti[:, 0], ti[:, 1]])
    onehot = (e_flat[:, None] == jnp.arange(N_EXPERTS, dtype=jnp.int32)[None, :]).astype(jnp.int32)
    cum = jnp.cumsum(onehot, axis=0)
    counts = cum[-1]
    padded = (counts + MOE_TM - 1) // MOE_TM * MOE_TM
    ends = jnp.cumsum(padded)
    pos = jnp.sum(onehot * (cum - 1 + (ends - padded)[None, :]), axis=1)
    tile_start = jnp.arange(n_rows // MOE_TM, dtype=jnp.int32) * MOE_TM
    tile_expert = jnp.minimum(jnp.sum((tile_start[:, None] >= ends[None, :]).astype(jnp.int32), axis=1),
                              N_EXPERTS - 1)
    return pos.astype(jnp.int32), tile_expert, (ends[-1:] // MOE_TM).astype(jnp.int32)


def _moe(x2, scale, shift, gate, lng, lnb, router_p, w1b, w2b, layer, seq, tm, alpha, grouped_buf):
    n, d = x2.shape
    n_rows = 2 * n + N_EXPERTS * MOE_TM
    assert n % GATHER_ROWS == 0
    if grouped_buf is None:
        grouped_buf = jnp.zeros((n_rows, d), F32)
    h, ti, tw = _router(x2, scale, shift, router_p, seq, tm)
    pos, tile_expert, n_used = _route_tables(ti, n_rows)
    xg = _scatter_rows(pos, h, grouped_buf)
    act = _expert_up(tile_expert, n_used, xg, w1b, layer)
    y = _expert_down(tile_expert, n_used, act, w2b, layer)
    return _moe_combine(pos, x2, gate, lng, lnb, tw, y, seq, tm, alpha), xg


def _rel_bucket_np(dist):
    n = np.maximum(dist, 0)
    max_exact = NUM_BUCKETS // 2
    nf = np.maximum(n, 1).astype(np.float32)
    large = max_exact + (np.log(nf / np.float32(max_exact)) / np.float32(math.log(MAX_DISTANCE / max_exact))
                         * np.float32(NUM_BUCKETS - max_exact)).astype(np.int32)
    large = np.minimum(large, NUM_BUCKETS - 1)
    return np.where(n < max_exact, n, large).astype(np.int32)


def _bias_lookup(rel_bias, dist, valid):
    onehot = _rel_bucket_np(dist)[..., None] == np.arange(NUM_BUCKETS)
    vals = jnp.einsum('...k,kh->...h', jnp.asarray(onehot).astype(F32), rel_bias, precision=HIGHEST)
    return jnp.where(jnp.asarray(valid)[..., None], vals * LOG2E, NEG_INF)


def _key_major(vals, add_masked_tile=False):
    n_tiles, tq, tk, _ = vals.shape
    if add_masked_tile:
        vals = jnp.concatenate([vals, jnp.full((1,) + vals.shape[1:], NEG_INF, vals.dtype)], axis=0)
        n_tiles += 1
    vals = vals.reshape(n_tiles, tq, tk, N_KV, Q_PER_KV).transpose(3, 0, 2, 4, 1)
    return vals.reshape(N_KV, n_tiles, tk, Q_PER_KV * tq).astype(F32)


def _band_dist(n_delta):
    i = np.arange(ATT_TQ)[None, :, None]
    k = np.arange(ATT_TQ)[None, None, :]
    return ATT_TQ * np.arange(n_delta)[:, None, None] + i - k


def _cmp_bias_table(rel_bias, n_qt, n_cmp_pad):
    per_tile = ATT_TQ // CMP_STRIDE
    shift = per_tile * (n_qt - 1)
    rel_n = np.arange(n_cmp_pad + shift) - shift
    dist = np.arange(ATT_TQ)[:, None] - (CMP_STRIDE * rel_n[None, :] + L_CMP - 1)
    base = _bias_lookup(rel_bias, dist, dist >= 0)
    tiles = [base[:, shift - per_tile * qt: shift - per_tile * qt + n_cmp_pad] for qt in range(n_qt)]
    return _key_major(jnp.stack(tiles))


def _inproj_weight(w_in):
    depth, d, _ = w_in.shape
    d_nsa = N_HEADS * HEAD_DIM
    d_kv = N_KV * HEAD_DIM
    base_ks = d_nsa + 2 * d_kv
    base_gl = d_nsa + 6 * d_kv
    base_cv = base_gl + N_HEADS * N_BRANCH
    wq = jnp.pad(w_in[:, :, :d_nsa].reshape(depth, d, N_HEADS, HEAD_DIM),
                 ((0, 0), (0, 0), (0, 0), (0, LANE - HEAD_DIM))).reshape(depth, d, Q_COLS)
    parts = [wq, w_in[:, :, d_nsa:base_ks]]
    for pair in range(2):
        kbase = base_ks + pair * 2 * d_kv
        for g in range(N_KV):
            parts.append(w_in[:, :, kbase + g * HEAD_DIM: kbase + (g + 1) * HEAD_DIM])
            parts.append(w_in[:, :, kbase + d_kv + g * HEAD_DIM: kbase + d_kv + (g + 1) * HEAD_DIM])
    parts.append(w_in[:, :, base_cv:])
    gl = w_in[:, :, base_gl:base_cv].reshape(depth, d, N_KV, Q_PER_KV, N_BRANCH).transpose(0, 2, 4, 3, 1)
    gl = jnp.pad(gl.reshape(depth, N_KV, N_BRANCH * Q_PER_KV, d),
                 ((0, 0), (0, 0), (0, LANE - N_BRANCH * Q_PER_KV), (0, 0)))
    return jnp.concatenate(parts, axis=-1).astype(BF16), gl.reshape(depth, GL_ROWS, d).astype(BF16)


def kernel(x, c, w_in, cmp_pos, cmp_w1, cmp_w2, conv_w, conv_b, conv_ln_g, conv_ln_b, w_out, rel_bias,
           ada_w, ada_b, ln_g, ln_b, ffn_w1, ffn_w2, router_w, moe_w1, moe_w2):
    bsz, seq, d = x.shape
    depth = w_in.shape[0]
    n = bsz * seq
    alpha = (2 * depth) ** 0.25
    n_blocks = seq // L_SEL
    n_sel = min(N_SEL, n_blocks)
    n_cmp = (seq - L_CMP) // CMP_STRIDE + 1
    nrow = seq // CMP_STRIDE
    assert d == 1024 and seq % 512 == 0 and nrow == LANE and n_cmp == nrow - 1
    tm = 1024

    n_qt = seq // ATT_TQ
    d_win = _band_dist(WINDOW // ATT_TQ + 1)
    win_tab = _key_major(_bias_lookup(rel_bias, d_win, (d_win >= 0) & (d_win < WINDOW)), add_masked_tile=True)
    d_sel = _band_dist(3)
    assert _rel_bucket_np(d_sel[2]).min() == NUM_BUCKETS - 1
    sel_tab = _key_major(_bias_lookup(rel_bias, d_sel, d_sel >= 0), add_masked_tile=True)
    cmp_tab = _cmp_bias_table(rel_bias, n_qt, nrow)

    w_in_p, w_gl_t = _inproj_weight(w_in)
    eye2 = jnp.eye(2, dtype=F32)
    w1r = cmp_w1.reshape(depth, 2, L_CMP, HEAD_DIM, HEAD_DIM).astype(BF16)
    zero_blk = jnp.zeros((depth, L_CMP, HEAD_DIM, HEAD_DIM), BF16)
    row_blocks = []
    for s in range(2):
        for g in range(N_KV):
            col_blocks = [zero_blk] * (2 * N_KV)
            col_blocks[g * 2 + s] = w1r[:, s]
            row_blocks.append(jnp.concatenate(col_blocks, axis=-1))
    w1x = jnp.concatenate(row_blocks, axis=2)
    half = L_CMP // 2
    w_lo = w1x[:, :half]
    w_hi = w1x[:, half:]
    w2bd = jnp.einsum('zsde,st,gh->zgsdhte', cmp_w2, eye2, eye2).reshape(depth, 4 * HEAD_DIM, 4 * HEAD_DIM).astype(BF16)
    posx = jnp.broadcast_to(cmp_pos[:, :, :, None, :], (depth, 2, L_CMP, N_KV, HEAD_DIM)).transpose(0, 2, 1, 3, 4)
    pos_lo = posx[:, :half].reshape(depth, half, 4 * HEAD_DIM)
    pos_hi = posx[:, half:].reshape(depth, half, 4 * HEAD_DIM)
    conv_wp = jnp.broadcast_to(conv_w[:, :, None, :], conv_w.shape[:2] + (SUBLANE, conv_w.shape[2]))
    w_out_b = w_out.astype(BF16)
    dff = ffn_w2.shape[1]
    ffn_wg = ffn_w1[:, :, :dff].astype(BF16)
    ffn_wu = ffn_w1[:, :, dff:].astype(BF16)
    ffn_w2b = ffn_w2.astype(BF16)
    router_p = jnp.pad(router_w, ((0, 0), (0, 0), (0, LANE - N_EXPERTS)))
    moe_w1b = moe_w1.astype(BF16)
    moe_w2b = moe_w2.astype(BF16)

    mod = _ada(c, ada_w, ada_b).reshape(depth, 6, bsz, 1, d)
    x2 = x.reshape(n, d)
    grouped_buf = None
    for l in range(depth):
        shift1, scale1, gate1, shift2, scale2, gate2 = [mod[l, k] for k in range(6)]
        q, kvc, kvsw, gl3, cv = _inproj(x2, scale1, shift1, w_in_p[l], w_gl_t[l], seq, tm)
        q3 = q.reshape(bsz, seq, Q_COLS)
        kvsw3 = kvsw.reshape(bsz, seq, KVSW_COLS)
        kvsw_t = kvsw3.reshape(bsz, n_qt, ATT_TQ, 2 * N_KV, LANE).transpose(0, 3, 1, 4, 2)
        kvcmp, kvcmp_t = _compress(kvc.reshape(bsz, seq, KVC_COLS), pos_lo[l], pos_hi[l], w_lo[l], w_hi[l],
                                   w2bd[l])
        o_nsa = _nsa_attn(q3, kvcmp, kvcmp_t, cmp_tab, kvsw3, kvsw_t, sel_tab, win_tab, gl3, n_blocks, n_sel)
        o_conv = _conv_branch(cv.reshape(bsz, seq, 1024), conv_wp[l], conv_b[l][None], conv_ln_g[l][None],
                              conv_ln_b[l][None])
        x2 = _outproj(x2, gate1, ln_g[l, 0][None], ln_b[l, 0][None], o_nsa.reshape(n, 512), o_conv.reshape(n, 512),
                      w_out_b[l], seq, tm, alpha)
        if l % 2 == 0:
            act = _ffn1(x2, scale2, shift2, ffn_wg[l // 2], ffn_wu[l // 2], seq, tm)
            x2 = _ffn2(x2, gate2, ln_g[l, 1][None], ln_b[l, 1][None], act, ffn_w2b[l // 2], seq, tm, alpha)
        else:
            x2, grouped_buf = _moe(x2, scale2, shift2, gate2, ln_g[l, 1][None], ln_b[l, 1][None], router_p[l // 2],
                                   moe_w1b, moe_w2b, l // 2, seq, tm, alpha, grouped_buf)
    return x2.reshape(bsz, seq, d)
```

```python
import functools
import math

import numpy as np
import jax
import jax.numpy as jnp
from jax import lax
from jax.experimental import pallas as pl
from jax.experimental.pallas import tpu as pltpu

F32 = jnp.float32
BF16 = jnp.bfloat16
HIGHEST = lax.Precision.HIGHEST

HEAD_DIM = 64
N_HEADS = 8
N_KV = 2
Q_PER_KV = N_HEADS // N_KV
N_BRANCH = 3
L_CMP = 32
CMP_STRIDE = 16
L_SEL = 64
N_SEL = 8
WINDOW = 512
CONV_WIDTH = 31
NUM_BUCKETS = 32
MAX_DISTANCE = 128
N_EXPERTS = 8
LN_EPS = 1e-5
NEG_INF = -1e30
LOG2E = math.log2(math.e)

LANE = 128
SUBLANE = 8
ATT_TQ = 128
ATT_ROWS = Q_PER_KV * ATT_TQ
CONV_TQ = 256
CONV_HALO = 32
CONV_RC = 64
VMEM_LIMIT_BYTES = 52 * 1024 * 1024


def _cparams(*sem):
    return pltpu.CompilerParams(dimension_semantics=sem, vmem_limit_bytes=VMEM_LIMIT_BYTES)


def _dot(a, b):
    return jnp.dot(a, b, preferred_element_type=F32)


def _dot_nt(a, b):
    return lax.dot_general(a, b, (((1,), (1,)), ((), ())), preferred_element_type=F32)


def _res_ln(x, gate, y, g, b, alpha):
    z = alpha * x + (1.0 + gate) * y
    mu = jnp.mean(z, axis=-1, keepdims=True)
    zc = z - mu
    var = jnp.mean(zc * zc, axis=-1, keepdims=True)
    return zc * lax.rsqrt(var + LN_EPS) * g + b


def _ada_kernel(c_ref, w_ref, b_ref, o_ref):
    sc = jax.nn.silu(c_ref[...])
    o_ref[0, 0] = jnp.dot(sc, w_ref[0], precision=HIGHEST, preferred_element_type=F32) + b_ref[0]


def _ada(c, ada_w, ada_b):
    depth, d, _ = ada_w.shape
    bsz = c.shape[0]
    return pl.pallas_call(
        _ada_kernel,
        out_shape=jax.ShapeDtypeStruct((depth, 6, bsz, d), F32),
        grid=(depth, 6),
        in_specs=[pl.BlockSpec((bsz, d), lambda l, k: (0, 0)),
                  pl.BlockSpec((1, d, d), lambda l, k: (l, 0, k)),
                  pl.BlockSpec((1, 1, d), lambda l, k: (l, 0, k))],
        out_specs=pl.BlockSpec((1, 1, bsz, d), lambda l, k: (l, k, 0, 0)),
        compiler_params=_cparams("parallel", "parallel"),
        name="ada_mod",
    )(c, ada_w, ada_b.reshape(depth, 1, 6 * d))


Q_COLS = N_HEADS * LANE
KVC_COLS = 4 * HEAD_DIM
KVSW_COLS = 8 * HEAD_DIM
GL_ROWS = N_KV * LANE


def _inproj_kernel(x_ref, sc_ref, sh_ref, w_ref, wgl_ref, q_ref, kvc_ref, kvsw_ref, glt_ref, cv_ref):
    h = (x_ref[...] * (1.0 + sc_ref[0]) + sh_ref[0]).astype(BF16)
    c0, c1, c2 = Q_COLS, Q_COLS + KVC_COLS, Q_COLS + KVC_COLS + KVSW_COLS
    q_ref[...] = (_dot(h, w_ref[:, 0:c0]) * (HEAD_DIM ** -0.5 * LOG2E)).astype(BF16)
    kvc_ref[...] = _dot(h, w_ref[:, c0:c1])
    kvsw_ref[...] = _dot(h, w_ref[:, c1:c2]).astype(BF16)
    cv_ref[...] = _dot(h, w_ref[:, c2:])
    glt_ref[...] = _dot_nt(wgl_ref[...], h)


def _inproj(x2, scale, shift, w, wgl_t, seq, tm):
    n, d = x2.shape
    per_b = seq // tm
    d_cv = w.shape[1] - (Q_COLS + KVC_COLS + KVSW_COLS)
    row = lambda i: (i, 0)
    mod = lambda i: (i // per_b, 0, 0)
    const = lambda i: (0, 0)
    return pl.pallas_call(
        _inproj_kernel,
        out_shape=(jax.ShapeDtypeStruct((n, Q_COLS), BF16), jax.ShapeDtypeStruct((n, KVC_COLS), F32),
                   jax.ShapeDtypeStruct((n, KVSW_COLS), BF16), jax.ShapeDtypeStruct((GL_ROWS, n), F32),
                   jax.ShapeDtypeStruct((n, d_cv), F32)),
        grid=(n // tm,),
        in_specs=[pl.BlockSpec((tm, d), row), pl.BlockSpec((1, 1, d), mod), pl.BlockSpec((1, 1, d), mod),
                  pl.BlockSpec(w.shape, const), pl.BlockSpec(wgl_t.shape, const)],
        out_specs=(pl.BlockSpec((tm, Q_COLS), row), pl.BlockSpec((tm, KVC_COLS), row),
                   pl.BlockSpec((tm, KVSW_COLS), row), pl.BlockSpec((GL_ROWS, tm), lambda i: (0, i)),
                   pl.BlockSpec((tm, d_cv), row)),
        compiler_params=_cparams("parallel"),
        name="in_proj",
    )(x2, scale, shift, w, wgl_t)


def _compress_kernel(ak_ref, av_ref, plo_ref, phi_ref, wlo_ref, whi_ref, w2_ref, o_ref, ot_ref):
    nrow = o_ref.shape[1]
    lo = jnp.zeros((nrow, o_ref.shape[2]), F32)
    hi = jnp.zeros((nrow, o_ref.shape[2]), F32)
    for l in range(CMP_STRIDE):
        for part, a_ref in enumerate((ak_ref, av_ref)):
            lanes = slice(part * LANE, (part + 1) * LANE)
            tok = a_ref[0, pl.ds(l, nrow, stride=CMP_STRIDE), :]
            lo = lo + _dot((tok + plo_ref[l:l + 1, lanes]).astype(BF16), wlo_ref[l, lanes, :])
            hi = hi + _dot((tok + phi_ref[l:l + 1, lanes]).astype(BF16), whi_ref[l, lanes, :])
    pre = lo + pltpu.roll(hi, nrow - 1, 0)
    act = jax.nn.gelu(pre, approximate=True)
    out = _dot(act.astype(BF16), w2_ref[...])
    o_ref[0] = out.astype(BF16)
    ot_ref[0] = out.T.astype(BF16)


def _compress(kvc3, pos_lo, pos_hi, wlo, whi, w2bd):
    bsz, seq, width = kvc3.shape
    nrow = seq // CMP_STRIDE
    const = lambda b: (0, 0)
    const3 = lambda b: (0, 0, 0)
    return pl.pallas_call(
        _compress_kernel,
        out_shape=(jax.ShapeDtypeStruct((bsz, nrow, 256), BF16), jax.ShapeDtypeStruct((bsz, 256, nrow), BF16)),
        grid=(bsz,),
        in_specs=[pl.BlockSpec((1, seq, LANE), lambda b: (b, 0, 0)), pl.BlockSpec((1, seq, LANE), lambda b: (b, 0, 1)),
                  pl.BlockSpec(pos_lo.shape, const), pl.BlockSpec(pos_hi.shape, const),
                  pl.BlockSpec(wlo.shape, const3), pl.BlockSpec(whi.shape, const3),
                  pl.BlockSpec((256, 256), const)],
        out_specs=(pl.BlockSpec((1, nrow, 256), lambda b: (b, 0, 0)),
                   pl.BlockSpec((1, 256, nrow), lambda b: (b, 0, 0))),
        compiler_params=_cparams("parallel"),
        name="compress_mlp",
    )(kvc3, kvc3, pos_lo, pos_hi, wlo, whi, w2bd)


def _stack_heads(q):
    return jnp.concatenate([q[:, r * LANE:(r + 1) * LANE] for r in range(Q_PER_KV)], axis=0)


def _branch_gates(gl_t, branch):
    return jnp.concatenate([jax.nn.sigmoid(gl_t[branch * Q_PER_KV + r:branch * Q_PER_KV + r + 1, :])
                            for r in range(Q_PER_KV)], axis=1)


def _heads_to_rows(o_t):
    tq = o_t.shape[1] // Q_PER_KV
    og = o_t.astype(BF16)
    ii = lax.broadcasted_iota(jnp.int32, (tq, tq), 0)
    kk = lax.broadcasted_iota(jnp.int32, (tq, tq), 1)
    eye = jnp.where(ii == kk, 1.0, 0.0).astype(BF16)
    halves = []
    for pair in range(Q_PER_KV // 2):
        two_heads = jnp.concatenate([og[:, (2 * pair) * tq:(2 * pair + 1) * tq],
                                     og[:, (2 * pair + 1) * tq:(2 * pair + 2) * tq]], axis=0)
        halves.append(_dot_nt(eye, two_heads))
    return jnp.concatenate(halves, axis=1).astype(BF16)


def _cmp_section(qs, qt, kv_ref, kvt_ref, tab_ref, gl_ref, sel_sc, *, n_blocks, n_sel):
    tq = ATT_TQ
    outs = []
    jj = lax.broadcasted_iota(jnp.int32, (n_blocks, LANE), 0)
    nn = lax.broadcasted_iota(jnp.int32, (n_blocks, LANE), 1)
    overlap_t = jnp.where((CMP_STRIDE * nn < L_SEL * jj + L_SEL) & (CMP_STRIDE * nn + L_CMP > L_SEL * jj),
                          1.0, 0.0).astype(F32)
    jb = lax.broadcasted_iota(jnp.int32, (n_blocks, tq), 0)
    tb = (qt * tq + lax.broadcasted_iota(jnp.int32, (n_blocks, tq), 1)) // L_SEL
    for g in range(N_KV):
        bias = tab_ref[g, 0]
        s = _dot_nt(kv_ref[0, :, g * LANE:(g + 1) * LANE], qs[g]) + bias
        valid = bias > 0.5 * NEG_INF
        m = jnp.max(s, axis=0, keepdims=True)
        p = jnp.where(valid, jnp.exp2(s - m), 0.0)
        l = jnp.sum(p, axis=0, keepdims=True)
        p = p * (1.0 / jnp.where(l > 0.0, l, 1.0))
        o_t = _dot(kvt_ref[0, g * LANE + HEAD_DIM:(g + 1) * LANE, :], p.astype(BF16))
        outs.append(o_t * _branch_gates(gl_ref[g * LANE:(g + 1) * LANE, :], 0))

        psum = p[:, 0:tq] + p[:, tq:2 * tq] + p[:, 2 * tq:3 * tq] + p[:, 3 * tq:4 * tq]
        imp_t = jnp.dot(overlap_t, psum, precision=HIGHEST, preferred_element_type=F32)
        score = jnp.where(jb > tb, -jnp.inf,
                          jnp.where((jb == 0) | (jb == tb) | (jb == tb - 1), jnp.inf, imp_t))
        rank = jnp.zeros((n_blocks, tq), F32)
        for jp in range(n_blocks):
            row = score[jp:jp + 1, :]
            earlier = jnp.where(jb > jp, 1.0, 0.0)
            rank = rank + jnp.where(row > score, 1.0, 0.0) + jnp.where(row == score, earlier, 0.0)
        madd = jnp.where(rank < float(n_sel), 0.0, NEG_INF)
        per_chunk = ATT_TQ // L_SEL
        for c in range(n_blocks // per_chunk):
            sel_sc[g, c, 0:per_chunk, :] = madd[c * per_chunk:(c + 1) * per_chunk, :]
    return outs


def _band_section(qs, qt, kv_ref, kvt_ref, tab_ref, gl_ref, sel_sc, s_sc, *, span, group, branch):
    tq = ATT_TQ
    n_delta = tab_ref.shape[1] - 1
    n_chunks = kv_ref.shape[1] // tq
    c_base = qt - span if span is not None else 0
    use_sel = sel_sc is not None

    def chunk_ids(gi, u):
        c = c_base + gi * group + u
        inside = (c >= 0) & (c <= qt)
        dd = jnp.where(inside, jnp.minimum(qt - c, n_delta - 1), n_delta)
        return jnp.clip(c, 0, n_chunks - 1), dd

    def rows(gi, u):
        return pl.ds(pl.multiple_of((gi * group + u) * tq, tq), tq)

    def scores(gi, ms):
        ms = list(ms)
        for u in range(group):
            cc, dd = chunk_ids(gi, u)
            kv = kv_ref[0, pl.ds(pl.multiple_of(cc * tq, tq), tq), :]
            for g in range(N_KV):
                s = _dot_nt(kv[:, g * LANE:(g + 1) * LANE], qs[g]) + tab_ref[g, dd]
                if use_sel:
                    blocks = sel_sc[g, cc, 0:tq // L_SEL, :]
                    madd = jnp.concatenate([jnp.broadcast_to(blocks[j:j + 1, :], (L_SEL, tq))
                                            for j in range(tq // L_SEL)], axis=0)
                    s = s + jnp.concatenate([madd] * Q_PER_KV, axis=1)
                s_sc[g, rows(gi, u), :] = s
                ms[g] = jnp.maximum(ms[g], jnp.max(s, axis=0, keepdims=True))
        return tuple(ms)

    m0 = tuple(jnp.full((1, ATT_ROWS), NEG_INF, F32) for _ in range(N_KV))
    if span is not None:
        n_groups = 1
        m = scores(0, m0)
    else:
        n_groups = qt // group + 1
        m = lax.fori_loop(0, n_groups, scores, m0)

    def values(gi, carry):
        carry = [list(c) for c in carry]
        for u in range(group):
            cc, _ = chunk_ids(gi, u)
            for g in range(N_KV):
                l, acc = carry[g]
                p = jnp.exp2(s_sc[g, rows(gi, u), :] - m[g])
                l = l + jnp.sum(p, axis=0, keepdims=True)
                acc = acc + _dot(kvt_ref[0, g, cc, HEAD_DIM:, :], p.astype(BF16))
                carry[g] = [l, acc]
        return tuple(tuple(c) for c in carry)

    init = tuple((jnp.zeros((1, ATT_ROWS), F32), jnp.zeros((HEAD_DIM, ATT_ROWS), F32)) for _ in range(N_KV))
    if span is not None:
        out = values(0, init)
    else:
        out = lax.fori_loop(0, n_groups, values, init)
    return [out[g][1] * (_branch_gates(gl_ref[g * LANE:(g + 1) * LANE, :], branch) / out[g][0]) for g in range(N_KV)]


SEL_GROUP = 4
WIN_SPAN = WINDOW // ATT_TQ


def _nsa_kernel(q_ref, kvc_ref, kvct_ref, ctab_ref, kvs_ref, kvst_ref, stab_ref, kvw_ref, kvwt_ref, wtab_ref,
                gl_ref, o_ref, sel_sc, ssel_sc, swin_sc, *, n_blocks, n_sel):
    qt = pl.program_id(1)
    hw = Q_PER_KV * LANE
    qs = [_stack_heads(q_ref[0, :, g * hw:(g + 1) * hw]) for g in range(N_KV)]
    o_cmp = _cmp_section(qs, qt, kvc_ref, kvct_ref, ctab_ref, gl_ref, sel_sc, n_blocks=n_blocks, n_sel=n_sel)
    o_sel = _band_section(qs, qt, kvs_ref, kvst_ref, stab_ref, gl_ref, sel_sc, ssel_sc,
                          span=None, group=SEL_GROUP, branch=1)
    o_win = _band_section(qs, qt, kvw_ref, kvwt_ref, wtab_ref, gl_ref, None, swin_sc,
                          span=WIN_SPAN, group=WIN_SPAN + 1, branch=2)
    for g in range(N_KV):
        o_ref[0, :, g * 256:(g + 1) * 256] = _heads_to_rows(o_cmp[g] + o_sel[g] + o_win[g])


def _nsa_attn(q3, kvcmp, kvcmp_t, cmp_tab, kvsw3, kvsw_t, sel_tab, win_tab, gl_t, n_blocks, n_sel):
    bsz, seq, _ = q3.shape
    nqt = seq // ATT_TQ
    n_cmp = kvcmp.shape[1]
    sel_slots = -(-nqt // SEL_GROUP) * SEL_GROUP
    full4 = lambda b, t: (0, 0, 0, 0)
    kv_spec = lambda pair: pl.BlockSpec((1, seq, N_KV * LANE), lambda b, t: (b, 0, pair))
    kvt_spec = lambda pair: pl.BlockSpec((1, N_KV) + kvsw_t.shape[2:], lambda b, t: (b, pair, 0, 0, 0))
    return pl.pallas_call(
        functools.partial(_nsa_kernel, n_blocks=n_blocks, n_sel=n_sel),
        out_shape=jax.ShapeDtypeStruct((bsz, seq, 512), BF16),
        grid=(bsz, nqt),
        in_specs=[pl.BlockSpec((1, ATT_TQ, Q_COLS), lambda b, t: (b, t, 0)),
                  pl.BlockSpec((1, n_cmp, N_KV * LANE), lambda b, t: (b, 0, 0)),
                  pl.BlockSpec((1, N_KV * LANE, n_cmp), lambda b, t: (b, 0, 0)),
                  pl.BlockSpec((N_KV, 1, n_cmp, ATT_ROWS), lambda b, t: (0, t, 0, 0)),
                  kv_spec(0), kvt_spec(0), pl.BlockSpec(sel_tab.shape, full4),
                  kv_spec(1), kvt_spec(1), pl.BlockSpec(win_tab.shape, full4),
                  pl.BlockSpec((GL_ROWS, ATT_TQ), lambda b, t: (0, b * nqt + t))],
        out_specs=pl.BlockSpec((1, ATT_TQ, 512), lambda b, t: (b, t, 0)),
        scratch_shapes=[pltpu.VMEM((N_KV, n_blocks * L_SEL // ATT_TQ, SUBLANE, ATT_TQ), F32),
                        pltpu.VMEM((N_KV, sel_slots * ATT_TQ, ATT_ROWS), F32),
                        pltpu.VMEM((N_KV, (WIN_SPAN + 1) * ATT_TQ, ATT_ROWS), F32)],
        compiler_params=_cparams("parallel", "parallel"),
        name="nsa_attn",
    )(q3, kvcmp, kvcmp_t, cmp_tab, kvsw3, kvsw_t, sel_tab, kvsw3, kvsw_t, win_tab, gl_t)


def _conv_kernel(cur_ref, prev_ref, w_ref, cb_ref, g_ref, b_ref, o_ref, ybuf, ysh, cbuf):
    i = pl.program_id(1)
    dc = o_ref.shape[-1]
    cur = cur_ref[0]
    prev = prev_ref[0]
    ybuf[CONV_HALO:, :] = cur[:, :dc] * jax.nn.sigmoid(cur[:, dc:])
    halo = prev[:, :dc] * jax.nn.sigmoid(prev[:, dc:])
    ybuf[0:CONV_HALO, :] = jnp.where(i > 0, halo, 0.0)
    n_sh = ysh.shape[1]
    for b in range(1, SUBLANE):
        ysh[b - 1] = ybuf[b:b + n_sh, :]
    off = CONV_HALO - (CONV_WIDTH - 1)
    sub = CONV_RC // SUBLANE
    n_rc = CONV_TQ // CONV_RC
    for lc in range(dc // LANE):
        cols = slice(lc * LANE, (lc + 1) * LANE)
        accs = [jnp.zeros((sub, SUBLANE, LANE), F32) for _ in range(n_rc)]
        for k in range(CONV_WIDTH):
            b = (off + k) % SUBLANE
            wk = w_ref[k, :, cols][None]
            for rc in range(n_rc):
                r0 = rc * CONV_RC + (off + k) - b
                src = ybuf[r0:r0 + CONV_RC, cols] if b == 0 else ysh[b - 1, r0:r0 + CONV_RC, cols]
                accs[rc] = accs[rc] + wk * src.reshape(sub, SUBLANE, LANE)
        for rc in range(n_rc):
            cbuf[rc * CONV_RC:(rc + 1) * CONV_RC, cols] = accs[rc].reshape(CONV_RC, LANE)
    z = cbuf[...] + cb_ref[...]
    mu = jnp.mean(z, axis=-1, keepdims=True)
    zc = z - mu
    var = jnp.mean(zc * zc, axis=-1, keepdims=True)
    y = zc * lax.rsqrt(var + LN_EPS) * g_ref[...] + b_ref[...]
    o_ref[0] = jax.nn.silu(y).astype(BF16)


def _conv_branch(cv3, w, cb, g, b):
    bsz, seq, two_dc = cv3.shape
    dc = two_dc // 2
    per = CONV_TQ // CONV_HALO
    const = lambda bb, i: (0, 0)
    return pl.pallas_call(
        _conv_kernel,
        out_shape=jax.ShapeDtypeStruct((bsz, seq, dc), BF16),
        grid=(bsz, seq // CONV_TQ),
        in_specs=[pl.BlockSpec((1, CONV_TQ, two_dc), lambda bb, i: (bb, i, 0)),
                  pl.BlockSpec((1, CONV_HALO, two_dc), lambda bb, i: (bb, jnp.maximum(i * per - 1, 0), 0)),
                  pl.BlockSpec(w.shape, lambda bb, i: (0, 0, 0)), pl.BlockSpec((1, dc), const),
                  pl.BlockSpec((1, dc), const), pl.BlockSpec((1, dc), const)],
        out_specs=pl.BlockSpec((1, CONV_TQ, dc), lambda bb, i: (bb, i, 0)),
        scratch_shapes=[pltpu.VMEM((CONV_HALO + CONV_TQ, dc), F32),
                        pltpu.VMEM((SUBLANE - 1, CONV_HALO + CONV_TQ - SUBLANE, dc), F32),
                        pltpu.VMEM((CONV_TQ, dc), F32)],
        compiler_params=_cparams("parallel", "parallel"),
        name="conv_branch",
    )(cv3, cv3, w, cb, g, b)


def _outproj_kernel(x_ref, gate_ref, g_ref, b_ref, nsa_ref, ocv_ref, w_ref, o_ref, *, alpha):
    half = nsa_ref.shape[1]
    y = _dot(nsa_ref[...], w_ref[0:half, :]) + _dot(ocv_ref[...], w_ref[half:, :])
    o_ref[...] = _res_ln(x_ref[...], gate_ref[0], y, g_ref[...], b_ref[...], alpha)


def _outproj(x2, gate, lng, lnb, o_nsa, o_conv, w, seq, tm, alpha):
    n, d = x2.shape
    per_b = seq // tm
    row = lambda i: (i, 0)
    const = lambda i: (0, 0)
    return pl.pallas_call(
        functools.partial(_outproj_kernel, alpha=alpha),
        out_shape=jax.ShapeDtypeStruct((n, d), F32),
        grid=(n // tm,),
        in_specs=[pl.BlockSpec((tm, d), row), pl.BlockSpec((1, 1, d), lambda i: (i // per_b, 0, 0)),
                  pl.BlockSpec((1, d), const), pl.BlockSpec((1, d), const),
                  pl.BlockSpec((tm, o_nsa.shape[1]), row), pl.BlockSpec((tm, o_conv.shape[1]), row),
                  pl.BlockSpec(w.shape, const)],
        out_specs=pl.BlockSpec((tm, d), row),
        compiler_params=_cparams("parallel"),
        name="out_proj_ln",
    )(x2, gate, lng, lnb, o_nsa, o_conv, w)


def _ffn1_kernel(x_ref, sc_ref, sh_ref, wg_ref, wu_ref, o_ref, *, tn):
    h = (x_ref[...] * (1.0 + sc_ref[0]) + sh_ref[0]).astype(BF16)
    for j in range(o_ref.shape[1] // tn):
        cols = slice(j * tn, (j + 1) * tn)
        gate = _dot(h, wg_ref[:, cols])
        up = _dot(h, wu_ref[:, cols])
        o_ref[:, cols] = (jax.nn.silu(gate) * up).astype(BF16)


def _ffn1(x2, scale, shift, wg, wu, seq, tm):
    n, d = x2.shape
    dff = wg.shape[1]
    per_b = seq // tm
    row = lambda i: (i, 0)
    mod = lambda i: (i // per_b, 0, 0)
    const = lambda i: (0, 0)
    return pl.pallas_call(
        functools.partial(_ffn1_kernel, tn=256),
        out_shape=jax.ShapeDtypeStruct((n, dff), BF16),
        grid=(n // tm,),
        in_specs=[pl.BlockSpec((tm, d), row), pl.BlockSpec((1, 1, d), mod), pl.BlockSpec((1, 1, d), mod),
                  pl.BlockSpec((d, dff), const), pl.BlockSpec((d, dff), const)],
        out_specs=pl.BlockSpec((tm, dff), row),
        compiler_params=_cparams("parallel"),
        name="ffn_up",
    )(x2, scale, shift, wg, wu)


def _ffn2_kernel(x_ref, gate_ref, g_ref, b_ref, a_ref, w_ref, o_ref, *, alpha):
    y = _dot(a_ref[...], w_ref[...])
    o_ref[...] = _res_ln(x_ref[...], gate_ref[0], y, g_ref[...], b_ref[...], alpha)


def _ffn2(x2, gate, lng, lnb, act, w, seq, tm, alpha):
    n, d = x2.shape
    per_b = seq // tm
    row = lambda i: (i, 0)
    const = lambda i: (0, 0)
    return pl.pallas_call(
        functools.partial(_ffn2_kernel, alpha=alpha),
        out_shape=jax.ShapeDtypeStruct((n, d), F32),
        grid=(n // tm,),
        in_specs=[pl.BlockSpec((tm, d), row), pl.BlockSpec((1, 1, d), lambda i: (i // per_b, 0, 0)),
                  pl.BlockSpec((1, d), const), pl.BlockSpec((1, d), const),
                  pl.BlockSpec((tm, act.shape[1]), row), pl.BlockSpec(w.shape, const)],
        out_specs=pl.BlockSpec((tm, d), row),
        compiler_params=_cparams("parallel"),
        name="ffn_down_ln",
    )(x2, gate, lng, lnb, act, w)


MOE_TM = 1024
MOE_FC = 512
GATHER_ROWS = 1024


def _router_kernel(x_ref, sc_ref, sh_ref, rw_ref, h_ref, ti_ref, tw_ref):
    hf = x_ref[...] * (1.0 + sc_ref[0]) + sh_ref[0]
    h_ref[...] = hf
    lane = lax.broadcasted_iota(jnp.int32, (hf.shape[0], LANE), 1)
    logits = jnp.dot(hf, rw_ref[...], precision=HIGHEST, preferred_element_type=F32)
    logits = jnp.where(lane < N_EXPERTS, logits, -jnp.inf)
    m1 = jnp.max(logits, axis=-1, keepdims=True)
    i1 = jnp.min(jnp.where(logits == m1, lane, LANE), axis=-1, keepdims=True)
    rest = jnp.where(lane == i1, -jnp.inf, logits)
    m2 = jnp.max(rest, axis=-1, keepdims=True)
    i2 = jnp.min(jnp.where(rest == m2, lane, LANE), axis=-1, keepdims=True)
    e2 = jnp.exp(m2 - m1)
    ti_ref[...] = jnp.where(lane == 0, i1, jnp.where(lane == 1, i2, 0))
    tw_ref[...] = jnp.where(lane == 0, 1.0 / (1.0 + e2), jnp.where(lane == 1, e2 / (1.0 + e2), 0.0))


def _router(x2, scale, shift, router_p, seq, tm):
    n, d = x2.shape
    per_b = seq // tm
    row = lambda i: (i, 0)
    mod = lambda i: (i // per_b, 0, 0)
    return pl.pallas_call(
        _router_kernel,
        out_shape=(jax.ShapeDtypeStruct((n, d), F32), jax.ShapeDtypeStruct((n, LANE), jnp.int32),
                   jax.ShapeDtypeStruct((n, LANE), F32)),
        grid=(n // tm,),
        in_specs=[pl.BlockSpec((tm, d), row), pl.BlockSpec((1, 1, d), mod), pl.BlockSpec((1, 1, d), mod),
                  pl.BlockSpec(router_p.shape, lambda i: (0, 0))],
        out_specs=(pl.BlockSpec((tm, d), row), pl.BlockSpec((tm, LANE), row), pl.BlockSpec((tm, LANE), row)),
        compiler_params=_cparams("parallel"),
        name="moe_router",
    )(x2, scale, shift, router_p)


def _scatter_rows_kernel(pos_ref, h_ref, init_ref, o_ref, sem):
    del init_ref
    rows = h_ref.shape[0]
    n = pl.num_programs(0) * rows
    base = pl.program_id(0) * rows

    def issue(r, carry):
        row = h_ref.at[pl.ds(r, 1)]
        pltpu.make_async_copy(row, o_ref.at[pl.ds(pos_ref[base + r], 1)], sem).start(priority=0)
        pltpu.make_async_copy(row, o_ref.at[pl.ds(pos_ref[n + base + r], 1)], sem).start(priority=1)
        return carry

    lax.fori_loop(0, rows, issue, 0, unroll=8)
    for _ in range(2):
        pltpu.make_async_copy(h_ref, o_ref.at[pl.ds(0, rows)], sem).wait()


def _scatter_rows(pos, h, init):
    n, d = h.shape
    return pl.pallas_call(
        _scatter_rows_kernel,
        out_shape=jax.ShapeDtypeStruct(init.shape, h.dtype),
        grid_spec=pltpu.PrefetchScalarGridSpec(
            num_scalar_prefetch=1,
            grid=(n // GATHER_ROWS,),
            in_specs=[pl.BlockSpec((GATHER_ROWS, d), lambda i, pos_ref: (i, 0)),
                      pl.BlockSpec(memory_space=pl.ANY)],
            out_specs=pl.BlockSpec(memory_space=pl.ANY),
            scratch_shapes=[pltpu.SemaphoreType.DMA]),
        input_output_aliases={2: 0},
        compiler_params=_cparams("arbitrary"),
        name="row_scatter",
    )(pos, h, init)


def _expert_up_kernel(te_ref, nu_ref, x_ref, wg_ref, wu_ref, o_ref, xb_sc):
    i = pl.program_id(0)
    f = pl.program_id(1)

    @pl.when(i < nu_ref[0])
    def _compute():
        @pl.when(f == 0)
        def _cast():
            xb_sc[...] = x_ref[...].astype(BF16)
        h = xb_sc[...]
        gate = _dot(h, wg_ref[0, 0])
        up = _dot(h, wu_ref[0, 0])
        o_ref[...] = (jax.nn.silu(gate) * up).astype(BF16)

    @pl.when(i >= nu_ref[0])
    def _unused_tile():
        o_ref[...] = jnp.zeros_like(o_ref)


def _expert_up(tile_expert, n_used, xg, w1b, layer):
    p_rows, d = xg.shape
    nf = w1b.shape[3] // (2 * MOE_FC)
    return pl.pallas_call(
        _expert_up_kernel,
        out_shape=jax.ShapeDtypeStruct((p_rows, nf * MOE_FC), BF16),
        grid_spec=pltpu.PrefetchScalarGridSpec(
            num_scalar_prefetch=2,
            grid=(p_rows // MOE_TM, nf),
            in_specs=[pl.BlockSpec((MOE_TM, d), lambda i, f, te, nu: (i, 0)),
                      pl.BlockSpec((1, 1, d, MOE_FC), lambda i, f, te, nu: (layer, te[i], 0, f)),
                      pl.BlockSpec((1, 1, d, MOE_FC), lambda i, f, te, nu: (layer, te[i], 0, nf + f))],
            out_specs=pl.BlockSpec((MOE_TM, MOE_FC), lambda i, f, te, nu: (i, f)),
            scratch_shapes=[pltpu.VMEM((MOE_TM, d), BF16)]),
        compiler_params=_cparams("parallel", "arbitrary"),
        name="moe_up",
    )(tile_expert, n_used, xg, w1b, w1b)


def _expert_down_kernel(te_ref, nu_ref, a_ref, w_ref, o_ref):
    i = pl.program_id(0)

    @pl.when(i < nu_ref[0])
    def _compute():
        o_ref[...] = _dot(a_ref[...], w_ref[0, 0])

    @pl.when(i >= nu_ref[0])
    def _unused_tile():
        o_ref[...] = jnp.zeros_like(o_ref)


def _expert_down(tile_expert, n_used, act, w2b, layer):
    p_rows, dffe = act.shape
    d = w2b.shape[3]
    return pl.pallas_call(
        _expert_down_kernel,
        out_shape=jax.ShapeDtypeStruct((p_rows, d), F32),
        grid_spec=pltpu.PrefetchScalarGridSpec(
            num_scalar_prefetch=2,
            grid=(p_rows // MOE_TM,),
            in_specs=[pl.BlockSpec((MOE_TM, dffe), lambda i, te, nu: (i, 0)),
                      pl.BlockSpec((1, 1, dffe, d), lambda i, te, nu: (layer, te[i], 0, 0))],
            out_specs=pl.BlockSpec((MOE_TM, d), lambda i, te, nu: (i, 0))),
        compiler_params=_cparams("parallel"),
        name="moe_down",
    )(tile_expert, n_used, act, w2b)


def _moe_combine_kernel(pos_ref, x_ref, gate_ref, g_ref, b_ref, tw_ref, y_ref, o_ref, ybuf, sems, *, alpha):
    i = pl.program_id(0)
    nt = pl.num_programs(0)
    tm = x_ref.shape[0]
    n = nt * tm

    def fetch(step, slot):
        def issue(r, carry):
            t = step * tm + r
            pltpu.make_async_copy(y_ref.at[pl.ds(pos_ref[t], 1)], ybuf.at[slot, 0, pl.ds(r, 1)],
                                  sems.at[slot]).start(priority=0)
            pltpu.make_async_copy(y_ref.at[pl.ds(pos_ref[n + t], 1)], ybuf.at[slot, 1, pl.ds(r, 1)],
                                  sems.at[slot]).start(priority=1)
            return carry
        lax.fori_loop(0, tm, issue, 0, unroll=8)

    @pl.when(i == 0)
    def _first():
        fetch(0, 0)

    @pl.when(i + 1 < nt)
    def _next():
        fetch(i + 1, (i + 1) % 2)

    slot = i % 2
    for choice in range(2):
        pltpu.make_async_copy(y_ref.at[pl.ds(0, tm)], ybuf.at[slot, choice], sems.at[slot]).wait()
    tw = tw_ref[...]
    y = tw[:, 0:1] * ybuf[slot, 0] + tw[:, 1:2] * ybuf[slot, 1]
    o_ref[...] = _res_ln(x_ref[...], gate_ref[0], y, g_ref[...], b_ref[...], alpha)


def _moe_combine(pos, x2, gate, lng, lnb, tw, y, seq, tm, alpha):
    n, d = x2.shape
    per_b = seq // tm
    row = lambda i, p: (i, 0)
    const = lambda i, p: (0, 0)
    return pl.pallas_call(
        functools.partial(_moe_combine_kernel, alpha=alpha),
        out_shape=jax.ShapeDtypeStruct((n, d), F32),
        grid_spec=pltpu.PrefetchScalarGridSpec(
            num_scalar_prefetch=1,
            grid=(n // tm,),
            in_specs=[pl.BlockSpec((tm, d), row), pl.BlockSpec((1, 1, d), lambda i, p: (i // per_b, 0, 0)),
                      pl.BlockSpec((1, d), const), pl.BlockSpec((1, d), const), pl.BlockSpec((tm, LANE), row),
                      pl.BlockSpec(memory_space=pl.ANY)],
            out_specs=pl.BlockSpec((tm, d), row),
            scratch_shapes=[pltpu.VMEM((2, 2, tm, d), F32), pltpu.SemaphoreType.DMA((2,))]),
        compiler_params=_cparams("arbitrary"),
        name="moe_combine_ln",
    )(pos, x2, gate, lng, lnb, tw, y)


def _route_tables(ti, n_rows):
    e_flat = jnp.concatenate([ti[:, 0], ti[:, 1]])
    onehot = (e_flat[:, None] == jnp.arange(N_EXPERTS, dtype=jnp.int32)[None, :]).astype(jnp.int32)
    cum = jnp.cumsum(onehot, axis=0)
    counts = cum[-1]
    padded = (counts + MOE_TM - 1) // MOE_TM * MOE_TM
    ends = jnp.cumsum(padded)
    pos = jnp.sum(onehot * (cum - 1 + (ends - padded)[None, :]), axis=1)
    tile_start = jnp.arange(n_rows // MOE_TM, dtype=jnp.int32) * MOE_TM
    tile_expert = jnp.minimum(jnp.sum((tile_start[:, None] >= ends[None, :]).astype(jnp.int32), axis=1),
                              N_EXPERTS - 1)
    return pos.astype(jnp.int32), tile_expert, (ends[-1:] // MOE_TM).astype(jnp.int32)


def _moe(x2, scale, shift, gate, lng, lnb, router_p, w1b, w2b, layer, seq, tm, alpha, grouped_buf):
    n, d = x2.shape
    n_rows = 2 * n + N_EXPERTS * MOE_TM
    assert n % GATHER_ROWS == 0
    if grouped_buf is None:
        grouped_buf = jnp.zeros((n_rows, d), F32)
    h, ti, tw = _router(x2, scale, shift, router_p, seq, tm)
    pos, tile_expert, n_used = _route_tables(ti, n_rows)
    xg = _scatter_rows(pos, h, grouped_buf)
    act = _expert_up(tile_expert, n_used, xg, w1b, layer)
    y = _expert_down(tile_expert, n_used, act, w2b, layer)
    return _moe_combine(pos, x2, gate, lng, lnb, tw, y, seq, tm, alpha), xg


def _rel_bucket_np(dist):
    n = np.maximum(dist, 0)
    max_exact = NUM_BUCKETS // 2
    nf = np.maximum(n, 1).astype(np.float32)
    large = max_exact + (np.log(nf / np.float32(max_exact)) / np.float32(math.log(MAX_DISTANCE / max_exact))
                         * np.float32(NUM_BUCKETS - max_exact)).astype(np.int32)
    large = np.minimum(large, NUM_BUCKETS - 1)
    return np.where(n < max_exact, n, large).astype(np.int32)


def _bias_lookup(rel_bias, dist, valid):
    onehot = _rel_bucket_np(dist)[..., None] == np.arange(NUM_BUCKETS)
    vals = jnp.einsum('...k,kh->...h', jnp.asarray(onehot).astype(F32), rel_bias, precision=HIGHEST)
    return jnp.where(jnp.asarray(valid)[..., None], vals * LOG2E, NEG_INF)


def _key_major(vals, add_masked_tile=False):
    n_tiles, tq, tk, _ = vals.shape
    if add_masked_tile:
        vals = jnp.concatenate([vals, jnp.full((1,) + vals.shape[1:], NEG_INF, vals.dtype)], axis=0)
        n_tiles += 1
    vals = vals.reshape(n_tiles, tq, tk, N_KV, Q_PER_KV).transpose(3, 0, 2, 4, 1)
    return vals.reshape(N_KV, n_tiles, tk, Q_PER_KV * tq).astype(F32)


def _band_dist(n_delta):
    i = np.arange(ATT_TQ)[None, :, None]
    k = np.arange(ATT_TQ)[None, None, :]
    return ATT_TQ * np.arange(n_delta)[:, None, None] + i - k


def _cmp_bias_table(rel_bias, n_qt, n_cmp_pad):
    per_tile = ATT_TQ // CMP_STRIDE
    shift = per_tile * (n_qt - 1)
    rel_n = np.arange(n_cmp_pad + shift) - shift
    dist = np.arange(ATT_TQ)[:, None] - (CMP_STRIDE * rel_n[None, :] + L_CMP - 1)
    base = _bias_lookup(rel_bias, dist, dist >= 0)
    tiles = [base[:, shift - per_tile * qt: shift - per_tile * qt + n_cmp_pad] for qt in range(n_qt)]
    return _key_major(jnp.stack(tiles))


def _inproj_weight(w_in):
    depth, d, _ = w_in.shape
    d_nsa = N_HEADS * HEAD_DIM
    d_kv = N_KV * HEAD_DIM
    base_ks = d_nsa + 2 * d_kv
    base_gl = d_nsa + 6 * d_kv
    base_cv = base_gl + N_HEADS * N_BRANCH
    wq = jnp.pad(w_in[:, :, :d_nsa].reshape(depth, d, N_HEADS, HEAD_DIM),
                 ((0, 0), (0, 0), (0, 0), (0, LANE - HEAD_DIM))).reshape(depth, d, Q_COLS)
    parts = [wq, w_in[:, :, d_nsa:base_ks]]
    for pair in range(2):
        kbase = base_ks + pair * 2 * d_kv
        for g in range(N_KV):
            parts.append(w_in[:, :, kbase + g * HEAD_DIM: kbase + (g + 1) * HEAD_DIM])
            parts.append(w_in[:, :, kbase + d_kv + g * HEAD_DIM: kbase + d_kv + (g + 1) * HEAD_DIM])
    parts.append(w_in[:, :, base_cv:])
    gl = w_in[:, :, base_gl:base_cv].reshape(depth, d, N_KV, Q_PER_KV, N_BRANCH).transpose(0, 2, 4, 3, 1)
    gl = jnp.pad(gl.reshape(depth, N_KV, N_BRANCH * Q_PER_KV, d),
                 ((0, 0), (0, 0), (0, LANE - N_BRANCH * Q_PER_KV), (0, 0)))
    return jnp.concatenate(parts, axis=-1).astype(BF16), gl.reshape(depth, GL_ROWS, d).astype(BF16)


def kernel(x, c, w_in, cmp_pos, cmp_w1, cmp_w2, conv_w, conv_b, conv_ln_g, conv_ln_b, w_out, rel_bias,
           ada_w, ada_b, ln_g, ln_b, ffn_w1, ffn_w2, router_w, moe_w1, moe_w2):
    bsz, seq, d = x.shape
    depth = w_in.shape[0]
    n = bsz * seq
    alpha = (2 * depth) ** 0.25
    n_blocks = seq // L_SEL
    n_sel = min(N_SEL, n_blocks)
    n_cmp = (seq - L_CMP) // CMP_STRIDE + 1
    nrow = seq // CMP_STRIDE
    assert d == 1024 and seq % 512 == 0 and nrow == LANE and n_cmp == nrow - 1
    tm = 1024

    n_qt = seq // ATT_TQ
    d_win = _band_dist(WINDOW // ATT_TQ + 1)
    win_tab = _key_major(_bias_lookup(rel_bias, d_win, (d_win >= 0) & (d_win < WINDOW)), add_masked_tile=True)
    d_sel = _band_dist(3)
    assert _rel_bucket_np(d_sel[2]).min() == NUM_BUCKETS - 1
    sel_tab = _key_major(_bias_lookup(rel_bias, d_sel, d_sel >= 0), add_masked_tile=True)
    cmp_tab = _cmp_bias_table(rel_bias, n_qt, nrow)

    w_in_p, w_gl_t = _inproj_weight(w_in)
    eye2 = jnp.eye(2, dtype=F32)
    w1r = cmp_w1.reshape(depth, 2, L_CMP, HEAD_DIM, HEAD_DIM).astype(BF16)
    zero_blk = jnp.zeros((depth, L_CMP, HEAD_DIM, HEAD_DIM), BF16)
    row_blocks = []
    for s in range(2):
        for g in range(N_KV):
            col_blocks = [zero_blk] * (2 * N_KV)
            col_blocks[g * 2 + s] = w1r[:, s]
            row_blocks.append(jnp.concatenate(col_blocks, axis=-1))
    w1x = jnp.concatenate(row_blocks, axis=2)
    half = L_CMP // 2
    w_lo = w1x[:, :half]
    w_hi = w1x[:, half:]
    w2bd = jnp.einsum('zsde,st,gh->zgsdhte', cmp_w2, eye2, eye2).reshape(depth, 4 * HEAD_DIM, 4 * HEAD_DIM).astype(BF16)
    posx = jnp.broadcast_to(cmp_pos[:, :, :, None, :], (depth, 2, L_CMP, N_KV, HEAD_DIM)).transpose(0, 2, 1, 3, 4)
    pos_lo = posx[:, :half].reshape(depth, half, 4 * HEAD_DIM)
    pos_hi = posx[:, half:].reshape(depth, half, 4 * HEAD_DIM)
    conv_wp = jnp.broadcast_to(conv_w[:, :, None, :], conv_w.shape[:2] + (SUBLANE, conv_w.shape[2]))
    w_out_b = w_out.astype(BF16)
    dff = ffn_w2.shape[1]
    ffn_wg = ffn_w1[:, :, :dff].astype(BF16)
    ffn_wu = ffn_w1[:, :, dff:].astype(BF16)
    ffn_w2b = ffn_w2.astype(BF16)
    router_p = jnp.pad(router_w, ((0, 0), (0, 0), (0, LANE - N_EXPERTS)))
    moe_w1b = moe_w1.astype(BF16)
    moe_w2b = moe_w2.astype(BF16)

    mod = _ada(c, ada_w, ada_b).reshape(depth, 6, bsz, 1, d)
    x2 = x.reshape(n, d)
    grouped_buf = None
    for l in range(depth):
        shift1, scale1, gate1, shift2, scale2, gate2 = [mod[l, k] for k in range(6)]
        q, kvc, kvsw, gl3, cv = _inproj(x2, scale1, shift1, w_in_p[l], w_gl_t[l], seq, tm)
        q3 = q.reshape(bsz, seq, Q_COLS)
        kvsw3 = kvsw.reshape(bsz, seq, KVSW_COLS)
        kvsw_t = kvsw3.reshape(bsz, n_qt, ATT_TQ, 2 * N_KV, LANE).transpose(0, 3, 1, 4, 2)
        kvcmp, kvcmp_t = _compress(kvc.reshape(bsz, seq, KVC_COLS), pos_lo[l], pos_hi[l], w_lo[l], w_hi[l],
                                   w2bd[l])
        o_nsa = _nsa_attn(q3, kvcmp, kvcmp_t, cmp_tab, kvsw3, kvsw_t, sel_tab, win_tab, gl3, n_blocks, n_sel)
        o_conv = _conv_branch(cv.reshape(bsz, seq, 1024), conv_wp[l], conv_b[l][None], conv_ln_g[l][None],
                              conv_ln_b[l][None])
        x2 = _outproj(x2, gate1, ln_g[l, 0][None], ln_b[l, 0][None], o_nsa.reshape(n, 512), o_conv.reshape(n, 512),
                      w_out_b[l], seq, tm, alpha)
        if l % 2 == 0:
            act = _ffn1(x2, scale2, shift2, ffn_wg[l // 2], ffn_wu[l // 2], seq, tm)
            x2 = _ffn2(x2, gate2, ln_g[l, 1][None], ln_b[l, 1][None], act, ffn_w2b[l // 2], seq, tm, alpha)
        else:
            x2, grouped_buf = _moe(x2, scale2, shift2, gate2, ln_g[l, 1][None], ln_b[l, 1][None], router_p[l // 2],
                                   moe_w1b, moe_w2b, l // 2, seq, tm, alpha, grouped_buf)
    return x2.reshape(bsz, seq, d)
```
